```python
import math
import jax, jax.numpy as jnp
from jax import lax
import numpy as np

D_MODEL = 4096
BATCH = 32
SEQ = 256
DEPTH = 4
DEC_BATCH = 4
DEC_SEQ = 1024
PAST_LEN = 512

GRID_W = 64
HEADS_PER_GROUP = D_MODEL // 512
MLA_HEADS = HEADS_PER_GROUP
MLA_Q_RANK = 3 * D_MODEL // 16
MLA_KV_RANK = D_MODEL // 8
MLA_NOPE_DIM = 128
MLA_ROPE_DIM = 64
MLA_V_DIM = 128
DIFF_HEADS = HEADS_PER_GROUP
DIFF_HEAD_DIM = 64
GQA_HEADS = HEADS_PER_GROUP
GQA_KV_HEADS = max(1, HEADS_PER_GROUP // 4)
GQA_HEAD_DIM = 128
HGRN_HEADS = HEADS_PER_GROUP
HGRN_KEY_DIM = 128
HGRN_VAL_DIM = 128
HGRN_CHUNK = 16
IN_SPLITS = (
    MLA_Q_RANK, MLA_KV_RANK, MLA_ROPE_DIM,
    DIFF_HEADS * 2 * DIFF_HEAD_DIM, DIFF_HEADS * 2 * DIFF_HEAD_DIM, DIFF_HEADS * 2 * DIFF_HEAD_DIM,
    GQA_HEADS * GQA_HEAD_DIM, GQA_KV_HEADS * GQA_HEAD_DIM, GQA_KV_HEADS * GQA_HEAD_DIM,
    HGRN_HEADS * HGRN_KEY_DIM, HGRN_HEADS * HGRN_VAL_DIM, HGRN_HEADS * HGRN_KEY_DIM,
    HGRN_HEADS * HGRN_KEY_DIM, HGRN_HEADS * HGRN_VAL_DIM,
)
IN_WIDTH = sum(IN_SPLITS)
MIX_WIDTH = (MLA_HEADS * MLA_V_DIM + DIFF_HEADS * 2 * DIFF_HEAD_DIM
             + GQA_HEADS * GQA_HEAD_DIM + HGRN_HEADS * HGRN_VAL_DIM)
D_FF = 7 * D_MODEL // 2
N_EXPERTS = 8
TOP_K = 2
EXPERT_FF = 7 * D_MODEL // 4
N_DENSE = (DEPTH + 1) // 2
N_MOE = DEPTH // 2
Q_BLOCK = 128
ROPE_THETA = 10000.0
ALPHA = (2 * DEPTH) ** 0.25
BETA = (8 * DEPTH) ** -0.25
LN_EPS = 1e-5
RMS_EPS = 1e-6

kernel_name = 'hybrid_diffusion_parallel_heads_step'


def rms_norm(x, g):
    xf = x.astype(jnp.float32)
    y = xf * lax.rsqrt(jnp.mean(xf * xf, axis=-1, keepdims=True) + RMS_EPS)
    return (y * g.astype(jnp.float32)).astype(x.dtype)


def layer_norm(x, g, b):
    xf = x.astype(jnp.float32)
    xc = xf - jnp.mean(xf, axis=-1, keepdims=True)
    y = xc * lax.rsqrt(jnp.mean(xc * xc, axis=-1, keepdims=True) + LN_EPS)
    return (y * g.astype(jnp.float32) + b.astype(jnp.float32)).astype(x.dtype)


def _rope_axis(x, pos):
    n = x.shape[-1] // 2
    inv = ROPE_THETA ** (-jnp.arange(n, dtype=jnp.float32) / n)
    ang = pos[:, None] * inv[None, :]
    cos = jnp.cos(ang)[None, :, None, :]
    sin = jnp.sin(ang)[None, :, None, :]
    x1, x2 = x[..., :n], x[..., n:]
    return jnp.concatenate([x1 * cos - x2 * sin, x2 * cos + x1 * sin], axis=-1)


def axial_rope(x, rows, cols):
    half = x.shape[-1] // 2
    xf = x.astype(jnp.float32)
    y = jnp.concatenate([_rope_axis(xf[..., :half], rows), _rope_axis(xf[..., half:], cols)], axis=-1)
    return y.astype(x.dtype)


def attention(q, k, v, scale):
    b, t, h, dk = q.shape
    hk, dv = k.shape[2], v.shape[-1]
    g = h // hk
    nb = t // Q_BLOCK
    qb = q.reshape(b, nb, Q_BLOCK, hk, g, dk).transpose(1, 0, 2, 3, 4, 5)

    def block(qi):
        s = jnp.einsum('bqhgd,bkhd->bhgqk', qi, k).astype(jnp.float32) * scale
        p = jax.nn.softmax(s, axis=-1).astype(v.dtype)
        return jnp.einsum('bhgqk,bkhe->bqhge', p, v)

    o = lax.map(block, qb)
    return o.transpose(1, 0, 2, 3, 4, 5).reshape(b, t, h, dv)


def gla_scan(q, k, v, logf, s0):
    b, t, h, dk = q.shape
    dv = v.shape[-1]
    n = t // HGRN_CHUNK

    def chunks(a):
        return a.astype(jnp.float32).reshape(b, n, HGRN_CHUNK, h, a.shape[-1]).transpose(1, 0, 3, 2, 4)

    causal = jnp.tril(jnp.ones((HGRN_CHUNK, HGRN_CHUNK), dtype=bool))[:, :, None]

    def step(state, xs):
        qc, kc, vc, gc = xs
        cum = jnp.cumsum(gc, axis=2)
        rel = cum[:, :, :, None, :] - cum[:, :, None, :, :]
        decay = jnp.exp(jnp.where(causal, rel, -jnp.inf))
        att = jnp.einsum('bhtd,bhsd,bhtsd->bhts', qc, kc, decay)
        out = (jnp.einsum('bhtd,bhde->bhte', qc * jnp.exp(cum), state)
               + jnp.einsum('bhts,bhse->bhte', att, vc))
        last = cum[:, :, -1:, :]
        new_state = (jnp.exp(last[:, :, 0, :])[..., None] * state
                     + jnp.einsum('bhsd,bhse->bhde', kc * jnp.exp(last - cum), vc))
        return new_state, out

    state, out = lax.scan(step, s0.astype(jnp.float32), (chunks(q), chunks(k), chunks(v), chunks(logf)))
    return out.transpose(1, 0, 3, 2, 4).reshape(b, t, h, dv), state


def hgrn_lower_bounds(gamma):
    cs = jnp.cumsum(jax.nn.softmax(gamma.astype(jnp.float32), axis=0), axis=0)
    return cs - cs[0:1]


def hgrn_forget(z, lb):
    zf = z.astype(jnp.float32)
    logf = jnp.logaddexp(jnp.log(lb), jnp.log1p(-lb) + jax.nn.log_sigmoid(zf))
    k = (1.0 - lb) * jax.nn.sigmoid(-zf)
    return logf, k


def split_columns(h):
    out, start = [], 0
    for w in IN_SPLITS:
        out.append(h[..., start:start + w])
        start += w
    return out


def token_mixer(u, P, l, lb_f, lb_b, pos, ctx):
    b, t, _ = u.shape
    (a_cq, a_ckv, a_kr, b_q, b_k, b_v, c_q, c_k, c_v,
     d_q, d_i, d_f, d_fb, d_g) = split_columns(u @ P['w_in'][l])
    if pos is None:
        rope = lambda z: z
    else:
        rope = lambda z: axial_rope(z, pos[0], pos[1])
    cat = lambda c0, z: z if ctx is None else jnp.concatenate([c0, z], axis=1)

    cq = rms_norm(a_cq, P['mla_g_cq'][l])
    q = (cq @ P['mla_w_uq'][l]).reshape(b, t, MLA_HEADS, MLA_NOPE_DIM + MLA_ROPE_DIM)
    q = jnp.concatenate([q[..., :MLA_NOPE_DIM], rope(q[..., MLA_NOPE_DIM:])], axis=-1)
    ckv = rms_norm(a_ckv, P['mla_g_ckv'][l])
    kr = rope(a_kr[:, :, None, :])[:, :, 0, :]
    ckv_all = cat(None if ctx is None else ctx[0], ckv)
    kr_all = cat(None if ctx is None else ctx[1], kr)
    kv = (ckv_all @ P['mla_w_ukv'][l]).reshape(b, -1, MLA_HEADS, MLA_NOPE_DIM + MLA_V_DIM)
    n_keys = kv.shape[1]
    k = jnp.concatenate([kv[..., :MLA_NOPE_DIM],
                         jnp.broadcast_to(kr_all[:, :, None, :], (b, n_keys, MLA_HEADS, MLA_ROPE_DIM))], axis=-1)
    o_a = attention(q, k, kv[..., MLA_NOPE_DIM:], (MLA_NOPE_DIM + MLA_ROPE_DIM) ** -0.5)

    dd = DIFF_HEAD_DIM
    dq = rope(b_q.reshape(b, t, 2 * DIFF_HEADS, dd)).reshape(b, t, DIFF_HEADS, 2 * dd)
    dk = rope(b_k.reshape(b, t, 2 * DIFF_HEADS, dd)).reshape(b, t, DIFF_HEADS, 2 * dd)
    dv = b_v.reshape(b, t, DIFF_HEADS, 2 * dd)
    dk_all = cat(None if ctx is None else ctx[2], dk)
    dv_all = cat(None if ctx is None else ctx[3], dv)
    a1 = attention(dq[..., :dd], dk_all[..., :dd], dv_all, dd ** -0.5)
    a2 = attention(dq[..., dd:], dk_all[..., dd:], dv_all, dd ** -0.5)
    lam_p = P['diff_lambda'][l].astype(jnp.float32)
    lam_init = 0.8 - 0.6 * math.exp(-0.3 * l)
    lam = (jnp.exp(jnp.sum(lam_p[0] * lam_p[1])) - jnp.exp(jnp.sum(lam_p[2] * lam_p[3])) + lam_init)
    o_b = rms_norm(a1 - lam.astype(a1.dtype) * a2, P['diff_subln_g'][l]) * (1.0 - lam_init)

    gq = rope(rms_norm(c_q.reshape(b, t, GQA_HEADS, GQA_HEAD_DIM), P['gqa_g_q'][l]))
    gk = rope(rms_norm(c_k.reshape(b, t, GQA_KV_HEADS, GQA_HEAD_DIM), P['gqa_g_k'][l]))
    gv = c_v.reshape(b, t, GQA_KV_HEADS, GQA_HEAD_DIM)
    gk_all = cat(None if ctx is None else ctx[4], gk)
    gv_all = cat(None if ctx is None else ctx[5], gv)
    o_c = attention(gq, gk_all, gv_all, GQA_HEAD_DIM ** -0.5)

    hq = jax.nn.silu(d_q).reshape(b, t, HGRN_HEADS, HGRN_KEY_DIM) * (HGRN_KEY_DIM ** -0.5)
    hi = d_i.reshape(b, t, HGRN_HEADS, HGRN_VAL_DIM)
    logf_f, k_f = hgrn_forget(d_f.reshape(b, t, HGRN_HEADS, HGRN_KEY_DIM),
                              lb_f[l].reshape(HGRN_HEADS, HGRN_KEY_DIM))
    logf_b, k_b = hgrn_forget(d_fb.reshape(b, t, HGRN_HEADS, HGRN_KEY_DIM),
                              lb_b[l].reshape(HGRN_HEADS, HGRN_KEY_DIM))
    if ctx is None:
        s0 = jnp.zeros((b, 2, HGRN_HEADS, HGRN_KEY_DIM, HGRN_VAL_DIM), jnp.float32)
    else:
        s0 = ctx[6]
    flip = lambda z: z[:, ::-1]
    o_f, s_f = gla_scan(hq, k_f, hi, logf_f, s0[:, 0])
    o_r, s_r = gla_scan(flip(hq), flip(k_b), flip(hi), flip(logf_b), s0[:, 1])
    o_d = (rms_norm((o_f + flip(o_r)).astype(u.dtype), P['hgrn_g_norm'][l])
           * jax.nn.silu(d_g).reshape(b, t, HGRN_HEADS, HGRN_VAL_DIM))

    mixed = jnp.concatenate([o_a.reshape(b, t, -1), o_b.reshape(b, t, -1),
                             o_c.reshape(b, t, -1), o_d.reshape(b, t, -1)], axis=-1)
    out = mixed @ P['w_out'][l]
    if ctx is None:
        new_ctx = (ckv, kr, dk, dv, gk, gv, jnp.stack([s_f, s_r], axis=1).astype(u.dtype))
    else:
        new_ctx = None
    return out, new_ctx


def swiglu(u, w1, w3, w2):
    return (jax.nn.silu(u @ w1) * (u @ w3)) @ w2


def moe_swiglu(u, router, w1, w3, w2):
    logits = (u @ router).astype(jnp.float32)
    top_v, top_i = lax.top_k(logits, TOP_K)
    wts = jax.nn.softmax(top_v, axis=-1)
    gates = jnp.sum(jax.nn.one_hot(top_i, N_EXPERTS, dtype=jnp.float32) * wts[..., None], axis=-2).astype(u.dtype)
    out = jnp.zeros_like(u)
    for e in range(N_EXPERTS):
        out = out + gates[..., e:e + 1] * swiglu(u, w1[e], w3[e], w2[e])
    return out


def trunk_layer(x, cond, P, l, lb_f, lb_b, pos, ctx):
    mod = (jax.nn.silu(cond) @ P['ada_w'][l] + P['ada_b'][l]).reshape(-1, 6, 1, D_MODEL)
    shift1, scale1, gate1 = mod[:, 0], mod[:, 1], mod[:, 2]
    shift2, scale2, gate2 = mod[:, 3], mod[:, 4], mod[:, 5]
    mix, new_ctx = token_mixer(x * (1.0 + scale1) + shift1, P, l, lb_f, lb_b, pos, ctx)
    x = layer_norm(ALPHA * x + gate1 * mix, P['ln_g'][l, 0], P['ln_b'][l, 0])
    u = x * (1.0 + scale2) + shift2
    j = l // 2
    if l % 2 == 0:
        f = swiglu(u, P['ffn_w1'][j], P['ffn_w3'][j], P['ffn_w2'][j])
    else:
        f = moe_swiglu(u, P['moe_router'][j], P['moe_w1'][j], P['moe_w3'][j], P['moe_w2'][j])
    x = layer_norm(ALPHA * x + gate2 * f, P['ln_g'][l, 1], P['ln_b'][l, 1])
    return x, new_ctx


def setup_inputs(seed: int = 0) -> dict:
    key = jax.random.key(seed)
    ks = iter(jax.random.split(key, 48))

    def nrm(shape, scale):
        return jax.random.normal(next(ks), shape, jnp.float32) * scale

    def gain(shape):
        return 1.0 + nrm(shape, 0.1)

    return {
        'x_prompt': nrm((BATCH, SEQ, D_MODEL), 1.0),
        'x_sample': nrm((DEC_BATCH, DEC_SEQ, D_MODEL), 1.0),
        'cache_mla_ckv': nrm((DEC_BATCH, DEPTH, PAST_LEN, MLA_KV_RANK), 1.0),
        'cache_mla_krope': nrm((DEC_BATCH, DEPTH, PAST_LEN, MLA_ROPE_DIM), 1.0),
        'cache_diff_k': nrm((DEC_BATCH, DEPTH, PAST_LEN, DIFF_HEADS, 2 * DIFF_HEAD_DIM), 1.0),
        'cache_diff_v': nrm((DEC_BATCH, DEPTH, PAST_LEN, DIFF_HEADS, 2 * DIFF_HEAD_DIM), 1.0),
        'cache_gqa_k': nrm((DEC_BATCH, DEPTH, PAST_LEN, GQA_KV_HEADS, GQA_HEAD_DIM), 1.0),
        'cache_gqa_v': nrm((DEC_BATCH, DEPTH, PAST_LEN, GQA_KV_HEADS, GQA_HEAD_DIM), 1.0),
        'state_hgrn': nrm((DEC_BATCH, DEPTH, 2, HGRN_HEADS, HGRN_KEY_DIM, HGRN_VAL_DIM), 0.5),
        'c': nrm((DEC_BATCH, D_MODEL), 1.0),
        'c_ctx': nrm((D_MODEL,), 1.0),
        'w_in': nrm((DEPTH, D_MODEL, IN_WIDTH), D_MODEL ** -0.5),
        'w_out': nrm((DEPTH, MIX_WIDTH, D_MODEL), BETA * MIX_WIDTH ** -0.5),
        'mla_g_cq': gain((DEPTH, MLA_Q_RANK)),
        'mla_g_ckv': gain((DEPTH, MLA_KV_RANK)),
        'mla_w_uq': nrm((DEPTH, MLA_Q_RANK, MLA_HEADS * (MLA_NOPE_DIM + MLA_ROPE_DIM)), MLA_Q_RANK ** -0.5),
        'mla_w_ukv': nrm((DEPTH, MLA_KV_RANK, MLA_HEADS * (MLA_NOPE_DIM + MLA_V_DIM)), MLA_KV_RANK ** -0.5),
        'diff_lambda': nrm((DEPTH, 4, DIFF_HEAD_DIM), 0.1),
        'diff_subln_g': gain((DEPTH, 2 * DIFF_HEAD_DIM)),
        'gqa_g_q': gain((DEPTH, GQA_HEAD_DIM)),
        'gqa_g_k': gain((DEPTH, GQA_HEAD_DIM)),
        'hgrn_gamma': nrm((2, DEPTH, HGRN_HEADS * HGRN_KEY_DIM), 0.5),
        'hgrn_g_norm': gain((DEPTH, HGRN_VAL_DIM)),
        'ada_w': nrm((DEPTH, D_MODEL, 6 * D_MODEL), 0.5 * D_MODEL ** -0.5),
        'ada_b': nrm((DEPTH, 6 * D_MODEL), 0.01),
        'ln_g': gain((DEPTH, 2, D_MODEL)),
        'ln_b': nrm((DEPTH, 2, D_MODEL), 0.01),
        'ffn_w1': nrm((N_DENSE, D_MODEL, D_FF), D_MODEL ** -0.5),
        'ffn_w3': nrm((N_DENSE, D_MODEL, D_FF), D_MODEL ** -0.5),
        'ffn_w2': nrm((N_DENSE, D_FF, D_MODEL), BETA * D_FF ** -0.5),
        'moe_router': nrm((N_MOE, D_MODEL, N_EXPERTS), D_MODEL ** -0.5),
        'moe_w1': nrm((N_MOE, N_EXPERTS, D_MODEL, EXPERT_FF), D_MODEL ** -0.5),
        'moe_w3': nrm((N_MOE, N_EXPERTS, D_MODEL, EXPERT_FF), D_MODEL ** -0.5),
        'moe_w2': nrm((N_MOE, N_EXPERTS, EXPERT_FF, D_MODEL), BETA * EXPERT_FF ** -0.5),
    }


def reference(x_prompt, x_sample, cache_mla_ckv, cache_mla_krope, cache_diff_k, cache_diff_v,
              cache_gqa_k, cache_gqa_v, state_hgrn, c, c_ctx, w_in, w_out, mla_g_cq, mla_g_ckv,
              mla_w_uq, mla_w_ukv, diff_lambda, diff_subln_g, gqa_g_q, gqa_g_k, hgrn_gamma,
              hgrn_g_norm, ada_w, ada_b, ln_g, ln_b, ffn_w1, ffn_w3, ffn_w2, moe_router,
              moe_w1, moe_w3, moe_w2):
    P = dict(w_in=w_in, w_out=w_out, mla_g_cq=mla_g_cq, mla_g_ckv=mla_g_ckv, mla_w_uq=mla_w_uq,
             mla_w_ukv=mla_w_ukv, diff_lambda=diff_lambda, diff_subln_g=diff_subln_g,
             gqa_g_q=gqa_g_q, gqa_g_k=gqa_g_k, hgrn_g_norm=hgrn_g_norm, ada_w=ada_w, ada_b=ada_b,
             ln_g=ln_g, ln_b=ln_b, ffn_w1=ffn_w1, ffn_w3=ffn_w3, ffn_w2=ffn_w2,
             moe_router=moe_router, moe_w1=moe_w1, moe_w3=moe_w3, moe_w2=moe_w2)
    lb_f = hgrn_lower_bounds(hgrn_gamma[0])
    lb_b = hgrn_lower_bounds(hgrn_gamma[1])

    x = x_prompt
    ctx_out = []
    for l in range(DEPTH):
        x, new_ctx = trunk_layer(x, c_ctx[None, :], P, l, lb_f, lb_b, None, None)
        ctx_out.append(new_ctx)
    y_prompt = x
    new_mla_ckv = jnp.stack([e[0] for e in ctx_out], axis=1)
    new_mla_krope = jnp.stack([e[1] for e in ctx_out], axis=1)
    new_diff_k = jnp.stack([e[2] for e in ctx_out], axis=1)
    new_diff_v = jnp.stack([e[3] for e in ctx_out], axis=1)
    new_gqa_k = jnp.stack([e[4] for e in ctx_out], axis=1)
    new_gqa_v = jnp.stack([e[5] for e in ctx_out], axis=1)
    new_state_hgrn = jnp.stack([e[6] for e in ctx_out], axis=1)

    n_tok = x_sample.shape[1]
    n_rows = n_tok // GRID_W
    rows = jnp.repeat(jnp.arange(n_rows, dtype=jnp.float32), GRID_W)
    cols = jnp.tile(jnp.arange(GRID_W, dtype=jnp.float32), n_rows)
    x = x_sample
    for l in range(DEPTH):
        ctx = (cache_mla_ckv[:, l], cache_mla_krope[:, l], cache_diff_k[:, l], cache_diff_v[:, l],
               cache_gqa_k[:, l], cache_gqa_v[:, l], state_hgrn[:, l])
        x, _ = trunk_layer(x, c, P, l, lb_f, lb_b, (rows, cols), ctx)
    y_sample = x

    return (y_prompt, y_sample, new_mla_ckv, new_mla_krope, new_diff_k, new_diff_v,
            new_gqa_k, new_gqa_v, new_state_hgrn)
```

```python
import functools
import math

import jax
import jax.numpy as jnp
from jax import lax
from jax.experimental import pallas as pl
from jax.experimental.pallas import tpu as pltpu

F32 = jnp.float32
BF16 = jnp.bfloat16

GRID_W = 64
ROPE_THETA = 10000.0
LN_EPS = 1e-5
RMS_EPS = 1e-6
MLA_NOPE_DIM = 128
MLA_V_DIM = 128
TOP_K = 2

LANE = 128
V7X_VMEM_BYTES = 64 * 1024 * 1024
VMEM_CAP = V7X_VMEM_BYTES - 6 * 1024 * 1024
HGRN_CHUNK = 64
HGRN_SUB = 16


def _cparams(sem, est_bytes):
    limit = int(min(VMEM_CAP, max(32 * 1024 * 1024, est_bytes * 5 // 4 + (4 << 20))))
    return pltpu.CompilerParams(dimension_semantics=sem, vmem_limit_bytes=limit)


def _pick_tile(n, target):
    if n <= target:
        return n
    t = (target // LANE) * LANE
    while t >= LANE:
        if n % t == 0:
            return t
        t -= LANE
    return n


def _pick_rows(n, target):
    if n <= target:
        return n
    t = target
    while t >= 8:
        if n % t == 0:
            return t
        t //= 2
    return n


def _dot_nt(a, b):
    return lax.dot_general(a, b, (((1,), (1,)), ((), ())), preferred_element_type=F32)


def _dot_tn(a, b):
    return lax.dot_general(a, b, (((0,), (0,)), ((), ())), preferred_element_type=F32)


def _mm_kernel(x_ref, w_ref, o_ref):
    o_ref[...] = jnp.dot(x_ref[...], w_ref[...], preferred_element_type=F32).astype(o_ref.dtype)


def matmul(x, w, out_dtype, layer=None, tm=1024, tn=1024):
    m, k = x.shape
    n = w.shape[-1]
    tm = _pick_rows(m, tm)
    tn = _pick_tile(n, tn)
    if w.ndim == 3:
        w_spec = pl.BlockSpec((None, k, tn), lambda j, i: (layer, 0, j))
    else:
        w_spec = pl.BlockSpec((k, tn), lambda j, i: (0, j))
    est = 2 * (tm * k * 2 + k * tn * 2 + tm * tn * jnp.dtype(out_dtype).itemsize) + tm * tn * 4
    return pl.pallas_call(
        _mm_kernel,
        grid=(n // tn, m // tm),
        in_specs=[pl.BlockSpec((tm, k), lambda j, i: (i, 0)), w_spec],
        out_specs=pl.BlockSpec((tm, tn), lambda j, i: (i, j)),
        out_shape=jax.ShapeDtypeStruct((m, n), out_dtype),
        compiler_params=_cparams(("arbitrary", "arbitrary"), est),
    )(x, w)


def _swiglu_kernel(x_ref, w1_ref, w3_ref, o_ref):
    x = x_ref[...]
    a = jnp.dot(x, w1_ref[...], preferred_element_type=F32)
    b = jnp.dot(x, w3_ref[...], preferred_element_type=F32)
    o_ref[...] = (a * jax.nn.sigmoid(a) * b).astype(o_ref.dtype)


def swiglu_up(x, w1, w3, sel, tm=1024, tn=512):
    m, k = x.shape
    n = w1.shape[-1]
    ne = len(sel)
    first = sel[0]
    assert tuple(sel) == tuple(range(first, first + ne))
    tm = _pick_rows(m, tm)
    tn = _pick_tile(n, tn)
    npe = n // tn
    w_spec = pl.BlockSpec((None, k, tn), lambda e, j, i: (first + e, 0, j))
    est = 2 * (tm * k * 2 + 2 * k * tn * 2 + tm * tn * 2) + 3 * tm * tn * 4
    return pl.pallas_call(
        _swiglu_kernel,
        grid=(ne, npe, m // tm),
        in_specs=[pl.BlockSpec((tm, k), lambda e, j, i: (i, 0)), w_spec, w_spec],
        out_specs=pl.BlockSpec((tm, tn), lambda e, j, i: (i, e * npe + j)),
        out_shape=jax.ShapeDtypeStruct((m, ne * n), BF16),
        compiler_params=_cparams(("arbitrary", "arbitrary", "arbitrary"), est),
    )(x, w1, w3)


def _down_kernel(x_ref, w_ref, o_ref):
    p = jnp.dot(x_ref[...], w_ref[...], preferred_element_type=F32)

    @pl.when(pl.program_id(2) == 0)
    def _():
        o_ref[...] = p

    @pl.when(pl.program_id(2) > 0)
    def _():
        o_ref[...] += p


def _down_gated_kernel(x_ref, w_ref, g_ref, o_ref):
    p = jnp.dot(x_ref[...], w_ref[...], preferred_element_type=F32) * g_ref[...]

    @pl.when(pl.program_id(2) == 0)
    def _():
        o_ref[...] = p

    @pl.when(pl.program_id(2) > 0)
    def _():
        o_ref[...] += p


def down_proj(x, w, gates=None, k_per_gate=None, tm=2048, tn=1024, tk=1792):
    m, k = x.shape
    n = w.shape[-1]
    tm = _pick_rows(m, tm)
    tn = _pick_tile(n, tn)
    kk = k if k_per_gate is None else k_per_gate
    tk = _pick_tile(kk, tk)
    nk = k // tk
    est = 2 * (tm * tk * 2 + tk * tn * 2 + tm * tn * 4) + tm * tn * 4
    in_specs = [pl.BlockSpec((tm, tk), lambda i, j, s: (i, s)),
                pl.BlockSpec((tk, tn), lambda i, j, s: (s, j))]
    args = [x, w]
    body = _down_kernel
    if gates is not None:
        per = k_per_gate // tk
        in_specs.append(pl.BlockSpec((None, tm, 1), lambda i, j, s: (s // per, i, 0)))
        args.append(gates)
        body = _down_gated_kernel
        est += 2 * tm * LANE * 4
    return pl.pallas_call(
        body,
        grid=(m // tm, n // tn, nk),
        in_specs=in_specs,
        out_specs=pl.BlockSpec((tm, tn), lambda i, j, s: (i, j)),
        out_shape=jax.ShapeDtypeStruct((m, n), F32),
        compiler_params=_cparams(("arbitrary", "arbitrary", "arbitrary"), est),
    )(*args)


def _modulate_kernel(x_ref, scale_ref, shift_ref, u_ref):
    u_ref[...] = (x_ref[...] * (1.0 + scale_ref[0, 0]) + shift_ref[0, 0]).astype(u_ref.dtype)


def _ln_kernel(x_ref, y_ref, gate_ref, g_ref, b_ref, scale_ref, shift_ref, xo_ref, uo_ref, *, alpha):
    z = alpha * x_ref[...] + gate_ref[0, 0] * y_ref[...]
    zc = z - jnp.mean(z, axis=-1, keepdims=True)
    yn = zc * lax.rsqrt(jnp.mean(zc * zc, axis=-1, keepdims=True) + LN_EPS)
    xn = yn * g_ref[...] + b_ref[...]
    xo_ref[...] = xn
    uo_ref[...] = (xn * (1.0 + scale_ref[0, 0]) + shift_ref[0, 0]).astype(uo_ref.dtype)


def _ln_last_kernel(x_ref, y_ref, gate_ref, g_ref, b_ref, xo_ref, *, alpha):
    z = alpha * x_ref[...] + gate_ref[0, 0] * y_ref[...]
    zc = z - jnp.mean(z, axis=-1, keepdims=True)
    yn = zc * lax.rsqrt(jnp.mean(zc * zc, axis=-1, keepdims=True) + LN_EPS)
    xo_ref[...] = yn * g_ref[...] + b_ref[...]


class _Tokens:
    def __init__(self, n_ctx, t_lat, n_lat_batches, d):
        self.n_ctx, self.t_lat, self.d = n_ctx, t_lat, d
        self.n = n_ctx + t_lat * n_lat_batches
        tm = 256
        while n_ctx % tm or t_lat % tm:
            tm //= 2
        self.tm = tm
        self.ctx_tiles = n_ctx // tm
        self.tiles_per_lat = t_lat // tm

    def mod_row(self, i):
        return jnp.where(i < self.ctx_tiles, 0, 1 + (i - self.ctx_tiles) // self.tiles_per_lat)

    def mod_spec(self, idx):
        return pl.BlockSpec((1, 1, 1, self.d), lambda i: (idx, self.mod_row(i), 0, 0))

    def row_spec(self):
        return pl.BlockSpec((self.tm, self.d), lambda i: (i, 0))

    def vec_spec(self, idx):
        return pl.BlockSpec((None, 1, self.d), lambda i: (idx, 0, 0))


def modulate(tok, x, mod, idx_scale, idx_shift):
    est = 2 * tok.tm * tok.d * 6 + 4 * tok.tm * tok.d * 4
    return pl.pallas_call(
        _modulate_kernel,
        grid=(tok.n // tok.tm,),
        in_specs=[tok.row_spec(), tok.mod_spec(idx_scale), tok.mod_spec(idx_shift)],
        out_specs=tok.row_spec(),
        out_shape=jax.ShapeDtypeStruct((tok.n, tok.d), BF16),
        compiler_params=_cparams(("arbitrary",), est),
    )(x, mod, mod)


def residual_ln(tok, x, y, mod, idx_gate, ln_gb, idx_g, idx_b, alpha, idx_scale=None, idx_shift=None):
    est = 2 * tok.tm * tok.d * (4 + 4 + 4 + 2) + 6 * tok.tm * tok.d * 4
    in_specs = [tok.row_spec(), tok.row_spec(), tok.mod_spec(idx_gate), tok.vec_spec(idx_g), tok.vec_spec(idx_b)]
    args = [x, y, mod, ln_gb, ln_gb]
    if idx_scale is None:
        return pl.pallas_call(
            functools.partial(_ln_last_kernel, alpha=alpha),
            grid=(tok.n // tok.tm,),
            in_specs=in_specs,
            out_specs=tok.row_spec(),
            out_shape=jax.ShapeDtypeStruct((tok.n, tok.d), F32),
            compiler_params=_cparams(("arbitrary",), est),
        )(*args), None
    in_specs += [tok.mod_spec(idx_scale), tok.mod_spec(idx_shift)]
    args += [mod, mod]
    return pl.pallas_call(
        functools.partial(_ln_kernel, alpha=alpha),
        grid=(tok.n // tok.tm,),
        in_specs=in_specs,
        out_specs=[tok.row_spec(), tok.row_spec()],
        out_shape=[jax.ShapeDtypeStruct((tok.n, tok.d), F32), jax.ShapeDtypeStruct((tok.n, tok.d), BF16)],
        compiler_params=_cparams(("arbitrary",), est),
    )(*args)


def _softmax_pv(s, v):
    m = jnp.max(s, axis=-1, keepdims=True)
    p = jnp.exp(s - m)
    l = jnp.sum(p, axis=-1, keepdims=True)
    return jnp.dot(p.astype(BF16), v, preferred_element_type=F32) / l


def _mla_attn_kernel(q_ref, kv_ref, kr_ref, o_ref, *, scale):
    q = q_ref[0]
    kv = kv_ref[0]
    s = _dot_nt(q[:, :MLA_NOPE_DIM], kv[:, :MLA_NOPE_DIM]) + _dot_nt(q[:, MLA_NOPE_DIM:], kr_ref[0])
    o_ref[0] = _softmax_pv(s * scale, kv[:, MLA_NOPE_DIM:]).astype(o_ref.dtype)


def _gqa_attn_kernel(q_ref, k_ref, v_ref, o_ref, *, scale):
    s = _dot_nt(q_ref[0], k_ref[0])
    o_ref[0] = _softmax_pv(s * scale, v_ref[0]).astype(o_ref.dtype)


def _diff_attn_kernel(q_ref, k_ref, v_ref, lam_ref, g_ref, o_ref, *, scale, dd):
    q = q_ref[0]
    k = k_ref[0]
    v = v_ref[0]
    lane = lax.broadcasted_iota(jnp.int32, q.shape, 1)
    zero = jnp.zeros_like(q)
    q1 = jnp.where(lane < dd, q, zero)
    q2 = jnp.where(lane < dd, zero, q)
    a1 = _softmax_pv(_dot_nt(q1, k) * scale, v)
    a2 = _softmax_pv(_dot_nt(q2, k) * scale, v)
    d = a1 - lam_ref[...] * a2
    y = d * lax.rsqrt(jnp.mean(d * d, axis=-1, keepdims=True) + RMS_EPS)
    o_ref[0] = (y * g_ref[...]).astype(o_ref.dtype)


def _attn_call(body, q, kvs, extra, heads, hw_q, hw_kv, kv_group, hw_o, tq):
    b, t, _ = q.shape
    tq = _pick_rows(t, tq)
    in_specs = [pl.BlockSpec((1, tq, hw_q), lambda bi, h, qi: (bi, qi, h))]
    est = 2 * tq * hw_q * 2
    length = kvs[0][0].shape[1]
    for arr, hw, shared in kvs:
        if shared:
            in_specs.append(pl.BlockSpec((1, length, hw), lambda bi, h, qi: (bi, 0, 0)))
        else:
            in_specs.append(pl.BlockSpec((1, length, hw), lambda bi, h, qi: (bi, 0, h // kv_group)))
        est += 2 * length * hw * 2
    for arr in extra:
        in_specs.append(pl.BlockSpec(arr.shape, lambda bi, h, qi: (0, 0)))
    est += 2 * tq * hw_o * 2 + 6 * tq * length * 4
    return pl.pallas_call(
        body,
        grid=(b, heads, t // tq),
        in_specs=in_specs,
        out_specs=pl.BlockSpec((1, tq, hw_o), lambda bi, h, qi: (bi, qi, h)),
        out_shape=jax.ShapeDtypeStruct((b, t, heads * hw_o), BF16),
        compiler_params=_cparams(("arbitrary", "arbitrary", "arbitrary"), est),
    )(q, *[a for a, _, _ in kvs], *extra)


def _hgrn_kernel(q_ref, k_ref, v_ref, g_ref, s0_ref, o_ref, sT_ref, st_scr, *, chunk, sub, n_chunks):
    st_scr[...] = s0_ref[0, 0, 0].T
    row = lax.broadcasted_iota(jnp.int32, (chunk, chunk), 0)
    col = lax.broadcasted_iota(jnp.int32, (chunk, chunk), 1)
    tri = jnp.where(row >= col, 1.0, 0.0).astype(F32)
    key_row = lax.broadcasted_iota(jnp.int32, (chunk, 1), 0)
    sub_row = lax.broadcasted_iota(jnp.int32, (sub, 1), 0)
    neg_inf = jnp.float32(-jnp.inf)

    def body(ci, carry):
        r0 = pl.multiple_of(ci * chunk, chunk)
        q = q_ref[0, 0, pl.ds(r0, chunk), :]
        k = k_ref[0, 0, pl.ds(r0, chunk), :]
        v = v_ref[0, 0, pl.ds(r0, chunk), :]
        g = g_ref[0, 0, pl.ds(r0, chunk), :]
        cum = jnp.dot(tri, g, precision=lax.Precision.HIGHEST, preferred_element_type=F32)
        last = cum[chunk - 1:chunk, :]
        st = st_scr[...]
        vb = v.astype(BF16)
        inter = _dot_nt((q * jnp.exp(cum)).astype(BF16), st.astype(BF16))
        outs = []
        for i in range(chunk // sub):
            lo = i * sub
            qi = q[lo:lo + sub]
            cumi = cum[lo:lo + sub]
            acc = inter[lo:lo + sub]
            if i > 0:
                base = cum[lo - 1:lo, :]
                qn = (qi * jnp.exp(cumi - base)).astype(BF16)
                kn = (k * jnp.exp(jnp.where(key_row < lo, base - cum, neg_inf))).astype(BF16)
                att = _dot_nt(qn, kn)
                acc = acc + jnp.dot(att.astype(BF16), vb, preferred_element_type=F32)
            for s in range(sub):
                r = lo + s
                w = jnp.exp(jnp.where(sub_row >= s, cumi - cum[r:r + 1, :], neg_inf))
                colv = jnp.sum(qi * k[r:r + 1, :] * w, axis=-1, keepdims=True)
                acc = acc + colv * v[r:r + 1, :]
            outs.append(acc)
        o_ref[0, 0, pl.ds(r0, chunk), :] = jnp.concatenate(outs, axis=0)
        kl = (k * jnp.exp(last - cum)).astype(BF16)
        st_scr[...] = st * jnp.exp(last) + _dot_tn(vb, kl)
        return carry

    lax.fori_loop(0, n_chunks, body, 0)
    sT_ref[0, 0, 0] = st_scr[...].T


def hgrn_scan(q2, k2, v2, g2, s0, heads, dk, dv):
    _, b, t, _ = q2.shape
    chunk = min(HGRN_CHUNK, t)
    sub = min(HGRN_SUB, chunk)
    seq_k = pl.BlockSpec((1, 1, t, dk), lambda d, bi, h: (d, bi, 0, h))
    seq_v = pl.BlockSpec((1, 1, t, dv), lambda d, bi, h: (d, bi, 0, h))
    st = pl.BlockSpec((1, 1, 1, dk, dv), lambda d, bi, h: (bi, d, h, 0, 0))
    est = 2 * t * (3 * dk + 2 * dv) * 4 + 6 * dk * dv * 4 + (2 << 20)
    return pl.pallas_call(
        functools.partial(_hgrn_kernel, chunk=chunk, sub=sub, n_chunks=t // chunk),
        grid=(2, b, heads),
        in_specs=[seq_k, seq_k, seq_v, seq_k, st],
        out_specs=[seq_v, st],
        out_shape=[jax.ShapeDtypeStruct((2, b, t, heads * dv), F32),
                   jax.ShapeDtypeStruct((b, 2, heads, dk, dv), F32)],
        scratch_shapes=[pltpu.VMEM((dv, dk), F32)],
        compiler_params=_cparams(("arbitrary", "arbitrary", "arbitrary"), est),
    )(q2, k2, v2, g2, s0)


def _rms(x, g):
    return x * lax.rsqrt(jnp.mean(x * x, axis=-1, keepdims=True) + RMS_EPS) * g


def _rope_tables(t, d):
    n = d // 4
    inv = ROPE_THETA ** (-jnp.arange(n, dtype=F32) / n)
    pos = jnp.arange(t)
    rows = (pos // GRID_W).astype(F32)
    cols = (pos % GRID_W).astype(F32)
    ar = rows[:, None] * inv[None, :]
    ac = cols[:, None] * inv[None, :]
    cos = jnp.concatenate([jnp.cos(ar), jnp.cos(ar), jnp.cos(ac), jnp.cos(ac)], axis=-1)
    sin = jnp.concatenate([jnp.sin(ar), jnp.sin(ar), jnp.sin(ac), jnp.sin(ac)], axis=-1)
    return cos, sin


def _rope(x, tables):
    if tables is None:
        return x
    cos, sin = tables
    d = x.shape[-1]
    n = d // 4
    xq = x.reshape(x.shape[:-1] + (2, 2, n))
    rot = jnp.stack([-xq[..., 1, :], xq[..., 0, :]], axis=-2).reshape(x.shape)
    return x * cos[None, :, None, :] + rot * sin[None, :, None, :]


def _hgrn_forget(z, lb):
    logf = jnp.logaddexp(jnp.log(lb), jnp.log1p(-lb) + jax.nn.log_sigmoid(z))
    return logf, (1.0 - lb) * jax.nn.sigmoid(-z)


def kernel(x_prompt, x_sample, cache_mla_ckv, cache_mla_krope, cache_diff_k, cache_diff_v, cache_gqa_k, cache_gqa_v, state_hgrn, c, c_ctx, w_in, w_out, mla_g_cq, mla_g_ckv, mla_w_uq, mla_w_ukv, diff_lambda, diff_subln_g, gqa_g_q, gqa_g_k, hgrn_gamma, hgrn_g_norm, ada_w, ada_b, ln_g, ln_b, ffn_w1, ffn_w3, ffn_w2, moe_router, moe_w1, moe_w3, moe_w2):
    bc, tc, d = x_prompt.shape
    bl, tl, _ = x_sample.shape
    depth = w_in.shape[0]
    past = cache_mla_ckv.shape[2]
    q_rank = mla_g_cq.shape[1]
    kv_rank = mla_g_ckv.shape[1]
    rope_d = cache_mla_krope.shape[-1]
    heads = cache_diff_k.shape[3]
    dd = cache_diff_k.shape[4] // 2
    kv_heads = cache_gqa_k.shape[3]
    gd = cache_gqa_k.shape[4]
    hk = state_hgrn.shape[4]
    hv = state_hgrn.shape[5]
    n_exp = moe_w1.shape[1]
    alpha = (2 * depth) ** 0.25
    n_ctx = bc * tc
    n_lat = bl * tl
    tok = _Tokens(n_ctx, tl, bl, d)
    mla_hw = 2 * LANE
    assert MLA_NOPE_DIM + rope_d <= mla_hw and MLA_NOPE_DIM + MLA_V_DIM == mla_hw

    w_a = q_rank + kv_rank
    w_rest = w_in.shape[2] - w_a - rope_d
    n_in = w_a + w_rest + LANE
    n_in_pad = -(-n_in // 1024) * 1024 if n_in > 1024 else n_in
    w_in_b = jnp.concatenate(
        [w_in[:, :, :w_a], w_in[:, :, w_a + rope_d:], w_in[:, :, w_a:w_a + rope_d],
         jnp.zeros((depth, d, n_in_pad - n_in + LANE - rope_d), w_in.dtype)], axis=-1).astype(BF16)
    off_kr = w_a + w_rest
    w_uq_b = jnp.pad(mla_w_uq.reshape(depth, q_rank, heads, MLA_NOPE_DIM + rope_d),
                     ((0, 0), (0, 0), (0, 0), (0, mla_hw - MLA_NOPE_DIM - rope_d))
                     ).reshape(depth, q_rank, heads * mla_hw).astype(BF16)
    w_ukv_b = mla_w_ukv.astype(BF16)
    w_out_b = w_out.astype(BF16)
    ffn_w1_b, ffn_w3_b, ffn_w2_b = ffn_w1.astype(BF16), ffn_w3.astype(BF16), ffn_w2.astype(BF16)
    e_ff = moe_w1.shape[-1]
    n_moe = moe_w1.shape[0]
    moe_w1_b = moe_w1.astype(BF16).reshape(n_moe * n_exp, d, e_ff)
    moe_w3_b = moe_w3.astype(BF16).reshape(n_moe * n_exp, d, e_ff)
    moe_w2_b = moe_w2.astype(BF16).reshape(n_moe, n_exp * e_ff, d)
    router_b = jnp.pad(moe_router, ((0, 0), (0, 0), (0, LANE - n_exp))).astype(BF16)

    cond = jnp.concatenate([c_ctx[None, :], c], axis=0)
    n_cond = cond.shape[0]
    cond_p = jnp.pad(jax.nn.silu(cond), ((0, -n_cond % 8), (0, 0))).astype(BF16)
    mods = []
    for l in range(depth):
        ml = matmul(cond_p, ada_w.astype(BF16), F32, layer=l, tn=2048)[:n_cond] + ada_b[l][None, :]
        mods.append(ml.reshape(n_cond, 6, d).transpose(1, 0, 2))
    mod = jnp.concatenate(mods, axis=0).reshape(depth * 6, n_cond, 1, d)
    ln_gb = jnp.concatenate([ln_g.reshape(depth * 2, 1, d), ln_b.reshape(depth * 2, 1, d)], axis=0)

    def lower_bounds(gamma):
        cs = jnp.cumsum(jax.nn.softmax(gamma.astype(F32), axis=0), axis=0)
        return cs - cs[0:1]

    lb_f = lower_bounds(hgrn_gamma[0])
    lb_b = lower_bounds(hgrn_gamma[1])
    tables = {w: _rope_tables(tl, w) for w in {rope_d, dd, gd}}

    def mixer(h, l, b, t, ctx):
        is_ctx = ctx is None
        tb = (lambda w: None) if is_ctx else (lambda w: tables[w])
        cat = (lambda c0, z: z) if is_ctx else (lambda c0, z: jnp.concatenate([c0.astype(z.dtype), z], axis=1))
        o = 0

        def take(w):
            nonlocal o
            z = h[:, o:o + w]
            o += w
            return z

        a_cq, a_ckv = take(q_rank), take(kv_rank)
        b_q, b_k, b_v = take(heads * 2 * dd), take(heads * 2 * dd), take(heads * 2 * dd)
        c_q, c_k, c_v = take(heads * gd), take(kv_heads * gd), take(kv_heads * gd)
        d_q, d_i, d_f, d_fb, d_g = take(heads * hk), take(heads * hv), take(heads * hk), take(heads * hk), take(heads * hv)
        a_kr = h[:, off_kr:off_kr + rope_d]

        cq = _rms(a_cq, mla_g_cq[l]).astype(BF16)
        q = matmul(cq, w_uq_b, F32, layer=l).reshape(b, t, heads, mla_hw)
        q = jnp.concatenate([q[..., :MLA_NOPE_DIM], _rope(q[..., MLA_NOPE_DIM:MLA_NOPE_DIM + rope_d], tb(rope_d)),
                             q[..., MLA_NOPE_DIM + rope_d:]], axis=-1).reshape(b, t, heads * mla_hw).astype(BF16)
        ckv = _rms(a_ckv, mla_g_ckv[l]).reshape(b, t, kv_rank)
        kr = _rope(a_kr.reshape(b, t, 1, rope_d), tb(rope_d)).reshape(b, t, rope_d)
        ckv_all = cat(None if is_ctx else ctx[0], ckv)
        kr_all = cat(None if is_ctx else ctx[1], kr)
        n_keys = ckv_all.shape[1]
        kv = matmul(ckv_all.reshape(b * n_keys, kv_rank).astype(BF16), w_ukv_b, BF16, layer=l
                    ).reshape(b, n_keys, heads * mla_hw)
        kr_p = jnp.pad(kr_all, ((0, 0), (0, 0), (0, LANE - rope_d))).astype(BF16)
        o_a = _attn_call(
            functools.partial(_mla_attn_kernel, scale=(MLA_NOPE_DIM + rope_d) ** -0.5),
            q, [(kv, mla_hw, False), (kr_p, LANE, True)], [], heads, mla_hw, mla_hw, 1, MLA_V_DIM, 512)

        dq = _rope(b_q.reshape(b, t, 2 * heads, dd), tb(dd)).reshape(b, t, heads * 2 * dd)
        dk = _rope(b_k.reshape(b, t, 2 * heads, dd), tb(dd)).reshape(b, t, heads, 2 * dd)
        dv = b_v.reshape(b, t, heads, 2 * dd)
        dk_all = cat(None if is_ctx else ctx[2], dk).reshape(b, n_keys, heads * 2 * dd)
        dv_all = cat(None if is_ctx else ctx[3], dv).reshape(b, n_keys, heads * 2 * dd)
        lam_p = diff_lambda[l].astype(F32)
        lam_init = 0.8 - 0.6 * math.exp(-0.3 * l)
        lam = jnp.exp(jnp.sum(lam_p[0] * lam_p[1])) - jnp.exp(jnp.sum(lam_p[2] * lam_p[3])) + lam_init
        lam_v = jnp.full((1, 2 * dd), lam, F32)
        g_v = (diff_subln_g[l] * (1.0 - lam_init)).reshape(1, 2 * dd)
        o_b = _attn_call(
            functools.partial(_diff_attn_kernel, scale=dd ** -0.5, dd=dd),
            dq.astype(BF16), [(dk_all.astype(BF16), 2 * dd, False), (dv_all.astype(BF16), 2 * dd, False)],
            [lam_v, g_v], heads, 2 * dd, 2 * dd, 1, 2 * dd, 512)

        gq = _rope(_rms(c_q.reshape(b, t, heads, gd), gqa_g_q[l]), tb(gd)).reshape(b, t, heads * gd)
        gk = _rope(_rms(c_k.reshape(b, t, kv_heads, gd), gqa_g_k[l]), tb(gd))
        gv = c_v.reshape(b, t, kv_heads, gd)
        gk_all = cat(None if is_ctx else ctx[4], gk).reshape(b, n_keys, kv_heads * gd)
        gv_all = cat(None if is_ctx else ctx[5], gv).reshape(b, n_keys, kv_heads * gd)
        o_c = _attn_call(
            functools.partial(_gqa_attn_kernel, scale=gd ** -0.5),
            gq.astype(BF16), [(gk_all.astype(BF16), gd, False), (gv_all.astype(BF16), gd, False)],
            [], heads, gd, gd, heads // kv_heads, gd, 512)

        hq = (jax.nn.silu(d_q) * (hk ** -0.5)).reshape(b, t, heads * hk)
        hi = d_i.reshape(b, t, heads * hv)
        logf_f, k_f = _hgrn_forget(d_f.reshape(b, t, heads * hk), lb_f[l])
        logf_b, k_b = _hgrn_forget(d_fb.reshape(b, t, heads * hk), lb_b[l])
        flip = lambda z: z[:, ::-1]
        s0 = jnp.zeros((b, 2, heads, hk, hv), F32) if is_ctx else ctx[6]
        o2, s_new = hgrn_scan(jnp.stack([hq, flip(hq)]), jnp.stack([k_f, flip(k_b)]),
                              jnp.stack([hi, flip(hi)]), jnp.stack([logf_f, flip(logf_b)]), s0, heads, hk, hv)
        o_sum = (o2[0] + flip(o2[1])).reshape(b, t, heads, hv)
        o_d = (_rms(o_sum, hgrn_g_norm[l]) * jax.nn.silu(d_g).reshape(b, t, heads, hv)).reshape(b, t, heads * hv)

        mixed = jnp.concatenate([o_a, o_b, o_c, o_d.astype(BF16)], axis=-1).reshape(b * t, -1)
        new_ctx = (ckv, kr, dk, dv, gk, gv, s_new) if is_ctx else None
        return mixed, new_ctx

    x = jnp.concatenate([x_prompt.reshape(n_ctx, d), x_sample.reshape(n_lat, d)], axis=0)
    u = modulate(tok, x, mod, 1, 0)
    ctx_out = []
    for l in range(depth):
        h = matmul(u, w_in_b, F32, layer=l)
        ctx_l = (cache_mla_ckv[:, l], cache_mla_krope[:, l], cache_diff_k[:, l], cache_diff_v[:, l],
                 cache_gqa_k[:, l], cache_gqa_v[:, l], state_hgrn[:, l])
        mixed_c, new_ctx = mixer(h[:n_ctx], l, bc, tc, None)
        mixed_l, _ = mixer(h[n_ctx:], l, bl, tl, ctx_l)
        ctx_out.append(new_ctx)
        mix = matmul(jnp.concatenate([mixed_c, mixed_l], axis=0), w_out_b, F32, layer=l)
        x, u = residual_ln(tok, x, mix, mod, l * 6 + 2, ln_gb, 2 * l, 2 * depth + 2 * l, alpha,
                           l * 6 + 4, l * 6 + 3)
        j = l // 2
        if l % 2 == 0:
            f = down_proj(swiglu_up(u, ffn_w1_b, ffn_w3_b, (j,)), ffn_w2_b[j])
        else:
            logits = matmul(u, router_b, F32, layer=j)[:, :n_exp]
            top_v, top_i = lax.top_k(logits, TOP_K)
            wts = jax.nn.softmax(top_v, axis=-1)
            gates = jnp.sum(jax.nn.one_hot(top_i, n_exp, dtype=F32) * wts[..., None], axis=-2)
            act = swiglu_up(u, moe_w1_b, moe_w3_b, tuple(range(j * n_exp, (j + 1) * n_exp)))
            f = down_proj(act, moe_w2_b[j], gates=gates.T[:, :, None], k_per_gate=e_ff)
        if l + 1 < depth:
            x, u = residual_ln(tok, x, f, mod, l * 6 + 5, ln_gb, 2 * l + 1, 2 * depth + 2 * l + 1, alpha,
                               (l + 1) * 6 + 1, (l + 1) * 6 + 0)
        else:
            x, _ = residual_ln(tok, x, f, mod, l * 6 + 5, ln_gb, 2 * l + 1, 2 * depth + 2 * l + 1, alpha)

    y_prompt = x[:n_ctx].reshape(bc, tc, d)
    y_sample = x[n_ctx:].reshape(bl, tl, d)
    stack = lambda i: jnp.stack([e[i] for e in ctx_out], axis=1)
    return (y_prompt, y_sample, stack(0), stack(1), stack(2), stack(3), stack(4), stack(5), stack(6))
```

```python
import functools
import math

import jax
import jax.numpy as jnp
from jax import lax
from jax.experimental import pallas as pl
from jax.experimental.pallas import tpu as pltpu

F32 = jnp.float32
BF16 = jnp.bfloat16

GRID_W = 64
ROPE_THETA = 10000.0
LN_EPS = 1e-5
RMS_EPS = 1e-6
MLA_NOPE_DIM = 128
MLA_V_DIM = 128
TOP_K = 2

LANE = 128
V7X_VMEM_BYTES = 64 * 1024 * 1024
VMEM_CAP = V7X_VMEM_BYTES - 6 * 1024 * 1024
HGRN_CHUNK = 64
HGRN_SUB = 16
HGRN_BATCHES = 2
MOE_ROWS = 512


def _cparams(sem, est_bytes):
    limit = int(min(VMEM_CAP, max(32 * 1024 * 1024, est_bytes * 5 // 4 + (4 << 20))))
    return pltpu.CompilerParams(dimension_semantics=sem, vmem_limit_bytes=limit)


def _pick_tile(n, target):
    if n <= target:
        return n
    t = (target // LANE) * LANE
    while t >= LANE:
        if n % t == 0:
            return t
        t -= LANE
    return n


def _pick_rows(n, target):
    if n <= target:
        return n
    t = target
    while t >= 8:
        if n % t == 0:
            return t
        t //= 2
    return n


def _dot_nt(a, b):
    return lax.dot_general(a, b, (((1,), (1,)), ((), ())), preferred_element_type=F32)


def _dot_tn(a, b):
    return lax.dot_general(a, b, (((0,), (0,)), ((), ())), preferred_element_type=F32)


def _mm_kernel(x_ref, w_ref, o_ref):
    o_ref[...] = jnp.dot(x_ref[...], w_ref[...], preferred_element_type=F32).astype(o_ref.dtype)


def matmul(x, w, out_dtype, layer=None, tm=1024, tn=1024):
    m, k = x.shape
    n = w.shape[-1]
    tm = _pick_rows(m, tm)
    tn = _pick_tile(n, tn)
    if w.ndim == 3:
        w_spec = pl.BlockSpec((None, k, tn), lambda j, i: (layer, 0, j))
    else:
        w_spec = pl.BlockSpec((k, tn), lambda j, i: (0, j))
    est = 2 * (tm * k * 2 + k * tn * 2 + tm * tn * jnp.dtype(out_dtype).itemsize) + tm * tn * 4
    return pl.pallas_call(
        _mm_kernel,
        grid=(n // tn, m // tm),
        in_specs=[pl.BlockSpec((tm, k), lambda j, i: (i, 0)), w_spec],
        out_specs=pl.BlockSpec((tm, tn), lambda j, i: (i, j)),
        out_shape=jax.ShapeDtypeStruct((m, n), out_dtype),
        compiler_params=_cparams(("arbitrary", "arbitrary"), est),
    )(x, w)


def _swiglu_kernel(x_ref, w1_ref, w3_ref, o_ref):
    x = x_ref[...]
    a = jnp.dot(x, w1_ref[...], preferred_element_type=F32)
    b = jnp.dot(x, w3_ref[...], preferred_element_type=F32)
    o_ref[...] = (a * jax.nn.sigmoid(a) * b).astype(o_ref.dtype)


def swiglu_up(x, w1, w3, layer, tm=1024, tn=512):
    m, k = x.shape
    n = w1.shape[-1]
    tm = _pick_rows(m, tm)
    tn = _pick_tile(n, tn)
    w_spec = pl.BlockSpec((None, k, tn), lambda j, i: (layer, 0, j))
    est = 2 * (tm * k * 2 + 2 * k * tn * 2 + tm * tn * 2) + 3 * tm * tn * 4
    return pl.pallas_call(
        _swiglu_kernel,
        grid=(n // tn, m // tm),
        in_specs=[pl.BlockSpec((tm, k), lambda j, i: (i, 0)), w_spec, w_spec],
        out_specs=pl.BlockSpec((tm, tn), lambda j, i: (i, j)),
        out_shape=jax.ShapeDtypeStruct((m, n), BF16),
        compiler_params=_cparams(("arbitrary", "arbitrary"), est),
    )(x, w1, w3)


def _down_kernel(x_ref, w_ref, o_ref):
    p = jnp.dot(x_ref[...], w_ref[...], preferred_element_type=F32)

    @pl.when(pl.program_id(2) == 0)
    def _():
        o_ref[...] = p

    @pl.when(pl.program_id(2) > 0)
    def _():
        o_ref[...] += p


def down_proj(x, w, tm=2048, tn=1024, tk=1792):
    m, k = x.shape
    n = w.shape[-1]
    tm = _pick_rows(m, tm)
    tn = _pick_tile(n, tn)
    tk = _pick_tile(k, tk)
    est = 2 * (tm * tk * 2 + tk * tn * 2 + tm * tn * 4) + tm * tn * 4
    return pl.pallas_call(
        _down_kernel,
        grid=(m // tm, n // tn, k // tk),
        in_specs=[pl.BlockSpec((tm, tk), lambda i, j, s: (i, s)), pl.BlockSpec((tk, tn), lambda i, j, s: (s, j))],
        out_specs=pl.BlockSpec((tm, tn), lambda i, j, s: (i, j)),
        out_shape=jax.ShapeDtypeStruct((m, n), F32),
        compiler_params=_cparams(("arbitrary", "arbitrary", "arbitrary"), est),
    )(x, w)


def _moe_up_kernel(rt_ref, nt_ref, ex_ref, ok_ref, x_ref, w1_ref, w3_ref, o_ref):
    @pl.when(ok_ref[pl.program_id(0)] == 1)
    def _():
        x = x_ref[...]
        a = jnp.dot(x, w1_ref[...], preferred_element_type=F32)
        b = jnp.dot(x, w3_ref[...], preferred_element_type=F32)
        o_ref[...] = (a * jax.nn.sigmoid(a) * b).astype(o_ref.dtype)


def _moe_down_kernel(rt_ref, nt_ref, ex_ref, ok_ref, x_ref, w_ref, g_ref, o_ref):
    @pl.when(ok_ref[pl.program_id(0)] == 1)
    def _():
        o_ref[...] = jnp.dot(x_ref[...], w_ref[...], preferred_element_type=F32) * g_ref[...]


def _moe_items(tiles_e, tile0_e, nj, n_tiles):
    n_items = n_tiles * nj
    per_e = tiles_e * nj
    end_e = jnp.cumsum(per_e)
    total = end_e[-1]
    pos = jnp.arange(n_items, dtype=jnp.int32)
    idx = jnp.minimum(pos, total - 1)
    ex = jnp.searchsorted(end_e, idx, side="right").astype(jnp.int32)
    local = idx - (end_e - per_e)[ex]
    te = jnp.maximum(tiles_e[ex], 1)
    rt = tile0_e[ex] + local % te
    nt = local // te
    return rt.astype(jnp.int32), nt.astype(jnp.int32), ex, (pos < total).astype(jnp.int32)


def moe_experts(u, logits, w1, w3, w2, group, n_exp):
    n, d = u.shape
    f = w1.shape[-1]
    tm = min(MOE_ROWS, n)
    top_v, top_i = lax.top_k(logits, TOP_K)
    wts = jax.nn.softmax(top_v, axis=-1)
    n_pairs = n * TOP_K
    n_rows = -(-n_pairs // tm) * tm + n_exp * tm
    n_tiles = n_rows // tm

    e_flat = top_i.reshape(-1).astype(jnp.int32)
    order = jnp.argsort(e_flat, stable=True)
    cnt = jnp.sum(jax.nn.one_hot(e_flat, n_exp, dtype=jnp.int32), axis=0)
    tiles_e = (cnt + tm - 1) // tm
    tile0_e = jnp.cumsum(tiles_e) - tiles_e
    e_sorted = e_flat[order]
    rank = jnp.arange(n_pairs, dtype=jnp.int32) - (jnp.cumsum(cnt) - cnt)[e_sorted]
    dest_sorted = tile0_e[e_sorted] * tm + rank
    dest = jnp.zeros((n_pairs,), jnp.int32).at[order].set(dest_sorted)
    row_token = jnp.zeros((n_rows,), jnp.int32).at[dest].set(jnp.arange(n_pairs, dtype=jnp.int32) // TOP_K)
    row_gate = jnp.zeros((n_rows,), F32).at[dest].set(wts.reshape(-1))

    xg = jnp.take(u, row_token, axis=0)

    tn = _pick_tile(f, 512)
    items = _moe_items(tiles_e, tile0_e, f // tn, n_tiles)
    w_spec = pl.BlockSpec((None, d, tn), lambda i, rt, nt, ex, ok: (group * n_exp + ex[i], 0, nt[i]))
    est = 2 * (tm * d * 2 + 2 * d * tn * 2 + tm * tn * 2) + 3 * tm * tn * 4
    act = pl.pallas_call(
        _moe_up_kernel,
        grid_spec=pltpu.PrefetchScalarGridSpec(
            num_scalar_prefetch=4, grid=(n_tiles * (f // tn),),
            in_specs=[pl.BlockSpec((tm, d), lambda i, rt, nt, ex, ok: (rt[i], 0)), w_spec, w_spec],
            out_specs=pl.BlockSpec((tm, tn), lambda i, rt, nt, ex, ok: (rt[i], nt[i]))),
        out_shape=jax.ShapeDtypeStruct((n_rows, f), BF16),
        compiler_params=_cparams(("arbitrary",), est),
    )(*items, xg, w1, w3)

    tn2 = _pick_tile(d, 512)
    items2 = _moe_items(tiles_e, tile0_e, d // tn2, n_tiles)
    est = 2 * (tm * f * 2 + f * tn2 * 2 + tm * tn2 * 4 + tm * LANE * 4) + tm * tn2 * 4
    y = pl.pallas_call(
        _moe_down_kernel,
        grid_spec=pltpu.PrefetchScalarGridSpec(
            num_scalar_prefetch=4, grid=(n_tiles * (d // tn2),),
            in_specs=[pl.BlockSpec((tm, f), lambda i, rt, nt, ex, ok: (rt[i], 0)),
                      pl.BlockSpec((None, f, tn2), lambda i, rt, nt, ex, ok: (group * n_exp + ex[i], 0, nt[i])),
                      pl.BlockSpec((tm, 1), lambda i, rt, nt, ex, ok: (rt[i], 0))],
            out_specs=pl.BlockSpec((tm, tn2), lambda i, rt, nt, ex, ok: (rt[i], nt[i]))),
        out_shape=jax.ShapeDtypeStruct((n_rows, d), F32),
        compiler_params=_cparams(("arbitrary",), est),
    )(*items2, act, w2, row_gate[:, None])

    dest = dest.reshape(n, TOP_K)
    return jnp.take(y, dest[:, 0], axis=0) + jnp.take(y, dest[:, 1], axis=0)


def _modulate_kernel(x_ref, scale_ref, shift_ref, u_ref):
    u_ref[...] = (x_ref[...] * (1.0 + scale_ref[0, 0]) + shift_ref[0, 0]).astype(u_ref.dtype)


def _ln_kernel(x_ref, y_ref, gate_ref, g_ref, b_ref, scale_ref, shift_ref, xo_ref, uo_ref, *, alpha):
    z = alpha * x_ref[...] + gate_ref[0, 0] * y_ref[...]
    zc = z - jnp.mean(z, axis=-1, keepdims=True)
    yn = zc * lax.rsqrt(jnp.mean(zc * zc, axis=-1, keepdims=True) + LN_EPS)
    xn = yn * g_ref[...] + b_ref[...]
    xo_ref[...] = xn
    uo_ref[...] = (xn * (1.0 + scale_ref[0, 0]) + shift_ref[0, 0]).astype(uo_ref.dtype)


def _ln_last_kernel(x_ref, y_ref, gate_ref, g_ref, b_ref, xo_ref, *, alpha):
    z = alpha * x_ref[...] + gate_ref[0, 0] * y_ref[...]
    zc = z - jnp.mean(z, axis=-1, keepdims=True)
    yn = zc * lax.rsqrt(jnp.mean(zc * zc, axis=-1, keepdims=True) + LN_EPS)
    xo_ref[...] = yn * g_ref[...] + b_ref[...]


class _Tokens:
    def __init__(self, n_ctx, t_lat, n_lat_batches, d):
        self.n_ctx, self.t_lat, self.d = n_ctx, t_lat, d
        self.n = n_ctx + t_lat * n_lat_batches
        tm = 256
        while n_ctx % tm or t_lat % tm:
            tm //= 2
        self.tm = tm
        self.ctx_tiles = n_ctx // tm
        self.tiles_per_lat = t_lat // tm

    def mod_row(self, i):
        return jnp.where(i < self.ctx_tiles, 0, 1 + (i - self.ctx_tiles) // self.tiles_per_lat)

    def mod_spec(self, idx):
        return pl.BlockSpec((1, 1, 1, self.d), lambda i: (idx, self.mod_row(i), 0, 0))

    def row_spec(self):
        return pl.BlockSpec((self.tm, self.d), lambda i: (i, 0))

    def vec_spec(self, idx):
        return pl.BlockSpec((None, 1, self.d), lambda i: (idx, 0, 0))


def modulate(tok, x, mod, idx_scale, idx_shift):
    est = 2 * tok.tm * tok.d * 6 + 4 * tok.tm * tok.d * 4
    return pl.pallas_call(
        _modulate_kernel,
        grid=(tok.n // tok.tm,),
        in_specs=[tok.row_spec(), tok.mod_spec(idx_scale), tok.mod_spec(idx_shift)],
        out_specs=tok.row_spec(),
        out_shape=jax.ShapeDtypeStruct((tok.n, tok.d), BF16),
        compiler_params=_cparams(("arbitrary",), est),
    )(x, mod, mod)


def residual_ln(tok, x, y, mod, idx_gate, ln_gb, idx_g, idx_b, alpha, idx_scale=None, idx_shift=None):
    est = 2 * tok.tm * tok.d * (4 + 4 + 4 + 2) + 6 * tok.tm * tok.d * 4
    in_specs = [tok.row_spec(), tok.row_spec(), tok.mod_spec(idx_gate), tok.vec_spec(idx_g), tok.vec_spec(idx_b)]
    args = [x, y, mod, ln_gb, ln_gb]
    if idx_scale is None:
        return pl.pallas_call(
            functools.partial(_ln_last_kernel, alpha=alpha),
            grid=(tok.n // tok.tm,),
            in_specs=in_specs,
            out_specs=tok.row_spec(),
            out_shape=jax.ShapeDtypeStruct((tok.n, tok.d), F32),
            compiler_params=_cparams(("arbitrary",), est),
        )(*args), None
    in_specs += [tok.mod_spec(idx_scale), tok.mod_spec(idx_shift)]
    args += [mod, mod]
    return pl.pallas_call(
        functools.partial(_ln_kernel, alpha=alpha),
        grid=(tok.n // tok.tm,),
        in_specs=in_specs,
        out_specs=[tok.row_spec(), tok.row_spec()],
        out_shape=[jax.ShapeDtypeStruct((tok.n, tok.d), F32), jax.ShapeDtypeStruct((tok.n, tok.d), BF16)],
        compiler_params=_cparams(("arbitrary",), est),
    )(*args)


def _softmax_pv(s, v):
    m = jnp.max(s, axis=-1, keepdims=True)
    p = jnp.exp(s - m)
    l = jnp.sum(p, axis=-1, keepdims=True)
    return jnp.dot(p.astype(BF16), v, preferred_element_type=F32) / l


def _mla_attn_kernel(q_ref, kv_ref, kr_ref, o_ref, *, scale):
    q = q_ref[0]
    kv = kv_ref[0]
    s = _dot_nt(q[:, :MLA_NOPE_DIM], kv[:, :MLA_NOPE_DIM]) + _dot_nt(q[:, MLA_NOPE_DIM:], kr_ref[0])
    o_ref[0] = _softmax_pv(s * scale, kv[:, MLA_NOPE_DIM:]).astype(o_ref.dtype)


def _gqa_attn_kernel(q_ref, k_ref, v_ref, o_ref, *, scale):
    s = _dot_nt(q_ref[0], k_ref[0])
    o_ref[0] = _softmax_pv(s * scale, v_ref[0]).astype(o_ref.dtype)


def _diff_attn_kernel(q_ref, k_ref, v_ref, lam_ref, g_ref, o_ref, *, scale, dd):
    q = q_ref[0]
    k = k_ref[0]
    v = v_ref[0]
    lane = lax.broadcasted_iota(jnp.int32, q.shape, 1)
    zero = jnp.zeros_like(q)
    q1 = jnp.where(lane < dd, q, zero)
    q2 = jnp.where(lane < dd, zero, q)
    a1 = _softmax_pv(_dot_nt(q1, k) * scale, v)
    a2 = _softmax_pv(_dot_nt(q2, k) * scale, v)
    d = a1 - lam_ref[...] * a2
    y = d * lax.rsqrt(jnp.mean(d * d, axis=-1, keepdims=True) + RMS_EPS)
    o_ref[0] = (y * g_ref[...]).astype(o_ref.dtype)


def _attn_call(body, q, kvs, extra, heads, hw_q, kv_group, hw_o, tq):
    b, t, _ = q.shape
    tq = _pick_rows(t, tq)
    in_specs = [pl.BlockSpec((1, tq, hw_q), lambda bi, h, qi: (bi, qi, h))]
    est = 2 * tq * hw_q * 2
    length = kvs[0][0].shape[1]
    for arr, hw, shared in kvs:
        if shared:
            in_specs.append(pl.BlockSpec((1, length, hw), lambda bi, h, qi: (bi, 0, 0)))
        else:
            in_specs.append(pl.BlockSpec((1, length, hw), lambda bi, h, qi: (bi, 0, h // kv_group)))
        est += 2 * length * hw * 2
    for arr in extra:
        in_specs.append(pl.BlockSpec(arr.shape, lambda bi, h, qi: (0, 0)))
    est += 2 * tq * hw_o * 2 + 6 * tq * length * 4
    return pl.pallas_call(
        body,
        grid=(b, heads, t // tq),
        in_specs=in_specs,
        out_specs=pl.BlockSpec((1, tq, hw_o), lambda bi, h, qi: (bi, qi, h)),
        out_shape=jax.ShapeDtypeStruct((b, t, heads * hw_o), BF16),
        compiler_params=_cparams(("arbitrary", "arbitrary", "arbitrary"), est),
    )(q, *[a for a, _, _ in kvs], *extra)


def _hgrn_kernel(*refs, t, nb, chunk, sub, q_scale, has_s0, want_state):
    it = iter(refs)
    dq_ref, di_ref, df_ref, dfb_ref, dg_ref, tab_ref = (next(it) for _ in range(6))
    s0_ref = next(it) if has_s0 else None
    o_ref = next(it)
    st_out_ref = next(it) if want_state else None
    st_scr, o_scr = next(it), next(it)

    n_chunks = t // chunk
    n_sub = chunk // sub
    row = lax.broadcasted_iota(jnp.int32, (chunk, chunk), 0)
    col = lax.broadcasted_iota(jnp.int32, (chunk, chunk), 1)
    key_row = lax.broadcasted_iota(jnp.int32, (chunk, 1), 0)
    sub_row = lax.broadcasted_iota(jnp.int32, (sub, 1), 0)
    neg_inf = jnp.float32(-jnp.inf)
    tab = tab_ref[...]
    tri_f = jnp.where(row >= col, 1.0, 0.0).astype(F32)
    tri_b = jnp.where(row <= col, 1.0, 0.0).astype(F32)

    for bb in range(nb):
        for direction in range(2):
            if has_s0:
                st_scr[2 * bb + direction] = s0_ref[bb, direction, 0].T
            else:
                st_scr[2 * bb + direction] = jnp.zeros(st_scr.shape[1:], F32)

    def chain_step(ci, bb, direction):
        rev = direction == 1
        z_ref = dfb_ref if rev else df_ref
        tri = tri_b if rev else tri_f
        log_lb = tab[3 * direction:3 * direction + 1, :]
        log_1m = tab[3 * direction + 1:3 * direction + 2, :]
        one_m = tab[3 * direction + 2:3 * direction + 3, :]
        r0 = pl.multiple_of(bb * t + ((n_chunks - 1 - ci) if rev else ci) * chunk, chunk)
        zq = dq_ref[pl.ds(r0, chunk), :]
        q = zq * jax.nn.sigmoid(zq) * q_scale
        v = di_ref[pl.ds(r0, chunk), :]
        z = z_ref[pl.ds(r0, chunk), :]
        log_sig = jnp.minimum(z, 0.0) - jnp.log1p(jnp.exp(-jnp.abs(z)))
        t2 = log_1m + log_sig
        g = jnp.maximum(log_lb, t2) + jnp.log1p(jnp.exp(-jnp.abs(log_lb - t2)))
        k = one_m * jax.nn.sigmoid(-z)
        cum = jnp.dot(tri, g, precision=lax.Precision.HIGHEST, preferred_element_type=F32)
        edge = cum[0:1, :] if rev else cum[chunk - 1:chunk, :]
        st = st_scr[2 * bb + direction]
        vb = v.astype(BF16)
        inter = _dot_nt((q * jnp.exp(cum)).astype(BF16), st.astype(BF16))
        outs = []
        for i in range(n_sub):
            lo = i * sub
            qi = q[lo:lo + sub]
            cumi = cum[lo:lo + sub]
            acc = inter[lo:lo + sub]
            if (i < n_sub - 1) if rev else (i > 0):
                base = cum[lo + sub:lo + sub + 1, :] if rev else cum[lo - 1:lo, :]
                seen = (key_row >= lo + sub) if rev else (key_row < lo)
                qn = (qi * jnp.exp(cumi - base)).astype(BF16)
                kn = (k * jnp.exp(jnp.where(seen, base - cum, neg_inf))).astype(BF16)
                att = _dot_nt(qn, kn)
                acc = acc + jnp.dot(att.astype(BF16), vb, preferred_element_type=F32)
            for s in range(sub):
                r = lo + s
                live = (sub_row <= s) if rev else (sub_row >= s)
                w = jnp.exp(jnp.where(live, cumi - cum[r:r + 1, :], neg_inf))
                colv = jnp.sum(qi * k[r:r + 1, :] * w, axis=-1, keepdims=True)
                acc = acc + colv * v[r:r + 1, :]
            outs.append(acc)
        o_scr[direction, pl.ds(r0, chunk), :] = jnp.concatenate(outs, axis=0)
        kl = (k * jnp.exp(edge - cum)).astype(BF16)
        st_scr[2 * bb + direction] = st * jnp.exp(edge) + _dot_tn(vb, kl)

    def body(ci, carry):
        for bb in range(nb):
            for direction in range(2):
                chain_step(ci, bb, direction)
        return carry

    lax.fori_loop(0, n_chunks, body, 0)

    tot = o_scr[0] + o_scr[1]
    zg = dg_ref[...]
    y = tot * lax.rsqrt(jnp.mean(tot * tot, axis=-1, keepdims=True) + RMS_EPS) * tab[6:7, :]
    o_ref[...] = (y * (zg * jax.nn.sigmoid(zg))).astype(o_ref.dtype)
    if want_state:
        for bb in range(nb):
            for direction in range(2):
                st_out_ref[bb, direction, 0] = st_scr[2 * bb + direction].T


def hgrn_mix(h, row0, b, t, heads, dk, dv, cols, tab, s0, want_state):
    assert dk == LANE and dv == LANE
    nb = HGRN_BATCHES if (b % HGRN_BATCHES == 0 and row0 % (HGRN_BATCHES * t) == 0) else 1
    assert row0 % (nb * t) == 0 and all(cc % LANE == 0 for cc in cols)
    chunk = min(HGRN_CHUNK, t)
    sub = min(HGRN_SUB, chunk)
    rb0 = row0 // (nb * t)

    def col_spec(c0):
        return pl.BlockSpec((nb * t, LANE), lambda bi, hh: (rb0 + bi, c0 // LANE + hh))

    st_spec = pl.BlockSpec((nb, 2, 1, dk, dv), lambda bi, hh: (bi, 0, hh, 0, 0))
    in_specs = [col_spec(cc) for cc in cols] + [pl.BlockSpec((8, LANE), lambda bi, hh: (0, hh))]
    args = [h] * len(cols) + [tab]
    if s0 is not None:
        in_specs.append(st_spec)
        args.append(s0)
    out_specs = [pl.BlockSpec((nb * t, LANE), lambda bi, hh: (bi, hh))]
    out_shape = [jax.ShapeDtypeStruct((b * t, heads * dv), BF16)]
    if want_state:
        out_specs.append(st_spec)
        out_shape.append(jax.ShapeDtypeStruct((b, 2, heads, dk, dv), F32))
    est = 2 * nb * t * LANE * (5 * 4 + 2) + 2 * nb * t * LANE * 4 + (8 * nb + 6) * dk * dv * 4 + (4 << 20)
    res = pl.pallas_call(
        functools.partial(_hgrn_kernel, t=t, nb=nb, chunk=chunk, sub=sub, q_scale=dk ** -0.5,
                          has_s0=s0 is not None, want_state=want_state),
        grid=(b // nb, heads),
        in_specs=in_specs,
        out_specs=out_specs,
        out_shape=out_shape,
        scratch_shapes=[pltpu.VMEM((2 * nb, dv, dk), F32), pltpu.VMEM((2, nb * t, LANE), F32)],
        compiler_params=_cparams(("arbitrary", "arbitrary"), est),
    )(*args)
    return (res[0], res[1]) if want_state else (res[0], None)


def _rms(x, g):
    return x * lax.rsqrt(jnp.mean(x * x, axis=-1, keepdims=True) + RMS_EPS) * g


def _rope_tables(t, d):
    n = d // 4
    inv = ROPE_THETA ** (-jnp.arange(n, dtype=F32) / n)
    pos = jnp.arange(t)
    rows = (pos // GRID_W).astype(F32)
    cols = (pos % GRID_W).astype(F32)
    ar = rows[:, None] * inv[None, :]
    ac = cols[:, None] * inv[None, :]
    cos = jnp.concatenate([jnp.cos(ar), jnp.cos(ar), jnp.cos(ac), jnp.cos(ac)], axis=-1)
    sin = jnp.concatenate([jnp.sin(ar), jnp.sin(ar), jnp.sin(ac), jnp.sin(ac)], axis=-1)
    return cos, sin


def _rope(x, tables):
    if tables is None:
        return x
    cos, sin = tables
    d = x.shape[-1]
    n = d // 4
    xq = x.reshape(x.shape[:-1] + (2, 2, n))
    rot = jnp.stack([-xq[..., 1, :], xq[..., 0, :]], axis=-2).reshape(x.shape)
    return x * cos[None, :, None, :] + rot * sin[None, :, None, :]


def kernel(x_prompt, x_sample, cache_mla_ckv, cache_mla_krope, cache_diff_k, cache_diff_v, cache_gqa_k, cache_gqa_v, state_hgrn, c, c_ctx, w_in, w_out, mla_g_cq, mla_g_ckv, mla_w_uq, mla_w_ukv, diff_lambda, diff_subln_g, gqa_g_q, gqa_g_k, hgrn_gamma, hgrn_g_norm, ada_w, ada_b, ln_g, ln_b, ffn_w1, ffn_w3, ffn_w2, moe_router, moe_w1, moe_w3, moe_w2):
    bc, tc, d = x_prompt.shape
    bl, tl, _ = x_sample.shape
    depth = w_in.shape[0]
    q_rank = mla_g_cq.shape[1]
    kv_rank = mla_g_ckv.shape[1]
    rope_d = cache_mla_krope.shape[-1]
    heads = cache_diff_k.shape[3]
    dd = cache_diff_k.shape[4] // 2
    kv_heads = cache_gqa_k.shape[3]
    gd = cache_gqa_k.shape[4]
    hk = state_hgrn.shape[4]
    hv = state_hgrn.shape[5]
    n_exp = moe_w1.shape[1]
    alpha = (2 * depth) ** 0.25
    n_ctx = bc * tc
    n_lat = bl * tl
    tok = _Tokens(n_ctx, tl, bl, d)
    mla_hw = 2 * LANE
    assert MLA_NOPE_DIM + rope_d <= mla_hw and MLA_NOPE_DIM + MLA_V_DIM == mla_hw

    w_a = q_rank + kv_rank
    w_rest = w_in.shape[2] - w_a - rope_d
    n_in = w_a + w_rest + LANE
    n_in_pad = -(-n_in // 1024) * 1024 if n_in > 1024 else n_in
    w_in_b = jnp.concatenate(
        [w_in[:, :, :w_a], w_in[:, :, w_a + rope_d:], w_in[:, :, w_a:w_a + rope_d],
         jnp.zeros((depth, d, n_in_pad - n_in + LANE - rope_d), w_in.dtype)], axis=-1).astype(BF16)
    off_kr = w_a + w_rest
    w_uq_b = jnp.pad(mla_w_uq.reshape(depth, q_rank, heads, MLA_NOPE_DIM + rope_d),
                     ((0, 0), (0, 0), (0, 0), (0, mla_hw - MLA_NOPE_DIM - rope_d))
                     ).reshape(depth, q_rank, heads * mla_hw).astype(BF16)
    w_ukv_b = mla_w_ukv.astype(BF16)
    w_out_b = w_out.astype(BF16)
    ffn_w1_b, ffn_w3_b, ffn_w2_b = ffn_w1.astype(BF16), ffn_w3.astype(BF16), ffn_w2.astype(BF16)
    e_ff = moe_w1.shape[-1]
    n_moe = moe_w1.shape[0]
    moe_w1_b = moe_w1.astype(BF16).reshape(n_moe * n_exp, d, e_ff)
    moe_w3_b = moe_w3.astype(BF16).reshape(n_moe * n_exp, d, e_ff)
    moe_w2_b = moe_w2.astype(BF16).reshape(n_moe * n_exp, e_ff, d)
    router_b = jnp.pad(moe_router, ((0, 0), (0, 0), (0, LANE - n_exp))).astype(BF16)
    ada_w_b = ada_w.astype(BF16)

    cond = jnp.concatenate([c_ctx[None, :], c], axis=0)
    n_cond = cond.shape[0]
    cond_p = jnp.pad(jax.nn.silu(cond), ((0, -n_cond % 16), (0, 0))).astype(BF16)
    mods = []
    for l in range(depth):
        ml = matmul(cond_p, ada_w_b, F32, layer=l)[:n_cond] + ada_b[l][None, :]
        mods.append(ml.reshape(n_cond, 6, d).transpose(1, 0, 2))
    mod = jnp.concatenate(mods, axis=0).reshape(depth * 6, n_cond, 1, d)
    ln_gb = jnp.concatenate([ln_g.reshape(depth * 2, 1, d), ln_b.reshape(depth * 2, 1, d)], axis=0)

    def lower_bounds(gamma):
        cs = jnp.cumsum(jax.nn.softmax(gamma.astype(F32), axis=0), axis=0)
        return cs - cs[0:1]

    lb_f = lower_bounds(hgrn_gamma[0])
    lb_b = lower_bounds(hgrn_gamma[1])
    tables = {w: _rope_tables(tl, w) for w in {rope_d, dd, gd}}

    def mixer(h, row0, l, b, t, ctx):
        is_ctx = ctx is None
        hg = h[row0:row0 + b * t]
        tb = (lambda w: None) if is_ctx else (lambda w: tables[w])
        cat = (lambda c0, z: z) if is_ctx else (lambda c0, z: jnp.concatenate([c0.astype(z.dtype), z], axis=1))
        o = 0

        def take(w):
            nonlocal o
            z = hg[:, o:o + w]
            o += w
            return z

        a_cq, a_ckv = take(q_rank), take(kv_rank)
        b_q, b_k, b_v = take(heads * 2 * dd), take(heads * 2 * dd), take(heads * 2 * dd)
        c_q, c_k, c_v = take(heads * gd), take(kv_heads * gd), take(kv_heads * gd)
        hgrn_cols = []
        for w in (heads * hk, heads * hv, heads * hk, heads * hk, heads * hv):
            hgrn_cols.append(o)
            o += w
        a_kr = hg[:, off_kr:off_kr + rope_d]

        cq = _rms(a_cq, mla_g_cq[l]).astype(BF16)
        q = matmul(cq, w_uq_b, F32, layer=l).reshape(b, t, heads, mla_hw)
        q = jnp.concatenate([q[..., :MLA_NOPE_DIM], _rope(q[..., MLA_NOPE_DIM:MLA_NOPE_DIM + rope_d], tb(rope_d)),
                             q[..., MLA_NOPE_DIM + rope_d:]], axis=-1).reshape(b, t, heads * mla_hw).astype(BF16)
        ckv = _rms(a_ckv, mla_g_ckv[l]).reshape(b, t, kv_rank)
        kr = _rope(a_kr.reshape(b, t, 1, rope_d), tb(rope_d)).reshape(b, t, rope_d)
        ckv_all = cat(None if is_ctx else ctx[0], ckv)
        kr_all = cat(None if is_ctx else ctx[1], kr)
        n_keys = ckv_all.shape[1]
        kv = matmul(ckv_all.reshape(b * n_keys, kv_rank).astype(BF16), w_ukv_b, BF16, layer=l
                    ).reshape(b, n_keys, heads * mla_hw)
        kr_p = jnp.pad(kr_all, ((0, 0), (0, 0), (0, LANE - rope_d))).astype(BF16)
        o_a = _attn_call(
            functools.partial(_mla_attn_kernel, scale=(MLA_NOPE_DIM + rope_d) ** -0.5),
            q, [(kv, mla_hw, False), (kr_p, LANE, True)], [], heads, mla_hw, 1, MLA_V_DIM, 512)

        dq = _rope(b_q.reshape(b, t, 2 * heads, dd), tb(dd)).reshape(b, t, heads * 2 * dd)
        dk = _rope(b_k.reshape(b, t, 2 * heads, dd), tb(dd)).reshape(b, t, heads, 2 * dd)
        dv = b_v.reshape(b, t, heads, 2 * dd)
        dk_all = cat(None if is_ctx else ctx[2], dk).reshape(b, n_keys, heads * 2 * dd)
        dv_all = cat(None if is_ctx else ctx[3], dv).reshape(b, n_keys, heads * 2 * dd)
        lam_p = diff_lambda[l].astype(F32)
        lam_init = 0.8 - 0.6 * math.exp(-0.3 * l)
        lam = jnp.exp(jnp.sum(lam_p[0] * lam_p[1])) - jnp.exp(jnp.sum(lam_p[2] * lam_p[3])) + lam_init
        lam_v = jnp.full((1, 2 * dd), lam, F32)
        g_v = (diff_subln_g[l] * (1.0 - lam_init)).reshape(1, 2 * dd)
        o_b = _attn_call(
            functools.partial(_diff_attn_kernel, scale=dd ** -0.5, dd=dd),
            dq.astype(BF16), [(dk_all.astype(BF16), 2 * dd, False), (dv_all.astype(BF16), 2 * dd, False)],
            [lam_v, g_v], heads, 2 * dd, 1, 2 * dd, 512)

        gq = _rope(_rms(c_q.reshape(b, t, heads, gd), gqa_g_q[l]), tb(gd)).reshape(b, t, heads * gd)
        gk = _rope(_rms(c_k.reshape(b, t, kv_heads, gd), gqa_g_k[l]), tb(gd))
        gv = c_v.reshape(b, t, kv_heads, gd)
        gk_all = cat(None if is_ctx else ctx[4], gk).reshape(b, n_keys, kv_heads * gd)
        gv_all = cat(None if is_ctx else ctx[5], gv).reshape(b, n_keys, kv_heads * gd)
        o_c = _attn_call(
            functools.partial(_gqa_attn_kernel, scale=gd ** -0.5),
            gq.astype(BF16), [(gk_all.astype(BF16), gd, False), (gv_all.astype(BF16), gd, False)],
            [], heads, gd, heads // kv_heads, gd, 512)

        tab = jnp.stack([jnp.log(lb_f[l]), jnp.log1p(-lb_f[l]), 1.0 - lb_f[l],
                         jnp.log(lb_b[l]), jnp.log1p(-lb_b[l]), 1.0 - lb_b[l],
                         jnp.tile(hgrn_g_norm[l], heads), jnp.zeros((heads * hk,), F32)], axis=0)
        o_d, s_new = hgrn_mix(h, row0, b, t, heads, hk, hv, hgrn_cols, tab, None if is_ctx else ctx[6], is_ctx)

        mixed = jnp.concatenate([o_a.reshape(b * t, -1), o_b.reshape(b * t, -1), o_c.reshape(b * t, -1), o_d], axis=-1)
        new_ctx = (ckv, kr, dk, dv, gk, gv, s_new) if is_ctx else None
        return mixed, new_ctx

    x = jnp.concatenate([x_prompt.reshape(n_ctx, d), x_sample.reshape(n_lat, d)], axis=0)
    u = modulate(tok, x, mod, 1, 0)
    ctx_out = []
    for l in range(depth):
        h = matmul(u, w_in_b, F32, layer=l)
        ctx_l = (cache_mla_ckv[:, l], cache_mla_krope[:, l], cache_diff_k[:, l], cache_diff_v[:, l],
                 cache_gqa_k[:, l], cache_gqa_v[:, l], state_hgrn[:, l])
        mixed_c, new_ctx = mixer(h, 0, l, bc, tc, None)
        mixed_l, _ = mixer(h, n_ctx, l, bl, tl, ctx_l)
        ctx_out.append(new_ctx)
        mix = matmul(jnp.concatenate([mixed_c, mixed_l], axis=0), w_out_b, F32, layer=l)
        x, u = residual_ln(tok, x, mix, mod, l * 6 + 2, ln_gb, 2 * l, 2 * depth + 2 * l, alpha,
                           l * 6 + 4, l * 6 + 3)
        j = l // 2
        if l % 2 == 0:
            f = down_proj(swiglu_up(u, ffn_w1_b, ffn_w3_b, j), ffn_w2_b[j])
        else:
            logits = matmul(u, router_b, F32, layer=j)[:, :n_exp]
            f = moe_experts(u, logits, moe_w1_b, moe_w3_b, moe_w2_b, j, n_exp)
        if l + 1 < depth:
            x, u = residual_ln(tok, x, f, mod, l * 6 + 5, ln_gb, 2 * l + 1, 2 * depth + 2 * l + 1, alpha,
                               (l + 1) * 6 + 1, (l + 1) * 6 + 0)
        else:
            x, _ = residual_ln(tok, x, f, mod, l * 6 + 5, ln_gb, 2 * l + 1, 2 * depth + 2 * l + 1, alpha)

    y_prompt = x[:n_ctx].reshape(bc, tc, d)
    y_sample = x[n_ctx:].reshape(bl, tl, d)
    stack = lambda i: jnp.stack([e[i] for e in ctx_out], axis=1)
    return (y_prompt, y_sample, stack(0), stack(1), stack(2), stack(3), stack(4), stack(5), stack(6))
```

```python
import functools
import math

import jax
import jax.numpy as jnp
from jax import lax
from jax.experimental import pallas as pl
from jax.experimental.pallas import tpu as pltpu

F32 = jnp.float32
BF16 = jnp.bfloat16

GRID_W = 64
ROPE_THETA = 10000.0
LN_EPS = 1e-5
RMS_EPS = 1e-6
MLA_NOPE_DIM = 128
MLA_V_DIM = 128
TOP_K = 2

LANE = 128
V7X_VMEM_BYTES = 64 * 1024 * 1024
VMEM_CAP = V7X_VMEM_BYTES - 6 * 1024 * 1024
HGRN_CHUNK = 64
HGRN_SUB = 16
HGRN_BATCHES = 2
MOE_ROWS = 512


def _cparams(sem, est_bytes):
    limit = int(min(VMEM_CAP, max(32 * 1024 * 1024, est_bytes * 5 // 4 + (4 << 20))))
    return pltpu.CompilerParams(dimension_semantics=sem, vmem_limit_bytes=limit)


def _pick_tile(n, target):
    if n <= target:
        return n
    t = (target // LANE) * LANE
    while t >= LANE:
        if n % t == 0:
            return t
        t -= LANE
    return n


def _pick_rows(n, target):
    if n <= target:
        return n
    t = target
    while t >= 8:
        if n % t == 0:
            return t
        t //= 2
    return n


def _dot_nt(a, b):
    return lax.dot_general(a, b, (((1,), (1,)), ((), ())), preferred_element_type=F32)


def _dot_tn(a, b):
    return lax.dot_general(a, b, (((0,), (0,)), ((), ())), preferred_element_type=F32)


def _mm_kernel(x_ref, w_ref, o_ref):
    o_ref[...] = jnp.dot(x_ref[...], w_ref[...], preferred_element_type=F32).astype(o_ref.dtype)


def matmul(x, w, out_dtype, layer=None, tm=1024, tn=1024):
    m, k = x.shape
    n = w.shape[-1]
    tm = _pick_rows(m, tm)
    tn = _pick_tile(n, tn)
    if w.ndim == 3:
        w_spec = pl.BlockSpec((None, k, tn), lambda j, i: (layer, 0, j))
    else:
        w_spec = pl.BlockSpec((k, tn), lambda j, i: (0, j))
    est = 2 * (tm * k * 2 + k * tn * 2 + tm * tn * jnp.dtype(out_dtype).itemsize) + tm * tn * 4
    return pl.pallas_call(
        _mm_kernel,
        grid=(n // tn, m // tm),
        in_specs=[pl.BlockSpec((tm, k), lambda j, i: (i, 0)), w_spec],
        out_specs=pl.BlockSpec((tm, tn), lambda j, i: (i, j)),
        out_shape=jax.ShapeDtypeStruct((m, n), out_dtype),
        compiler_params=_cparams(("arbitrary", "arbitrary"), est),
    )(x, w)


def _mm_castw_kernel(x_ref, w_ref, o_ref):
    o_ref[...] = jnp.dot(x_ref[...], w_ref[...].astype(BF16), preferred_element_type=F32).astype(o_ref.dtype)


def matmul_few_rows(x, w, layer, tn=1024):
    m, k = x.shape
    n = w.shape[-1]
    tn = _pick_tile(n, tn)
    est = 2 * (m * k * 2 + k * tn * 4 + m * tn * 4) + k * tn * 2
    return pl.pallas_call(
        _mm_castw_kernel,
        grid=(n // tn,),
        in_specs=[pl.BlockSpec((m, k), lambda j: (0, 0)), pl.BlockSpec((None, k, tn), lambda j: (layer, 0, j))],
        out_specs=pl.BlockSpec((m, tn), lambda j: (0, j)),
        out_shape=jax.ShapeDtypeStruct((m, n), F32),
        compiler_params=_cparams(("arbitrary",), est),
    )(x, w)


def _rms_mm_kernel(x_ref, g_ref, w_ref, o_ref):
    x = x_ref[:, :w_ref.shape[0]]
    xn = x * lax.rsqrt(jnp.mean(x * x, axis=-1, keepdims=True) + RMS_EPS) * g_ref[...]
    o_ref[...] = jnp.dot(xn.astype(BF16), w_ref[...], preferred_element_type=F32).astype(o_ref.dtype)


def rms_matmul(h, col0, kb, gain, w, layer, out_dtype, tm=1024, tn=1024):
    m = h.shape[0]
    k, n = w.shape[1], w.shape[2]
    assert col0 % kb == 0 and kb >= k
    tm = _pick_rows(m, tm)
    tn = _pick_tile(n, tn)
    est = 2 * (tm * kb * 4 + k * tn * 2 + tm * tn * 4) + 3 * tm * k * 4 + tm * tn * 4
    return pl.pallas_call(
        _rms_mm_kernel,
        grid=(n // tn, m // tm),
        in_specs=[pl.BlockSpec((tm, kb), lambda j, i: (i, col0 // kb)),
                  pl.BlockSpec((1, k), lambda j, i: (0, 0)),
                  pl.BlockSpec((None, k, tn), lambda j, i: (layer, 0, j))],
        out_specs=pl.BlockSpec((tm, tn), lambda j, i: (i, j)),
        out_shape=jax.ShapeDtypeStruct((m, n), out_dtype),
        compiler_params=_cparams(("arbitrary", "arbitrary"), est),
    )(h, gain, w)


def _out_proj_kernel(a_ref, b_ref, c_ref, d_ref, w_ref, o_ref, wb_scr):
    @pl.when(pl.program_id(1) == 0)
    def _():
        wb_scr[...] = w_ref[...].astype(BF16)

    kq = a_ref.shape[1]
    acc = jnp.dot(a_ref[...], wb_scr[0:kq, :], preferred_element_type=F32)
    for g, r in enumerate((b_ref, c_ref, d_ref), start=1):
        acc = acc + jnp.dot(r[...], wb_scr[g * kq:(g + 1) * kq, :], preferred_element_type=F32)
    o_ref[...] = acc


def out_proj(parts, w, layer, tm=1024, tn=512):
    m, kq = parts[0].shape
    k, n = w.shape[1], w.shape[2]
    assert len(parts) == 4 and 4 * kq == k
    tm = _pick_rows(m, tm)
    tn = _pick_tile(n, tn)
    est = 2 * (4 * tm * kq * 2 + k * tn * 4 + tm * tn * 4) + k * tn * 2 + 2 * tm * tn * 4
    part_spec = pl.BlockSpec((tm, kq), lambda j, i: (i, 0))
    return pl.pallas_call(
        _out_proj_kernel,
        grid=(n // tn, m // tm),
        in_specs=[part_spec] * 4 + [pl.BlockSpec((None, k, tn), lambda j, i: (layer, 0, j))],
        out_specs=pl.BlockSpec((tm, tn), lambda j, i: (i, j)),
        out_shape=jax.ShapeDtypeStruct((m, n), F32),
        scratch_shapes=[pltpu.VMEM((k, tn), BF16)],
        compiler_params=_cparams(("arbitrary", "arbitrary"), est),
    )(*parts, w)


def _swiglu_kernel(x_ref, w1_ref, w3_ref, o_ref):
    x = x_ref[...]
    a = jnp.dot(x, w1_ref[...], preferred_element_type=F32)
    b = jnp.dot(x, w3_ref[...], preferred_element_type=F32)
    o_ref[...] = (a * jax.nn.sigmoid(a) * b).astype(o_ref.dtype)


def swiglu_up(x, w1, w3, layer, tm=1024, tn=512):
    m, k = x.shape
    n = w1.shape[-1]
    tm = _pick_rows(m, tm)
    tn = _pick_tile(n, tn)
    w_spec = pl.BlockSpec((None, k, tn), lambda j, i: (layer, 0, j))
    est = 2 * (tm * k * 2 + 2 * k * tn * 2 + tm * tn * 2) + 3 * tm * tn * 4
    return pl.pallas_call(
        _swiglu_kernel,
        grid=(n // tn, m // tm),
        in_specs=[pl.BlockSpec((tm, k), lambda j, i: (i, 0)), w_spec, w_spec],
        out_specs=pl.BlockSpec((tm, tn), lambda j, i: (i, j)),
        out_shape=jax.ShapeDtypeStruct((m, n), BF16),
        compiler_params=_cparams(("arbitrary", "arbitrary"), est),
    )(x, w1, w3)


def _down_kernel(x_ref, w_ref, o_ref):
    p = jnp.dot(x_ref[...], w_ref[...], preferred_element_type=F32)

    @pl.when(pl.program_id(2) == 0)
    def _():
        o_ref[...] = p

    @pl.when(pl.program_id(2) > 0)
    def _():
        o_ref[...] += p


def down_proj(x, w, tm=2048, tn=1024, tk=1792):
    m, k = x.shape
    n = w.shape[-1]
    tm = _pick_rows(m, tm)
    tn = _pick_tile(n, tn)
    tk = _pick_tile(k, tk)
    est = 2 * (tm * tk * 2 + tk * tn * 2 + tm * tn * 4) + tm * tn * 4
    return pl.pallas_call(
        _down_kernel,
        grid=(m // tm, n // tn, k // tk),
        in_specs=[pl.BlockSpec((tm, tk), lambda i, j, s: (i, s)), pl.BlockSpec((tk, tn), lambda i, j, s: (s, j))],
        out_specs=pl.BlockSpec((tm, tn), lambda i, j, s: (i, j)),
        out_shape=jax.ShapeDtypeStruct((m, n), F32),
        compiler_params=_cparams(("arbitrary", "arbitrary", "arbitrary"), est),
    )(x, w)


def _moe_up_kernel(rt_ref, nt_ref, ex_ref, ok_ref, x_ref, w1_ref, w3_ref, o_ref):
    @pl.when(ok_ref[pl.program_id(0)] == 1)
    def _():
        x = x_ref[...]
        a = jnp.dot(x, w1_ref[...], preferred_element_type=F32)
        b = jnp.dot(x, w3_ref[...], preferred_element_type=F32)
        o_ref[...] = (a * jax.nn.sigmoid(a) * b).astype(o_ref.dtype)


def _moe_down_kernel(rt_ref, nt_ref, ex_ref, ok_ref, x_ref, w_ref, g_ref, o_ref):
    @pl.when(ok_ref[pl.program_id(0)] == 1)
    def _():
        o_ref[...] = jnp.dot(x_ref[...], w_ref[...], preferred_element_type=F32) * g_ref[...]


def _moe_items(tiles_e, tile0_e, nj, n_tiles):
    n_items = n_tiles * nj
    per_e = tiles_e * nj
    end_e = jnp.cumsum(per_e)
    total = end_e[-1]
    pos = jnp.arange(n_items, dtype=jnp.int32)
    idx = jnp.minimum(pos, total - 1)
    ex = jnp.searchsorted(end_e, idx, side="right").astype(jnp.int32)
    local = idx - (end_e - per_e)[ex]
    te = jnp.maximum(tiles_e[ex], 1)
    rt = tile0_e[ex] + local % te
    nt = local // te
    return rt.astype(jnp.int32), nt.astype(jnp.int32), ex, (pos < total).astype(jnp.int32)


def moe_experts(u, logits, w1, w3, w2, group, n_exp):
    n, d = u.shape
    f = w1.shape[-1]
    tm = min(MOE_ROWS, n)
    top_v, top_i = lax.top_k(logits, TOP_K)
    wts = jax.nn.softmax(top_v, axis=-1)
    n_pairs = n * TOP_K
    n_rows = -(-n_pairs // tm) * tm + n_exp * tm
    n_tiles = n_rows // tm

    e_flat = top_i.reshape(-1).astype(jnp.int32)
    order = jnp.argsort(e_flat, stable=True)
    cnt = jnp.sum(jax.nn.one_hot(e_flat, n_exp, dtype=jnp.int32), axis=0)
    tiles_e = (cnt + tm - 1) // tm
    tile0_e = jnp.cumsum(tiles_e) - tiles_e
    e_sorted = e_flat[order]
    rank = jnp.arange(n_pairs, dtype=jnp.int32) - (jnp.cumsum(cnt) - cnt)[e_sorted]
    dest_sorted = tile0_e[e_sorted] * tm + rank
    dest = jnp.zeros((n_pairs,), jnp.int32).at[order].set(dest_sorted)
    row_token = jnp.zeros((n_rows,), jnp.int32).at[dest].set(jnp.arange(n_pairs, dtype=jnp.int32) // TOP_K)
    row_gate = jnp.zeros((n_rows,), F32).at[dest].set(wts.reshape(-1))

    xg = jnp.take(u, row_token, axis=0)

    tn = _pick_tile(f, 512)
    items = _moe_items(tiles_e, tile0_e, f // tn, n_tiles)
    w_spec = pl.BlockSpec((None, d, tn), lambda i, rt, nt, ex, ok: (group * n_exp + ex[i], 0, nt[i]))
    est = 2 * (tm * d * 2 + 2 * d * tn * 2 + tm * tn * 2) + 3 * tm * tn * 4
    act = pl.pallas_call(
        _moe_up_kernel,
        grid_spec=pltpu.PrefetchScalarGridSpec(
            num_scalar_prefetch=4, grid=(n_tiles * (f // tn),),
            in_specs=[pl.BlockSpec((tm, d), lambda i, rt, nt, ex, ok: (rt[i], 0)), w_spec, w_spec],
            out_specs=pl.BlockSpec((tm, tn), lambda i, rt, nt, ex, ok: (rt[i], nt[i]))),
        out_shape=jax.ShapeDtypeStruct((n_rows, f), BF16),
        compiler_params=_cparams(("arbitrary",), est),
    )(*items, xg, w1, w3)

    tn2 = _pick_tile(d, 512)
    items2 = _moe_items(tiles_e, tile0_e, d // tn2, n_tiles)
    est = 2 * (tm * f * 2 + f * tn2 * 2 + tm * tn2 * 4 + tm * LANE * 4) + tm * tn2 * 4
    y = pl.pallas_call(
        _moe_down_kernel,
        grid_spec=pltpu.PrefetchScalarGridSpec(
            num_scalar_prefetch=4, grid=(n_tiles * (d // tn2),),
            in_specs=[pl.BlockSpec((tm, f), lambda i, rt, nt, ex, ok: (rt[i], 0)),
                      pl.BlockSpec((None, f, tn2), lambda i, rt, nt, ex, ok: (group * n_exp + ex[i], 0, nt[i])),
                      pl.BlockSpec((tm, 1), lambda i, rt, nt, ex, ok: (rt[i], 0))],
            out_specs=pl.BlockSpec((tm, tn2), lambda i, rt, nt, ex, ok: (rt[i], nt[i]))),
        out_shape=jax.ShapeDtypeStruct((n_rows, d), F32),
        compiler_params=_cparams(("arbitrary",), est),
    )(*items2, act, w2, row_gate[:, None])

    dest = dest.reshape(n, TOP_K)
    return jnp.take(y, dest[:, 0], axis=0) + jnp.take(y, dest[:, 1], axis=0)


def _modulate_kernel(x_ref, scale_ref, shift_ref, u_ref):
    u_ref[...] = (x_ref[...] * (1.0 + scale_ref[0, 0]) + shift_ref[0, 0]).astype(u_ref.dtype)


def _ln_kernel(x_ref, y_ref, gate_ref, g_ref, b_ref, scale_ref, shift_ref, xo_ref, uo_ref, *, alpha):
    z = alpha * x_ref[...] + gate_ref[0, 0] * y_ref[...]
    zc = z - jnp.mean(z, axis=-1, keepdims=True)
    yn = zc * lax.rsqrt(jnp.mean(zc * zc, axis=-1, keepdims=True) + LN_EPS)
    xn = yn * g_ref[...] + b_ref[...]
    xo_ref[...] = xn
    uo_ref[...] = (xn * (1.0 + scale_ref[0, 0]) + shift_ref[0, 0]).astype(uo_ref.dtype)


def _ln_last_kernel(x_ref, y_ref, gate_ref, g_ref, b_ref, xo_ref, *, alpha):
    z = alpha * x_ref[...] + gate_ref[0, 0] * y_ref[...]
    zc = z - jnp.mean(z, axis=-1, keepdims=True)
    yn = zc * lax.rsqrt(jnp.mean(zc * zc, axis=-1, keepdims=True) + LN_EPS)
    xo_ref[...] = yn * g_ref[...] + b_ref[...]


class _Tokens:
    def __init__(self, n_ctx, t_lat, n_lat_batches, d):
        self.n_ctx, self.t_lat, self.d = n_ctx, t_lat, d
        self.n = n_ctx + t_lat * n_lat_batches
        tm = 256
        while n_ctx % tm or t_lat % tm:
            tm //= 2
        self.tm = tm
        self.ctx_tiles = n_ctx // tm
        self.tiles_per_lat = t_lat // tm

    def mod_row(self, i):
        return jnp.where(i < self.ctx_tiles, 0, 1 + (i - self.ctx_tiles) // self.tiles_per_lat)

    def mod_spec(self, idx):
        return pl.BlockSpec((1, 1, 1, self.d), lambda i: (idx, self.mod_row(i), 0, 0))

    def row_spec(self):
        return pl.BlockSpec((self.tm, self.d), lambda i: (i, 0))

    def vec_spec(self, idx):
        return pl.BlockSpec((None, 1, self.d), lambda i: (idx, 0, 0))


def modulate(tok, x, mod, idx_scale, idx_shift):
    est = 2 * tok.tm * tok.d * 6 + 4 * tok.tm * tok.d * 4
    return pl.pallas_call(
        _modulate_kernel,
        grid=(tok.n // tok.tm,),
        in_specs=[tok.row_spec(), tok.mod_spec(idx_scale), tok.mod_spec(idx_shift)],
        out_specs=tok.row_spec(),
        out_shape=jax.ShapeDtypeStruct((tok.n, tok.d), BF16),
        compiler_params=_cparams(("arbitrary",), est),
    )(x, mod, mod)


def residual_ln(tok, x, y, mod, idx_gate, ln_gb, idx_g, idx_b, alpha, idx_scale=None, idx_shift=None):
    est = 2 * tok.tm * tok.d * (4 + 4 + 4 + 2) + 6 * tok.tm * tok.d * 4
    in_specs = [tok.row_spec(), tok.row_spec(), tok.mod_spec(idx_gate), tok.vec_spec(idx_g), tok.vec_spec(idx_b)]
    args = [x, y, mod, ln_gb, ln_gb]
    if idx_scale is None:
        return pl.pallas_call(
            functools.partial(_ln_last_kernel, alpha=alpha),
            grid=(tok.n // tok.tm,),
            in_specs=in_specs,
            out_specs=tok.row_spec(),
            out_shape=jax.ShapeDtypeStruct((tok.n, tok.d), F32),
            compiler_params=_cparams(("arbitrary",), est),
        )(*args), None
    in_specs += [tok.mod_spec(idx_scale), tok.mod_spec(idx_shift)]
    args += [mod, mod]
    return pl.pallas_call(
        functools.partial(_ln_kernel, alpha=alpha),
        grid=(tok.n // tok.tm,),
        in_specs=in_specs,
        out_specs=[tok.row_spec(), tok.row_spec()],
        out_shape=[jax.ShapeDtypeStruct((tok.n, tok.d), F32), jax.ShapeDtypeStruct((tok.n, tok.d), BF16)],
        compiler_params=_cparams(("arbitrary",), est),
    )(*args)


ATTN_Q_ROWS = 512


def _softmax_pv(scores, vals):
    m = jnp.max(scores[0], axis=-1, keepdims=True)
    for s in scores[1:]:
        m = jnp.maximum(m, jnp.max(s, axis=-1, keepdims=True))
    acc = None
    den = None
    for s, v in zip(scores, vals):
        p = jnp.exp(s - m)
        ps = jnp.sum(p, axis=-1, keepdims=True)
        pv = jnp.dot(p.astype(BF16), v, preferred_element_type=F32)
        acc = pv if acc is None else acc + pv
        den = ps if den is None else den + ps
    return acc / den


def _rope_lanes(x, cos, sin_signed, n):
    lane = lax.broadcasted_iota(jnp.int32, x.shape, 1)
    first = ((lane // n) % 2) == 0
    partner = jnp.where(first, pltpu.roll(x, LANE - n, 1), pltpu.roll(x, n, 1))
    return x * cos + partner * sin_signed


def _rms_rows(x, g):
    return x * lax.rsqrt(jnp.mean(x * x, axis=-1, keepdims=True) + RMS_EPS) * g


def _attn_refs(refs, n_main, use_rope, has_cache, n_cache, aliased):
    it = iter(refs)
    main = [next(it) for _ in range(n_main)]
    rope = (next(it), next(it)) if use_rope else (None, None)
    cache = [next(it) for _ in range(n_cache)] if has_cache else [None] * n_cache
    if aliased:
        next(it)
    return main, rope, cache, next(it)


def _gqa_fused_kernel(*refs, hp, kv_group, t, scale, n_rot, use_rope, has_cache, aliased):
    (q_ref, k_ref, v_ref, gq_ref, gk_ref), (cos_ref, sin_ref), (ck_ref, cv_ref), o_ref = _attn_refs(
        refs, 5, use_rope, has_cache, 2, aliased)
    tq = min(ATTN_Q_ROWS, t)
    n_kv = max(1, hp // kv_group)
    keys, vals = [], []
    for j in range(n_kv):
        sl = slice(j * LANE, (j + 1) * LANE)
        k = _rms_rows(k_ref[:, sl], gk_ref[...])
        if use_rope:
            k = _rope_lanes(k, cos_ref[...], sin_ref[...], n_rot)
        kj, vj = [k.astype(BF16)], [v_ref[:, sl].astype(BF16)]
        if has_cache:
            kj.insert(0, ck_ref[:, sl].astype(BF16))
            vj.insert(0, cv_ref[:, sl].astype(BF16))
        keys.append(kj)
        vals.append(vj)
    for hh in range(hp):
        j = hh // kv_group if n_kv > 1 else 0
        for r0 in range(0, t, tq):
            q = _rms_rows(q_ref[r0:r0 + tq, hh * LANE:(hh + 1) * LANE], gq_ref[...])
            if use_rope:
                q = _rope_lanes(q, cos_ref[r0:r0 + tq, :], sin_ref[r0:r0 + tq, :], n_rot)
            qb = q.astype(BF16)
            o = _softmax_pv([_dot_nt(qb, k) * scale for k in keys[j]], vals[j])
            o_ref[r0:r0 + tq, hh * LANE:(hh + 1) * LANE] = o.astype(o_ref.dtype)


def _diff_fused_kernel(*refs, hp, t, scale, dd, n_rot, use_rope, has_cache, aliased):
    (q_ref, k_ref, v_ref, lam_ref, g_ref), (cos_ref, sin_ref), (ck_ref, cv_ref), o_ref = _attn_refs(
        refs, 5, use_rope, has_cache, 2, aliased)
    tq = min(ATTN_Q_ROWS, t)
    for hh in range(hp):
        sl = slice(hh * LANE, (hh + 1) * LANE)
        k = k_ref[:, sl]
        if use_rope:
            k = _rope_lanes(k, cos_ref[...], sin_ref[...], n_rot)
        keys, vals = [k.astype(BF16)], [v_ref[:, sl].astype(BF16)]
        if has_cache:
            keys.insert(0, ck_ref[:, sl].astype(BF16))
            vals.insert(0, cv_ref[:, sl].astype(BF16))
        for r0 in range(0, t, tq):
            q = q_ref[r0:r0 + tq, sl]
            if use_rope:
                q = _rope_lanes(q, cos_ref[r0:r0 + tq, :], sin_ref[r0:r0 + tq, :], n_rot)
            lane = lax.broadcasted_iota(jnp.int32, q.shape, 1)
            q1 = jnp.where(lane < dd, q, 0.0).astype(BF16)
            q2 = jnp.where(lane < dd, 0.0, q).astype(BF16)
            a1 = _softmax_pv([_dot_nt(q1, kk) * scale for kk in keys], vals)
            a2 = _softmax_pv([_dot_nt(q2, kk) * scale for kk in keys], vals)
            d = a1 - lam_ref[...] * a2
            o_ref[r0:r0 + tq, sl] = _rms_rows(d, g_ref[...]).astype(o_ref.dtype)


def _mla_fused_kernel(*refs, hp, t, scale, rope_d, n_rot, use_rope, has_cache, aliased):
    (q_ref, kv_ref, kr_ref), (cos_ref, sin_ref), (kvc_ref, krc_ref), o_ref = _attn_refs(
        refs, 3, use_rope, has_cache, 2, aliased)
    tq = min(ATTN_Q_ROWS, t)
    hw = 2 * LANE
    kr = kr_ref[...]
    if use_rope:
        kr = _rope_lanes(kr, cos_ref[...], sin_ref[...], n_rot)
    krs = [kr[:, :rope_d].astype(BF16)]
    if has_cache:
        krs.insert(0, krc_ref[...].astype(BF16))
    for hh in range(hp):
        c0 = hh * hw
        kns, vals = [kv_ref[:, c0:c0 + LANE]], [kv_ref[:, c0 + LANE:c0 + hw]]
        if has_cache:
            kns.insert(0, kvc_ref[:, c0:c0 + LANE])
            vals.insert(0, kvc_ref[:, c0 + LANE:c0 + hw])
        for r0 in range(0, t, tq):
            qn = q_ref[r0:r0 + tq, c0:c0 + LANE].astype(BF16)
            qr = q_ref[r0:r0 + tq, c0 + LANE:c0 + hw]
            if use_rope:
                qr = _rope_lanes(qr, cos_ref[r0:r0 + tq, :], sin_ref[r0:r0 + tq, :], n_rot)
            qr = qr[:, :rope_d].astype(BF16)
            scores = [(_dot_nt(qn, kn) + _dot_nt(qr, kk)) * scale for kn, kk in zip(kns, krs)]
            o_ref[r0:r0 + tq, hh * LANE:(hh + 1) * LANE] = _softmax_pv(scores, vals).astype(o_ref.dtype)


def _attn_launch(body, n_rows, heads, hp, row0, b, t, main, rope, cache, prev, width_est):
    rb0 = row0 // t
    in_specs = [s for _, s in main]
    args = [a for a, _ in main]
    if rope is not None:
        in_specs += [pl.BlockSpec((t, LANE), lambda bi, hb: (0, 0))] * 2
        args += list(rope)
    in_specs += [s for _, s in cache]
    args += [a for a, _ in cache]
    aliases = {}
    if prev is not None:
        aliases = {len(args): 0}
        in_specs.append(pl.BlockSpec(memory_space=pl.ANY))
        args.append(prev)
    n_cached = cache[1][0].shape[2] if cache else 0
    est = 2 * width_est + 8 * min(ATTN_Q_ROWS, t) * (t + n_cached) * 4 + (4 << 20)
    return pl.pallas_call(
        body,
        grid=(b, heads // hp),
        in_specs=in_specs,
        out_specs=pl.BlockSpec((t, hp * LANE), lambda bi, hb: (rb0 + bi, hb)),
        out_shape=jax.ShapeDtypeStruct((n_rows, heads * LANE), BF16),
        input_output_aliases=aliases,
        compiler_params=_cparams(("arbitrary", "arbitrary"), est),
    )(*args)


def _hgrn_kernel(*refs, t, nb, chunk, sub, q_scale, has_s0, want_state, aliased):
    it = iter(refs)
    dq_ref, di_ref, df_ref, dfb_ref, dg_ref, tab_ref = (next(it) for _ in range(6))
    s0_ref = next(it) if has_s0 else None
    if aliased:
        next(it)
    o_ref = next(it)
    st_out_ref = next(it) if want_state else None
    st_scr, o_scr = next(it), next(it)

    n_chunks = t // chunk
    n_sub = chunk // sub
    row = lax.broadcasted_iota(jnp.int32, (chunk, chunk), 0)
    col = lax.broadcasted_iota(jnp.int32, (chunk, chunk), 1)
    key_row = lax.broadcasted_iota(jnp.int32, (chunk, 1), 0)
    sub_row = lax.broadcasted_iota(jnp.int32, (sub, 1), 0)
    neg_inf = jnp.float32(-jnp.inf)
    tab = tab_ref[...]
    tri_f = jnp.where(row >= col, 1.0, 0.0).astype(F32)
    tri_b = jnp.where(row <= col, 1.0, 0.0).astype(F32)

    for bb in range(nb):
        for direction in range(2):
            if has_s0:
                st_scr[2 * bb + direction] = s0_ref[bb, direction, 0].T
            else:
                st_scr[2 * bb + direction] = jnp.zeros(st_scr.shape[1:], F32)

    def chain_step(ci, bb, direction):
        rev = direction == 1
        z_ref = dfb_ref if rev else df_ref
        tri = tri_b if rev else tri_f
        log_lb = tab[3 * direction:3 * direction + 1, :]
        log_1m = tab[3 * direction + 1:3 * direction + 2, :]
        one_m = tab[3 * direction + 2:3 * direction + 3, :]
        r0 = pl.multiple_of(bb * t + ((n_chunks - 1 - ci) if rev else ci) * chunk, chunk)
        zq = dq_ref[pl.ds(r0, chunk), :]
        q = zq * jax.nn.sigmoid(zq) * q_scale
        v = di_ref[pl.ds(r0, chunk), :]
        z = z_ref[pl.ds(r0, chunk), :]
        log_sig = jnp.minimum(z, 0.0) - jnp.log1p(jnp.exp(-jnp.abs(z)))
        t2 = log_1m + log_sig
        g = jnp.maximum(log_lb, t2) + jnp.log1p(jnp.exp(-jnp.abs(log_lb - t2)))
        k = one_m * jax.nn.sigmoid(-z)
        cum = jnp.dot(tri, g, precision=lax.Precision.HIGHEST, preferred_element_type=F32)
        edge = cum[0:1, :] if rev else cum[chunk - 1:chunk, :]
        st = st_scr[2 * bb + direction]
        vb = v.astype(BF16)
        inter = _dot_nt((q * jnp.exp(cum)).astype(BF16), st.astype(BF16))
        outs = []
        for i in range(n_sub):
            lo = i * sub
            qi = q[lo:lo + sub]
            cumi = cum[lo:lo + sub]
            acc = inter[lo:lo + sub]
            if (i < n_sub - 1) if rev else (i > 0):
                base = cum[lo + sub:lo + sub + 1, :] if rev else cum[lo - 1:lo, :]
                seen = (key_row >= lo + sub) if rev else (key_row < lo)
                qn = (qi * jnp.exp(cumi - base)).astype(BF16)
                kn = (k * jnp.exp(jnp.where(seen, base - cum, neg_inf))).astype(BF16)
                att = _dot_nt(qn, kn)
                acc = acc + jnp.dot(att.astype(BF16), vb, preferred_element_type=F32)
            for s in range(sub):
                r = lo + s
                live = (sub_row <= s) if rev else (sub_row >= s)
                w = jnp.exp(jnp.where(live, cumi - cum[r:r + 1, :], neg_inf))
                colv = jnp.sum(qi * k[r:r + 1, :] * w, axis=-1, keepdims=True)
                acc = acc + colv * v[r:r + 1, :]
            outs.append(acc)
        o_scr[direction, pl.ds(r0, chunk), :] = jnp.concatenate(outs, axis=0)
        kl = (k * jnp.exp(edge - cum)).astype(BF16)
        st_scr[2 * bb + direction] = st * jnp.exp(edge) + _dot_tn(vb, kl)

    def body(ci, carry):
        for bb in range(nb):
            for direction in range(2):
                chain_step(ci, bb, direction)
        return carry

    lax.fori_loop(0, n_chunks, body, 0)

    tot = o_scr[0] + o_scr[1]
    zg = dg_ref[...]
    y = tot * lax.rsqrt(jnp.mean(tot * tot, axis=-1, keepdims=True) + RMS_EPS) * tab[6:7, :]
    o_ref[...] = (y * (zg * jax.nn.sigmoid(zg))).astype(o_ref.dtype)
    if want_state:
        for bb in range(nb):
            for direction in range(2):
                st_out_ref[bb, direction, 0] = st_scr[2 * bb + direction].T


def hgrn_mix(h, row0, b, t, heads, dk, dv, cols, tab, s0, want_state, prev):
    assert dk == LANE and dv == LANE
    nb = HGRN_BATCHES if (b % HGRN_BATCHES == 0 and row0 % (HGRN_BATCHES * t) == 0) else 1
    assert row0 % (nb * t) == 0 and all(cc % LANE == 0 for cc in cols)
    chunk = min(HGRN_CHUNK, t)
    sub = min(HGRN_SUB, chunk)
    rb0 = row0 // (nb * t)

    def col_spec(c0):
        return pl.BlockSpec((nb * t, LANE), lambda bi, hh: (rb0 + bi, c0 // LANE + hh))

    st_spec = pl.BlockSpec((nb, 2, 1, dk, dv), lambda bi, hh: (bi, 0, hh, 0, 0))
    in_specs = [col_spec(cc) for cc in cols] + [pl.BlockSpec((8, LANE), lambda bi, hh: (0, hh))]
    args = [h] * len(cols) + [tab]
    if s0 is not None:
        in_specs.append(st_spec)
        args.append(s0)
    aliases = {}
    if prev is not None:
        aliases = {len(args): 0}
        in_specs.append(pl.BlockSpec(memory_space=pl.ANY))
        args.append(prev)
    out_specs = [pl.BlockSpec((nb * t, LANE), lambda bi, hh: (rb0 + bi, hh))]
    out_shape = [jax.ShapeDtypeStruct((h.shape[0], heads * dv), BF16)]
    if want_state:
        out_specs.append(st_spec)
        out_shape.append(jax.ShapeDtypeStruct((b, 2, heads, dk, dv), F32))
    est = 2 * nb * t * LANE * (5 * 4 + 2) + 2 * nb * t * LANE * 4 + (8 * nb + 6) * dk * dv * 4 + (4 << 20)
    res = pl.pallas_call(
        functools.partial(_hgrn_kernel, t=t, nb=nb, chunk=chunk, sub=sub, q_scale=dk ** -0.5,
                          has_s0=s0 is not None, want_state=want_state, aliased=prev is not None),
        grid=(b // nb, heads),
        in_specs=in_specs,
        out_specs=out_specs,
        out_shape=out_shape,
        input_output_aliases=aliases,
        scratch_shapes=[pltpu.VMEM((2 * nb, dv, dk), F32), pltpu.VMEM((2, nb * t, LANE), F32)],
        compiler_params=_cparams(("arbitrary", "arbitrary"), est),
    )(*args)
    return (res[0], res[1]) if want_state else (res[0], None)


def _rms(x, g):
    return x * lax.rsqrt(jnp.mean(x * x, axis=-1, keepdims=True) + RMS_EPS) * g


def _rope_lane_tables(t, d):
    n = d // 4
    inv = ROPE_THETA ** (-jnp.arange(n, dtype=F32) / n)
    pos = jnp.arange(t)
    rows = (pos // GRID_W).astype(F32)
    cols = (pos % GRID_W).astype(F32)
    ar = rows[:, None] * inv[None, :]
    ac = cols[:, None] * inv[None, :]
    cos = jnp.concatenate([jnp.cos(ar), jnp.cos(ar), jnp.cos(ac), jnp.cos(ac)], axis=-1)
    sin = jnp.concatenate([-jnp.sin(ar), jnp.sin(ar), -jnp.sin(ac), jnp.sin(ac)], axis=-1)
    reps = LANE // d
    return jnp.tile(cos, (1, reps)), jnp.tile(sin, (1, reps))


def kernel(x_prompt, x_sample, cache_mla_ckv, cache_mla_krope, cache_diff_k, cache_diff_v, cache_gqa_k, cache_gqa_v, state_hgrn, c, c_ctx, w_in, w_out, mla_g_cq, mla_g_ckv, mla_w_uq, mla_w_ukv, diff_lambda, diff_subln_g, gqa_g_q, gqa_g_k, hgrn_gamma, hgrn_g_norm, ada_w, ada_b, ln_g, ln_b, ffn_w1, ffn_w3, ffn_w2, moe_router, moe_w1, moe_w3, moe_w2):
    bc, tc, d = x_prompt.shape
    bl, tl, _ = x_sample.shape
    depth = w_in.shape[0]
    q_rank = mla_g_cq.shape[1]
    kv_rank = mla_g_ckv.shape[1]
    rope_d = cache_mla_krope.shape[-1]
    heads = cache_diff_k.shape[3]
    dd = cache_diff_k.shape[4] // 2
    kv_heads = cache_gqa_k.shape[3]
    gd = cache_gqa_k.shape[4]
    hk = state_hgrn.shape[4]
    hv = state_hgrn.shape[5]
    n_exp = moe_w1.shape[1]
    alpha = (2 * depth) ** 0.25
    n_ctx = bc * tc
    n_lat = bl * tl
    tok = _Tokens(n_ctx, tl, bl, d)
    mla_hw = 2 * LANE
    assert MLA_NOPE_DIM + rope_d <= mla_hw and MLA_NOPE_DIM + MLA_V_DIM == mla_hw

    unit = heads * LANE
    kvw = kv_heads * gd
    assert 2 * dd == LANE and gd == LANE and hk == LANE and hv == LANE
    names = ("cq", "ckv", "kr", "bq", "bk", "bv", "gq", "gk", "gv", "dq", "di", "df", "dfb", "dg")
    widths = (q_rank, kv_rank, rope_d, unit, unit, unit, unit, kvw, kvw, unit, unit, unit, unit, unit)
    assert sum(widths) == w_in.shape[2]
    src, o = {}, 0
    for nm, w in zip(names, widths):
        src[nm] = (o, w)
        o += w
    col, pieces, o = {}, [], 0

    def place(nm, pad_to=None):
        nonlocal o
        if nm is not None:
            s0, w = src[nm]
            col[nm] = o
            pieces.append(w_in[:, :, s0:s0 + w])
            o += w
        if pad_to is not None and o % pad_to:
            z = pad_to - o % pad_to
            pieces.append(jnp.zeros((depth, d, z), w_in.dtype))
            o += z

    place("cq")
    place("gk", pad_to=unit)
    place("ckv")
    place("gv")
    place("kr", pad_to=LANE)
    place(None, pad_to=unit)
    for nm in ("bq", "bk", "bv", "gq", "dq", "di", "df", "dfb", "dg"):
        place(nm)
    w_in_b = jnp.concatenate(pieces, axis=-1).astype(BF16)
    assert q_rank <= unit and col["gk"] % kvw == 0 and col["gv"] % kvw == 0 and col["ckv"] % kv_rank == 0
    w_uq_b = jnp.pad(mla_w_uq.reshape(depth, q_rank, heads, MLA_NOPE_DIM + rope_d),
                     ((0, 0), (0, 0), (0, 0), (0, mla_hw - MLA_NOPE_DIM - rope_d))
                     ).reshape(depth, q_rank, heads * mla_hw).astype(BF16)
    w_ukv_b = mla_w_ukv.astype(BF16)
    ffn_w1_b, ffn_w3_b, ffn_w2_b = ffn_w1.astype(BF16), ffn_w3.astype(BF16), ffn_w2.astype(BF16)
    e_ff = moe_w1.shape[-1]
    n_moe = moe_w1.shape[0]
    moe_w1_b = moe_w1.astype(BF16).reshape(n_moe * n_exp, d, e_ff)
    moe_w3_b = moe_w3.astype(BF16).reshape(n_moe * n_exp, d, e_ff)
    moe_w2_b = moe_w2.astype(BF16).reshape(n_moe * n_exp, e_ff, d)
    router_b = jnp.pad(moe_router, ((0, 0), (0, 0), (0, LANE - n_exp))).astype(BF16)

    cond = jnp.concatenate([c_ctx[None, :], c], axis=0)
    n_cond = cond.shape[0]
    cond_p = jnp.pad(jax.nn.silu(cond), ((0, -n_cond % 16), (0, 0))).astype(BF16)
    mods = []
    for l in range(depth):
        ml = matmul_few_rows(cond_p, ada_w, l)[:n_cond] + ada_b[l][None, :]
        mods.append(ml.reshape(n_cond, 6, d).transpose(1, 0, 2))
    mod = jnp.concatenate(mods, axis=0).reshape(depth * 6, n_cond, 1, d)
    ln_gb = jnp.concatenate([ln_g.reshape(depth * 2, 1, d), ln_b.reshape(depth * 2, 1, d)], axis=0)

    def lower_bounds(gamma):
        cs = jnp.cumsum(jax.nn.softmax(gamma.astype(F32), axis=0), axis=0)
        return cs - cs[0:1]

    lb_f = lower_bounds(hgrn_gamma[0])
    lb_b = lower_bounds(hgrn_gamma[1])
    n_rot_small, n_rot_big = dd // 4, gd // 4
    assert rope_d == dd
    rope_small = _rope_lane_tables(tl, dd)
    rope_big = _rope_lane_tables(tl, gd)
    n_tok = n_ctx + n_lat
    past = cache_mla_ckv.shape[2]
    cache_dk = cache_diff_k.reshape(bl, depth, past, unit)
    cache_dv = cache_diff_v.reshape(bl, depth, past, unit)
    cache_gk = cache_gqa_k.reshape(bl, depth, past, kvw)
    cache_gv = cache_gqa_v.reshape(bl, depth, past, kvw)
    hgrn_cols = [col[nm] for nm in ("dq", "di", "df", "dfb", "dg")]

    def mixers(h, q_all, kv_new, kv_cache, l, lat, prevs):
        row0, b, t = (n_ctx, bl, tl) if lat else (0, bc, tc)
        hp = 1 if lat else heads
        rb0 = row0 // t
        flags = dict(use_rope=lat, has_cache=lat, aliased=prevs is not None)
        pa, pb, pc, pd = prevs if prevs is not None else (None,) * 4

        def hcols(c0, width):
            return pl.BlockSpec((t, width), lambda bi, hb: (rb0 + bi, c0 // width + hb))

        def vec(n):
            return pl.BlockSpec((1, n), lambda bi, hb: (0, 0))

        def cache_spec(width, per_head):
            if per_head:
                return pl.BlockSpec((None, None, past, width), lambda bi, hb: (bi, l, 0, hb))
            return pl.BlockSpec((None, None, past, width), lambda bi, hb: (bi, l, 0, 0))

        wide = pl.BlockSpec((t, hp * mla_hw), lambda bi, hb: (rb0 + bi, hb))
        main = [(q_all, wide), (kv_new, wide), (h, pl.BlockSpec((t, LANE), lambda bi, hb: (rb0 + bi, col["kr"] // LANE)))]
        cache = []
        if lat:
            cache = [(kv_cache, pl.BlockSpec((past, hp * mla_hw), lambda bi, hb: (bi, hb))),
                     (cache_mla_krope, cache_spec(rope_d, False))]
        o_a = _attn_launch(
            functools.partial(_mla_fused_kernel, hp=hp, t=t, scale=(MLA_NOPE_DIM + rope_d) ** -0.5, rope_d=rope_d,
                              n_rot=n_rot_small, **flags),
            n_tok, heads, hp, row0, b, t, main, rope_small if lat else None, cache, pa,
            t * hp * mla_hw * 6 + past * hp * mla_hw * 2)

        lam_p = diff_lambda[l].astype(F32)
        lam_init = 0.8 - 0.6 * math.exp(-0.3 * l)
        lam = jnp.exp(jnp.sum(lam_p[0] * lam_p[1])) - jnp.exp(jnp.sum(lam_p[2] * lam_p[3])) + lam_init
        lam_v = jnp.full((1, LANE), lam, F32)
        g_v = (diff_subln_g[l] * (1.0 - lam_init)).reshape(1, LANE)
        w = hp * LANE
        main = [(h, hcols(col["bq"], w)), (h, hcols(col["bk"], w)), (h, hcols(col["bv"], w)),
                (lam_v, vec(LANE)), (g_v, vec(LANE))]
        cache = [(cache_dk, cache_spec(w, True)), (cache_dv, cache_spec(w, True))] if lat else []
        o_b = _attn_launch(
            functools.partial(_diff_fused_kernel, hp=hp, t=t, scale=dd ** -0.5, dd=dd, n_rot=n_rot_small, **flags),
            n_tok, heads, hp, row0, b, t, main, rope_small if lat else None, cache, pb,
            (3 * t + 2 * past) * w * 4)

        kv_group = heads // kv_heads
        if lat:
            kspec = pl.BlockSpec((t, LANE), lambda bi, hb: (rb0 + bi, col["gk"] // LANE + hb // kv_group))
            vspec = pl.BlockSpec((t, LANE), lambda bi, hb: (rb0 + bi, col["gv"] // LANE + hb // kv_group))
            cspec = pl.BlockSpec((None, None, past, LANE), lambda bi, hb: (bi, l, 0, hb // kv_group))
            cache = [(cache_gk, cspec), (cache_gv, cspec)]
        else:
            kspec = pl.BlockSpec((t, kvw), lambda bi, hb: (rb0 + bi, col["gk"] // kvw))
            vspec = pl.BlockSpec((t, kvw), lambda bi, hb: (rb0 + bi, col["gv"] // kvw))
            cache = []
        main = [(h, hcols(col["gq"], w)), (h, kspec), (h, vspec),
                (gqa_g_q[l].reshape(1, gd), vec(gd)), (gqa_g_k[l].reshape(1, gd), vec(gd))]
        o_c = _attn_launch(
            functools.partial(_gqa_fused_kernel, hp=hp, kv_group=kv_group, t=t, scale=gd ** -0.5, n_rot=n_rot_big,
                              **flags),
            n_tok, heads, hp, row0, b, t, main, rope_big if lat else None, cache, pc,
            (t + 2 * t + 2 * past) * w * 4)

        tab = jnp.stack([jnp.log(lb_f[l]), jnp.log1p(-lb_f[l]), 1.0 - lb_f[l],
                         jnp.log(lb_b[l]), jnp.log1p(-lb_b[l]), 1.0 - lb_b[l],
                         jnp.tile(hgrn_g_norm[l], heads), jnp.zeros((heads * hk,), F32)], axis=0)
        o_d, s_new = hgrn_mix(h, row0, b, t, heads, hk, hv, hgrn_cols, tab, state_hgrn[:, l] if lat else None,
                              not lat, pd)
        return (o_a, o_b, o_c, o_d), s_new

    x = jnp.concatenate([x_prompt.reshape(n_ctx, d), x_sample.reshape(n_lat, d)], axis=0)
    u = modulate(tok, x, mod, 1, 0)
    ctx_out = []
    for l in range(depth):
        h = matmul(u, w_in_b, F32, layer=l)
        q_all = rms_matmul(h, col["cq"], unit, mla_g_cq[l].reshape(1, q_rank), w_uq_b, l, F32)
        kv_new = rms_matmul(h, col["ckv"], kv_rank, mla_g_ckv[l].reshape(1, kv_rank), w_ukv_b, l, BF16)
        kv_cache = matmul(cache_mla_ckv[:, l].reshape(bl * past, kv_rank).astype(BF16), w_ukv_b, BF16, layer=l)
        parts, s_new = mixers(h, q_all, kv_new, kv_cache, l, False, None)
        parts, _ = mixers(h, q_all, kv_new, kv_cache, l, True, parts)
        hc = h[:n_ctx]

        def piece(nm, width):
            return hc[:, col[nm]:col[nm] + width]

        ctx_out.append((
            _rms(piece("ckv", kv_rank), mla_g_ckv[l]).reshape(bc, tc, kv_rank),
            piece("kr", rope_d).reshape(bc, tc, rope_d),
            piece("bk", unit).reshape(bc, tc, heads, 2 * dd),
            piece("bv", unit).reshape(bc, tc, heads, 2 * dd),
            _rms(piece("gk", kvw).reshape(bc, tc, kv_heads, gd), gqa_g_k[l]),
            piece("gv", kvw).reshape(bc, tc, kv_heads, gd),
            s_new))
        mix = out_proj(parts, w_out, l)
        x, u = residual_ln(tok, x, mix, mod, l * 6 + 2, ln_gb, 2 * l, 2 * depth + 2 * l, alpha,
                           l * 6 + 4, l * 6 + 3)
        j = l // 2
        if l % 2 == 0:
            f = down_proj(swiglu_up(u, ffn_w1_b, ffn_w3_b, j), ffn_w2_b[j])
        else:
            logits = matmul(u, router_b, F32, layer=j)[:, :n_exp]
            f = moe_experts(u, logits, moe_w1_b, moe_w3_b, moe_w2_b, j, n_exp)
        if l + 1 < depth:
            x, u = residual_ln(tok, x, f, mod, l * 6 + 5, ln_gb, 2 * l + 1, 2 * depth + 2 * l + 1, alpha,
                               (l + 1) * 6 + 1, (l + 1) * 6 + 0)
        else:
            x, _ = residual_ln(tok, x, f, mod, l * 6 + 5, ln_gb, 2 * l + 1, 2 * depth + 2 * l + 1, alpha)

    y_prompt = x[:n_ctx].reshape(bc, tc, d)
    y_sample = x[n_ctx:].reshape(bl, tl, d)
    stack = lambda i: jnp.stack([e[i] for e in ctx_out], axis=1)
    return (y_prompt, y_sample, stack(0), stack(1), stack(2), stack(3), stack(4), stack(5), stack(6))
```

```python
import functools
import math

import jax
import jax.numpy as jnp
from jax import lax
from jax.experimental import pallas as pl
from jax.experimental.pallas import tpu as pltpu

F32 = jnp.float32
BF16 = jnp.bfloat16

GRID_W = 64
ROPE_THETA = 10000.0
LN_EPS = 1e-5
RMS_EPS = 1e-6
MLA_NOPE_DIM = 128
MLA_V_DIM = 128
TOP_K = 2

LANE = 128
V7X_VMEM_BYTES = 64 * 1024 * 1024
VMEM_CAP = V7X_VMEM_BYTES - 6 * 1024 * 1024
HGRN_CHUNK = 64
HGRN_SUB = 16
HGRN_BATCHES = 4
MOE_ROWS = 512


def _cparams(sem, est_bytes):
    limit = int(min(VMEM_CAP, max(32 * 1024 * 1024, est_bytes * 5 // 4 + (4 << 20))))
    return pltpu.CompilerParams(dimension_semantics=sem, vmem_limit_bytes=limit)


def _pick_tile(n, target):
    if n <= target:
        return n
    t = (target // LANE) * LANE
    while t >= LANE:
        if n % t == 0:
            return t
        t -= LANE
    return n


def _pick_rows(n, target):
    if n <= target:
        return n
    t = target
    while t >= 8:
        if n % t == 0:
            return t
        t //= 2
    return n


def _dot_nt(a, b):
    return lax.dot_general(a, b, (((1,), (1,)), ((), ())), preferred_element_type=F32)


def _dot_tn(a, b):
    return lax.dot_general(a, b, (((0,), (0,)), ((), ())), preferred_element_type=F32)


def _mm_kernel(x_ref, w_ref, o_ref):
    o_ref[...] = jnp.dot(x_ref[...], w_ref[...], preferred_element_type=F32).astype(o_ref.dtype)


def matmul(x, w, out_dtype, layer=None, tm=1024, tn=1024):
    m, k = x.shape
    n = w.shape[-1]
    tm = _pick_rows(m, tm)
    tn = _pick_tile(n, tn)
    if w.ndim == 3:
        w_spec = pl.BlockSpec((None, k, tn), lambda j, i: (layer, 0, j))
    else:
        w_spec = pl.BlockSpec((k, tn), lambda j, i: (0, j))
    est = 2 * (tm * k * 2 + k * tn * 2 + tm * tn * jnp.dtype(out_dtype).itemsize) + tm * tn * 4
    return pl.pallas_call(
        _mm_kernel,
        grid=(n // tn, m // tm),
        in_specs=[pl.BlockSpec((tm, k), lambda j, i: (i, 0)), w_spec],
        out_specs=pl.BlockSpec((tm, tn), lambda j, i: (i, j)),
        out_shape=jax.ShapeDtypeStruct((m, n), out_dtype),
        compiler_params=_cparams(("arbitrary", "arbitrary"), est),
    )(x, w)


def _mm_castw_kernel(x_ref, w_ref, o_ref):
    o_ref[...] = jnp.dot(x_ref[...], w_ref[...].astype(BF16), preferred_element_type=F32).astype(o_ref.dtype)


def matmul_few_rows(x, w, layer, tn=1024):
    m, k = x.shape
    n = w.shape[-1]
    tn = _pick_tile(n, tn)
    est = 2 * (m * k * 2 + k * tn * 4 + m * tn * 4) + k * tn * 2
    return pl.pallas_call(
        _mm_castw_kernel,
        grid=(n // tn,),
        in_specs=[pl.BlockSpec((m, k), lambda j: (0, 0)), pl.BlockSpec((None, k, tn), lambda j: (layer, 0, j))],
        out_specs=pl.BlockSpec((m, tn), lambda j: (0, j)),
        out_shape=jax.ShapeDtypeStruct((m, n), F32),
        compiler_params=_cparams(("arbitrary",), est),
    )(x, w)


def _rms_mm_kernel(x_ref, g_ref, w_ref, o_ref):
    x = x_ref[:, :w_ref.shape[0]]
    xn = x * lax.rsqrt(jnp.mean(x * x, axis=-1, keepdims=True) + RMS_EPS) * g_ref[...]
    o_ref[...] = jnp.dot(xn.astype(BF16), w_ref[...], preferred_element_type=F32).astype(o_ref.dtype)


def rms_matmul(h, col0, kb, gain, w, layer, out_dtype, tm=1024, tn=1024):
    m = h.shape[0]
    k, n = w.shape[1], w.shape[2]
    assert col0 % kb == 0 and kb >= k
    tm = _pick_rows(m, tm)
    tn = _pick_tile(n, tn)
    est = 2 * (tm * kb * 4 + k * tn * 2 + tm * tn * 4) + 3 * tm * k * 4 + tm * tn * 4
    return pl.pallas_call(
        _rms_mm_kernel,
        grid=(n // tn, m // tm),
        in_specs=[pl.BlockSpec((tm, kb), lambda j, i: (i, col0 // kb)),
                  pl.BlockSpec((1, k), lambda j, i: (0, 0)),
                  pl.BlockSpec((None, k, tn), lambda j, i: (layer, 0, j))],
        out_specs=pl.BlockSpec((tm, tn), lambda j, i: (i, j)),
        out_shape=jax.ShapeDtypeStruct((m, n), out_dtype),
        compiler_params=_cparams(("arbitrary", "arbitrary"), est),
    )(h, gain, w)


def _out_proj_kernel(a_ref, b_ref, c_ref, d_ref, w_ref, o_ref, wb_scr):
    @pl.when(pl.program_id(1) == 0)
    def _():
        wb_scr[...] = w_ref[...].astype(BF16)

    kq = a_ref.shape[1]
    acc = jnp.dot(a_ref[...], wb_scr[0:kq, :], preferred_element_type=F32)
    for g, r in enumerate((b_ref, c_ref, d_ref), start=1):
        acc = acc + jnp.dot(r[...], wb_scr[g * kq:(g + 1) * kq, :], preferred_element_type=F32)
    o_ref[...] = acc


def out_proj(parts, w, layer, tm=1024, tn=512):
    m, kq = parts[0].shape
    k, n = w.shape[1], w.shape[2]
    assert len(parts) == 4 and 4 * kq == k
    tm = _pick_rows(m, tm)
    tn = _pick_tile(n, tn)
    est = 2 * (4 * tm * kq * 2 + k * tn * 4 + tm * tn * 4) + k * tn * 2 + 2 * tm * tn * 4
    part_spec = pl.BlockSpec((tm, kq), lambda j, i: (i, 0))
    return pl.pallas_call(
        _out_proj_kernel,
        grid=(n // tn, m // tm),
        in_specs=[part_spec] * 4 + [pl.BlockSpec((None, k, tn), lambda j, i: (layer, 0, j))],
        out_specs=pl.BlockSpec((tm, tn), lambda j, i: (i, j)),
        out_shape=jax.ShapeDtypeStruct((m, n), F32),
        scratch_shapes=[pltpu.VMEM((k, tn), BF16)],
        compiler_params=_cparams(("arbitrary", "arbitrary"), est),
    )(*parts, w)


def _swiglu_kernel(x_ref, w1_ref, w3_ref, o_ref, w1b_scr, w3b_scr):
    @pl.when(pl.program_id(1) == 0)
    def _():
        w1b_scr[...] = w1_ref[...].astype(BF16)
        w3b_scr[...] = w3_ref[...].astype(BF16)

    x = x_ref[...]
    a = jnp.dot(x, w1b_scr[...], preferred_element_type=F32)
    b = jnp.dot(x, w3b_scr[...], preferred_element_type=F32)
    o_ref[...] = (a * jax.nn.sigmoid(a) * b).astype(o_ref.dtype)


def swiglu_up(x, w1, w3, layer, tm=1024, tn=256):
    m, k = x.shape
    n = w1.shape[-1]
    tm = _pick_rows(m, tm)
    tn = _pick_tile(n, tn)
    w_spec = pl.BlockSpec((None, k, tn), lambda j, i: (layer, 0, j))
    est = 2 * (tm * k * 2 + 2 * k * tn * 4 + tm * tn * 2) + 2 * k * tn * 2 + 3 * tm * tn * 4
    return pl.pallas_call(
        _swiglu_kernel,
        grid=(n // tn, m // tm),
        in_specs=[pl.BlockSpec((tm, k), lambda j, i: (i, 0)), w_spec, w_spec],
        out_specs=pl.BlockSpec((tm, tn), lambda j, i: (i, j)),
        out_shape=jax.ShapeDtypeStruct((m, n), BF16),
        scratch_shapes=[pltpu.VMEM((k, tn), BF16), pltpu.VMEM((k, tn), BF16)],
        compiler_params=_cparams(("arbitrary", "arbitrary"), est),
    )(x, w1, w3)


def _down_kernel(x_ref, w_ref, o_ref):
    p = jnp.dot(x_ref[...], w_ref[...], preferred_element_type=F32)

    @pl.when(pl.program_id(2) == 0)
    def _():
        o_ref[...] = p

    @pl.when(pl.program_id(2) > 0)
    def _():
        o_ref[...] += p


def down_proj(x, w, tm=2048, tn=1024, tk=1792):
    m, k = x.shape
    n = w.shape[-1]
    tm = _pick_rows(m, tm)
    tn = _pick_tile(n, tn)
    tk = _pick_tile(k, tk)
    est = 2 * (tm * tk * 2 + tk * tn * 2 + tm * tn * 4) + tm * tn * 4
    return pl.pallas_call(
        _down_kernel,
        grid=(m // tm, n // tn, k // tk),
        in_specs=[pl.BlockSpec((tm, tk), lambda i, j, s: (i, s)), pl.BlockSpec((tk, tn), lambda i, j, s: (s, j))],
        out_specs=pl.BlockSpec((tm, tn), lambda i, j, s: (i, j)),
        out_shape=jax.ShapeDtypeStruct((m, n), F32),
        compiler_params=_cparams(("arbitrary", "arbitrary", "arbitrary"), est),
    )(x, w)


def _moe_up_kernel(rt_ref, nt_ref, ex_ref, ok_ref, new_ref, x_ref, w1_ref, w3_ref, o_ref, w1b_scr, w3b_scr):
    @pl.when(new_ref[pl.program_id(0)] == 1)
    def _():
        w1b_scr[...] = w1_ref[...].astype(BF16)
        w3b_scr[...] = w3_ref[...].astype(BF16)

    @pl.when(ok_ref[pl.program_id(0)] == 1)
    def _():
        x = x_ref[...]
        a = jnp.dot(x, w1b_scr[...], preferred_element_type=F32)
        b = jnp.dot(x, w3b_scr[...], preferred_element_type=F32)
        o_ref[...] = (a * jax.nn.sigmoid(a) * b).astype(o_ref.dtype)


def _moe_down_kernel(rt_ref, nt_ref, ex_ref, ok_ref, x_ref, w_ref, g_ref, o_ref):
    @pl.when(ok_ref[pl.program_id(0)] == 1)
    def _():
        o_ref[...] = jnp.dot(x_ref[...], w_ref[...], preferred_element_type=F32) * g_ref[...]


def _moe_items(tiles_e, tile0_e, nj, n_tiles):
    n_items = n_tiles * nj
    per_e = tiles_e * nj
    end_e = jnp.cumsum(per_e)
    total = end_e[-1]
    pos = jnp.arange(n_items, dtype=jnp.int32)
    idx = jnp.minimum(pos, total - 1)
    ex = jnp.searchsorted(end_e, idx, side="right").astype(jnp.int32)
    local = idx - (end_e - per_e)[ex]
    te = jnp.maximum(tiles_e[ex], 1)
    rt = tile0_e[ex] + local % te
    nt = local // te
    ok = pos < total
    new = ok & (local % te == 0)
    return rt.astype(jnp.int32), nt.astype(jnp.int32), ex, ok.astype(jnp.int32), new.astype(jnp.int32)


def moe_experts(u, logits, w1, w3, w2, group, n_exp):
    n, d = u.shape
    f = w1.shape[-1]
    tm = min(MOE_ROWS, n)
    top_v, top_i = lax.top_k(logits, TOP_K)
    wts = jax.nn.softmax(top_v, axis=-1)
    n_pairs = n * TOP_K
    n_rows = -(-n_pairs // tm) * tm + n_exp * tm
    n_tiles = n_rows // tm

    e_flat = top_i.reshape(-1).astype(jnp.int32)
    order = jnp.argsort(e_flat, stable=True)
    cnt = jnp.sum(jax.nn.one_hot(e_flat, n_exp, dtype=jnp.int32), axis=0)
    tiles_e = (cnt + tm - 1) // tm
    tile0_e = jnp.cumsum(tiles_e) - tiles_e
    e_sorted = e_flat[order]
    rank = jnp.arange(n_pairs, dtype=jnp.int32) - (jnp.cumsum(cnt) - cnt)[e_sorted]
    dest_sorted = tile0_e[e_sorted] * tm + rank
    dest = jnp.zeros((n_pairs,), jnp.int32).at[order].set(dest_sorted)
    row_token = jnp.zeros((n_rows,), jnp.int32).at[dest].set(jnp.arange(n_pairs, dtype=jnp.int32) // TOP_K)
    row_gate = jnp.zeros((n_rows,), F32).at[dest].set(wts.reshape(-1))

    xg = jnp.take(u, row_token, axis=0)

    tn = _pick_tile(f, 256)
    items = _moe_items(tiles_e, tile0_e, f // tn, n_tiles)
    w_spec = pl.BlockSpec((None, d, tn), lambda i, rt, nt, ex, ok, new: (group * n_exp + ex[i], 0, nt[i]))
    est = 2 * (tm * d * 2 + 2 * d * tn * 4 + tm * tn * 2) + 2 * d * tn * 2 + 3 * tm * tn * 4
    act = pl.pallas_call(
        _moe_up_kernel,
        grid_spec=pltpu.PrefetchScalarGridSpec(
            num_scalar_prefetch=5, grid=(n_tiles * (f // tn),),
            in_specs=[pl.BlockSpec((tm, d), lambda i, rt, nt, ex, ok, new: (rt[i], 0)), w_spec, w_spec],
            out_specs=pl.BlockSpec((tm, tn), lambda i, rt, nt, ex, ok, new: (rt[i], nt[i])),
            scratch_shapes=[pltpu.VMEM((d, tn), BF16), pltpu.VMEM((d, tn), BF16)]),
        out_shape=jax.ShapeDtypeStruct((n_rows, f), BF16),
        compiler_params=_cparams(("arbitrary",), est),
    )(*items, xg, w1, w3)

    tn2 = _pick_tile(d, 512)
    items2 = _moe_items(tiles_e, tile0_e, d // tn2, n_tiles)[:4]
    est = 2 * (tm * f * 2 + f * tn2 * 2 + tm * tn2 * 4 + tm * LANE * 4) + tm * tn2 * 4
    y = pl.pallas_call(
        _moe_down_kernel,
        grid_spec=pltpu.PrefetchScalarGridSpec(
            num_scalar_prefetch=4, grid=(n_tiles * (d // tn2),),
            in_specs=[pl.BlockSpec((tm, f), lambda i, rt, nt, ex, ok: (rt[i], 0)),
                      pl.BlockSpec((None, f, tn2), lambda i, rt, nt, ex, ok: (group * n_exp + ex[i], 0, nt[i])),
                      pl.BlockSpec((tm, 1), lambda i, rt, nt, ex, ok: (rt[i], 0))],
            out_specs=pl.BlockSpec((tm, tn2), lambda i, rt, nt, ex, ok: (rt[i], nt[i]))),
        out_shape=jax.ShapeDtypeStruct((n_rows, d), F32),
        compiler_params=_cparams(("arbitrary",), est),
    )(*items2, act, w2, row_gate[:, None])

    dest = dest.reshape(n, TOP_K)
    return jnp.take(y, dest[:, 0], axis=0) + jnp.take(y, dest[:, 1], axis=0)


def _modulate_kernel(x_ref, scale_ref, shift_ref, u_ref):
    u_ref[...] = (x_ref[...] * (1.0 + scale_ref[0, 0]) + shift_ref[0, 0]).astype(u_ref.dtype)


def _ln_kernel(x_ref, y_ref, gate_ref, g_ref, b_ref, scale_ref, shift_ref, xo_ref, uo_ref, *, alpha):
    z = alpha * x_ref[...] + gate_ref[0, 0] * y_ref[...]
    zc = z - jnp.mean(z, axis=-1, keepdims=True)
    yn = zc * lax.rsqrt(jnp.mean(zc * zc, axis=-1, keepdims=True) + LN_EPS)
    xn = yn * g_ref[...] + b_ref[...]
    xo_ref[...] = xn
    uo_ref[...] = (xn * (1.0 + scale_ref[0, 0]) + shift_ref[0, 0]).astype(uo_ref.dtype)


def _ln_last_kernel(x_ref, y_ref, gate_ref, g_ref, b_ref, xo_ref, *, alpha):
    z = alpha * x_ref[...] + gate_ref[0, 0] * y_ref[...]
    zc = z - jnp.mean(z, axis=-1, keepdims=True)
    yn = zc * lax.rsqrt(jnp.mean(zc * zc, axis=-1, keepdims=True) + LN_EPS)
    xo_ref[...] = yn * g_ref[...] + b_ref[...]


class _Tokens:
    def __init__(self, n_ctx, t_lat, n_lat_batches, d):
        self.n_ctx, self.t_lat, self.d = n_ctx, t_lat, d
        self.n = n_ctx + t_lat * n_lat_batches
        tm = 256
        while n_ctx % tm or t_lat % tm:
            tm //= 2
        self.tm = tm
        self.ctx_tiles = n_ctx // tm
        self.tiles_per_lat = t_lat // tm

    def mod_row(self, i):
        return jnp.where(i < self.ctx_tiles, 0, 1 + (i - self.ctx_tiles) // self.tiles_per_lat)

    def mod_spec(self, idx):
        return pl.BlockSpec((1, 1, 1, self.d), lambda i: (idx, self.mod_row(i), 0, 0))

    def row_spec(self):
        return pl.BlockSpec((self.tm, self.d), lambda i: (i, 0))

    def vec_spec(self, idx):
        return pl.BlockSpec((None, 1, self.d), lambda i: (idx, 0, 0))


def modulate(tok, x, mod, idx_scale, idx_shift):
    est = 2 * tok.tm * tok.d * 6 + 4 * tok.tm * tok.d * 4
    return pl.pallas_call(
        _modulate_kernel,
        grid=(tok.n // tok.tm,),
        in_specs=[tok.row_spec(), tok.mod_spec(idx_scale), tok.mod_spec(idx_shift)],
        out_specs=tok.row_spec(),
        out_shape=jax.ShapeDtypeStruct((tok.n, tok.d), BF16),
        compiler_params=_cparams(("arbitrary",), est),
    )(x, mod, mod)


def residual_ln(tok, x, y, mod, idx_gate, ln_gb, idx_g, idx_b, alpha, idx_scale=None, idx_shift=None):
    est = 2 * tok.tm * tok.d * (4 + 4 + 4 + 2) + 6 * tok.tm * tok.d * 4
    in_specs = [tok.row_spec(), tok.row_spec(), tok.mod_spec(idx_gate), tok.vec_spec(idx_g), tok.vec_spec(idx_b)]
    args = [x, y, mod, ln_gb, ln_gb]
    if idx_scale is None:
        return pl.pallas_call(
            functools.partial(_ln_last_kernel, alpha=alpha),
            grid=(tok.n // tok.tm,),
            in_specs=in_specs,
            out_specs=tok.row_spec(),
            out_shape=jax.ShapeDtypeStruct((tok.n, tok.d), F32),
            compiler_params=_cparams(("arbitrary",), est),
        )(*args), None
    in_specs += [tok.mod_spec(idx_scale), tok.mod_spec(idx_shift)]
    args += [mod, mod]
    return pl.pallas_call(
        functools.partial(_ln_kernel, alpha=alpha),
        grid=(tok.n // tok.tm,),
        in_specs=in_specs,
        out_specs=[tok.row_spec(), tok.row_spec()],
        out_shape=[jax.ShapeDtypeStruct((tok.n, tok.d), F32), jax.ShapeDtypeStruct((tok.n, tok.d), BF16)],
        compiler_params=_cparams(("arbitrary",), est),
    )(*args)


ATTN_Q_ROWS = 512


def _softmax_pv(scores, vals):
    m = jnp.max(scores[0], axis=-1, keepdims=True)
    for s in scores[1:]:
        m = jnp.maximum(m, jnp.max(s, axis=-1, keepdims=True))
    acc = None
    den = None
    for s, v in zip(scores, vals):
        p = jnp.exp(s - m)
        ps = jnp.sum(p, axis=-1, keepdims=True)
        pv = jnp.dot(p.astype(BF16), v, preferred_element_type=F32)
        acc = pv if acc is None else acc + pv
        den = ps if den is None else den + ps
    return acc / den


def _rope_lanes(x, cos, sin_signed, n):
    lane = lax.broadcasted_iota(jnp.int32, x.shape, 1)
    first = ((lane // n) % 2) == 0
    partner = jnp.where(first, pltpu.roll(x, LANE - n, 1), pltpu.roll(x, n, 1))
    return x * cos + partner * sin_signed


def _rms_rows(x, g):
    return x * lax.rsqrt(jnp.mean(x * x, axis=-1, keepdims=True) + RMS_EPS) * g


def _attn_refs(refs, n_main, use_rope, has_cache, n_cache, aliased):
    it = iter(refs)
    main = [next(it) for _ in range(n_main)]
    rope = (next(it), next(it)) if use_rope else (None, None)
    cache = [next(it) for _ in range(n_cache)] if has_cache else [None] * n_cache
    if aliased:
        next(it)
    return main, rope, cache, next(it)


def _gqa_fused_kernel(*refs, hp, kv_group, t, scale, n_rot, use_rope, has_cache, aliased):
    (q_ref, k_ref, v_ref, gq_ref, gk_ref), (cos_ref, sin_ref), (ck_ref, cv_ref), o_ref = _attn_refs(
        refs, 5, use_rope, has_cache, 2, aliased)
    tq = min(ATTN_Q_ROWS, t)
    n_kv = max(1, hp // kv_group)
    keys, vals = [], []
    for j in range(n_kv):
        sl = slice(j * LANE, (j + 1) * LANE)
        k = _rms_rows(k_ref[:, sl], gk_ref[...])
        if use_rope:
            k = _rope_lanes(k, cos_ref[...], sin_ref[...], n_rot)
        kj, vj = [k.astype(BF16)], [v_ref[:, sl].astype(BF16)]
        if has_cache:
            kj.insert(0, ck_ref[:, sl].astype(BF16))
            vj.insert(0, cv_ref[:, sl].astype(BF16))
        keys.append(kj)
        vals.append(vj)
    for hh in range(hp):
        j = hh // kv_group if n_kv > 1 else 0
        for r0 in range(0, t, tq):
            q = _rms_rows(q_ref[r0:r0 + tq, hh * LANE:(hh + 1) * LANE], gq_ref[...])
            if use_rope:
                q = _rope_lanes(q, cos_ref[r0:r0 + tq, :], sin_ref[r0:r0 + tq, :], n_rot)
            qb = q.astype(BF16)
            o = _softmax_pv([_dot_nt(qb, k) * scale for k in keys[j]], vals[j])
            o_ref[r0:r0 + tq, hh * LANE:(hh + 1) * LANE] = o.astype(o_ref.dtype)


def _diff_fused_kernel(*refs, hp, t, scale, dd, n_rot, use_rope, has_cache, aliased):
    (q_ref, k_ref, v_ref, lam_ref, g_ref), (cos_ref, sin_ref), (ck_ref, cv_ref), o_ref = _attn_refs(
        refs, 5, use_rope, has_cache, 2, aliased)
    tq = min(ATTN_Q_ROWS, t)
    for hh in range(hp):
        sl = slice(hh * LANE, (hh + 1) * LANE)
        k = k_ref[:, sl]
        if use_rope:
            k = _rope_lanes(k, cos_ref[...], sin_ref[...], n_rot)
        keys, vals = [k.astype(BF16)], [v_ref[:, sl].astype(BF16)]
        if has_cache:
            keys.insert(0, ck_ref[:, sl].astype(BF16))
            vals.insert(0, cv_ref[:, sl].astype(BF16))
        for r0 in range(0, t, tq):
            q = q_ref[r0:r0 + tq, sl]
            if use_rope:
                q = _rope_lanes(q, cos_ref[r0:r0 + tq, :], sin_ref[r0:r0 + tq, :], n_rot)
            lane = lax.broadcasted_iota(jnp.int32, q.shape, 1)
            q1 = jnp.where(lane < dd, q, 0.0).astype(BF16)
            q2 = jnp.where(lane < dd, 0.0, q).astype(BF16)
            a1 = _softmax_pv([_dot_nt(q1, kk) * scale for kk in keys], vals)
            a2 = _softmax_pv([_dot_nt(q2, kk) * scale for kk in keys], vals)
            d = a1 - lam_ref[...] * a2
            o_ref[r0:r0 + tq, sl] = _rms_rows(d, g_ref[...]).astype(o_ref.dtype)


def _mla_fused_kernel(*refs, hp, t, scale, rope_d, n_rot, use_rope, has_cache, aliased):
    (q_ref, kv_ref, kr_ref), (cos_ref, sin_ref), (kvc_ref, krc_ref), o_ref = _attn_refs(
        refs, 3, use_rope, has_cache, 2, aliased)
    tq = min(ATTN_Q_ROWS, t)
    hw = 2 * LANE
    kr = kr_ref[...]
    if use_rope:
        kr = _rope_lanes(kr, cos_ref[...], sin_ref[...], n_rot)
    krs = [kr[:, :rope_d].astype(BF16)]
    if has_cache:
        krs.insert(0, krc_ref[...].astype(BF16))
    for hh in range(hp):
        c0 = hh * hw
        kns, vals = [kv_ref[:, c0:c0 + LANE]], [kv_ref[:, c0 + LANE:c0 + hw]]
        if has_cache:
            kns.insert(0, kvc_ref[:, c0:c0 + LANE])
            vals.insert(0, kvc_ref[:, c0 + LANE:c0 + hw])
        for r0 in range(0, t, tq):
            qn = q_ref[r0:r0 + tq, c0:c0 + LANE].astype(BF16)
            qr = q_ref[r0:r0 + tq, c0 + LANE:c0 + hw]
            if use_rope:
                qr = _rope_lanes(qr, cos_ref[r0:r0 + tq, :], sin_ref[r0:r0 + tq, :], n_rot)
            qr = qr[:, :rope_d].astype(BF16)
            scores = [(_dot_nt(qn, kn) + _dot_nt(qr, kk)) * scale for kn, kk in zip(kns, krs)]
            o_ref[r0:r0 + tq, hh * LANE:(hh + 1) * LANE] = _softmax_pv(scores, vals).astype(o_ref.dtype)


def _attn_launch(body, n_rows, heads, hp, row0, b, t, main, rope, cache, prev, width_est):
    rb0 = row0 // t
    in_specs = [s for _, s in main]
    args = [a for a, _ in main]
    if rope is not None:
        in_specs += [pl.BlockSpec((t, LANE), lambda bi, hb: (0, 0))] * 2
        args += list(rope)
    in_specs += [s for _, s in cache]
    args += [a for a, _ in cache]
    aliases = {}
    if prev is not None:
        aliases = {len(args): 0}
        in_specs.append(pl.BlockSpec(memory_space=pl.ANY))
        args.append(prev)
    n_cached = cache[1][0].shape[2] if cache else 0
    est = 2 * width_est + 8 * min(ATTN_Q_ROWS, t) * (t + n_cached) * 4 + (4 << 20)
    return pl.pallas_call(
        body,
        grid=(b, heads // hp),
        in_specs=in_specs,
        out_specs=pl.BlockSpec((t, hp * LANE), lambda bi, hb: (rb0 + bi, hb)),
        out_shape=jax.ShapeDtypeStruct((n_rows, heads * LANE), BF16),
        input_output_aliases=aliases,
        compiler_params=_cparams(("arbitrary", "arbitrary"), est),
    )(*args)


def _hgrn_kernel(*refs, t, nb, chunk, sub, q_scale, has_s0, want_state, aliased):
    it = iter(refs)
    dq_ref, di_ref, df_ref, dfb_ref, dg_ref, tab_ref = (next(it) for _ in range(6))
    s0_ref = next(it) if has_s0 else None
    if aliased:
        next(it)
    o_ref = next(it)
    st_out_ref = next(it) if want_state else None
    st_scr, o_scr, cum_scr, k_scr, q_scr = (next(it) for _ in range(5))

    n_chunks = t // chunk
    n_sub = chunk // sub
    half = sub // 2
    half_row = lax.broadcasted_iota(jnp.int32, (half, 1), 0)
    row = lax.broadcasted_iota(jnp.int32, (chunk, chunk), 0)
    col = lax.broadcasted_iota(jnp.int32, (chunk, chunk), 1)
    key_row = lax.broadcasted_iota(jnp.int32, (chunk, 1), 0)
    neg_inf = jnp.float32(-jnp.inf)
    tab = tab_ref[...]
    tri_f = jnp.where(row >= col, 1.0, 0.0).astype(F32)
    tri_b = jnp.where(row <= col, 1.0, 0.0).astype(F32)
    same_sub = (row // sub) == (col // sub)

    for bb in range(nb):
        for direction in range(2):
            if has_s0:
                st_scr[2 * bb + direction] = s0_ref[bb, direction, 0].T
            else:
                st_scr[2 * bb + direction] = jnp.zeros(st_scr.shape[1:], F32)

    def prepare(ci, carry):
        r0 = pl.multiple_of(ci * chunk, chunk)
        zq = dq_ref[pl.ds(r0, chunk), :]
        q_scr[pl.ds(r0, chunk), :] = zq * jax.nn.sigmoid(zq) * q_scale
        for direction, z_ref in enumerate((df_ref, dfb_ref)):
            log_lb = tab[3 * direction:3 * direction + 1, :]
            log_1m = tab[3 * direction + 1:3 * direction + 2, :]
            one_m = tab[3 * direction + 2:3 * direction + 3, :]
            z = z_ref[pl.ds(r0, chunk), :]
            e = jnp.exp(-jnp.abs(z))
            d1 = 1.0 + e
            t2 = log_1m + (jnp.minimum(z, 0.0) - jnp.log(d1))
            g = jnp.maximum(log_lb, t2) + jnp.log(1.0 + jnp.exp(-jnp.abs(log_lb - t2)))
            k_scr[direction, pl.ds(r0, chunk), :] = one_m * (jnp.where(z >= 0.0, e, 1.0) / d1)
            cum_scr[direction, pl.ds(r0, chunk), :] = jnp.dot(
                tri_b if direction else tri_f, g, precision=lax.Precision.HIGHEST, preferred_element_type=F32)
        return carry

    lax.fori_loop(0, nb * n_chunks, prepare, 0, unroll=2)

    def chain_step(ci, bb, direction):
        rev = direction == 1
        r0 = pl.multiple_of(bb * t + ((n_chunks - 1 - ci) if rev else ci) * chunk, chunk)
        q = q_scr[pl.ds(r0, chunk), :]
        k = k_scr[direction, pl.ds(r0, chunk), :]
        cum = cum_scr[direction, pl.ds(r0, chunk), :]
        v = di_ref[pl.ds(r0, chunk), :]
        edge = cum[0:1, :] if rev else cum[chunk - 1:chunk, :]
        slot = 2 * bb + direction
        st = st_scr[slot]
        vb = v.astype(BF16)
        inter = _dot_nt((q * jnp.exp(cum)).astype(BF16), st.astype(BF16))

        mid = [cum[i * sub + half:i * sub + half + 1, :] if rev else cum[i * sub + half - 1:i * sub + half, :]
               for i in range(n_sub)]
        base2 = jnp.concatenate([jnp.broadcast_to(m, (sub, LANE)) for m in mid], axis=0)
        late = (key_row % sub) >= half
        q_side, k_side = (~late, late) if rev else (late, ~late)
        q2 = (q * jnp.exp(jnp.where(q_side, cum - base2, neg_inf))).astype(BF16)
        k2 = (k * jnp.exp(jnp.where(k_side, base2 - cum, neg_inf))).astype(BF16)
        att2 = jnp.where(same_sub, _dot_nt(q2, k2), 0.0)
        near = inter + jnp.dot(att2.astype(BF16), vb, preferred_element_type=F32)

        outs = []
        for i in range(n_sub):
            lo = i * sub
            acc = near[lo:lo + sub]
            if (i < n_sub - 1) if rev else (i > 0):
                base = cum[lo + sub:lo + sub + 1, :] if rev else cum[lo - 1:lo, :]
                seen = (key_row >= lo + sub) if rev else (key_row < lo)
                qn = (q[lo:lo + sub] * jnp.exp(cum[lo:lo + sub] - base)).astype(BF16)
                kn = (k * jnp.exp(jnp.where(seen, base - cum, neg_inf))).astype(BF16)
                att = _dot_nt(qn, kn)
                acc = acc + jnp.dot(att.astype(BF16), vb, preferred_element_type=F32)
            for hb in range(sub // half):
                lo8 = lo + hb * half
                q8 = q[lo8:lo8 + half]
                cum8 = cum[lo8:lo8 + half]
                acc8 = acc[hb * half:(hb + 1) * half]
                for s in range(half):
                    r = lo8 + s
                    live = (half_row <= s) if rev else (half_row >= s)
                    w = jnp.exp(jnp.where(live, cum8 - cum_scr[direction, pl.ds(r0 + r, 1), :], neg_inf))
                    colv = jnp.sum(q8 * k_scr[direction, pl.ds(r0 + r, 1), :] * w, axis=-1, keepdims=True)
                    acc8 = acc8 + colv * di_ref[pl.ds(r0 + r, 1), :]
                outs.append(acc8)
        o_scr[direction, pl.ds(r0, chunk), :] = jnp.concatenate(outs, axis=0)
        kl = (k * jnp.exp(edge - cum)).astype(BF16)
        st_scr[2 * bb + direction] = st * jnp.exp(edge) + _dot_tn(vb, kl)

    def body(ci, carry):
        for bb in range(nb):
            for direction in range(2):
                chain_step(ci, bb, direction)
        return carry

    lax.fori_loop(0, n_chunks, body, 0)

    tot = o_scr[0] + o_scr[1]
    zg = dg_ref[...]
    y = tot * lax.rsqrt(jnp.mean(tot * tot, axis=-1, keepdims=True) + RMS_EPS) * tab[6:7, :]
    o_ref[...] = (y * (zg * jax.nn.sigmoid(zg))).astype(o_ref.dtype)
    if want_state:
        for bb in range(nb):
            for direction in range(2):
                st_out_ref[bb, direction, 0] = st_scr[2 * bb + direction].T


def hgrn_mix(h, row0, b, t, heads, dk, dv, cols, tab, s0, want_state, prev):
    assert dk == LANE and dv == LANE
    nb = HGRN_BATCHES if (b % HGRN_BATCHES == 0 and row0 % (HGRN_BATCHES * t) == 0) else 1
    assert row0 % (nb * t) == 0 and all(cc % LANE == 0 for cc in cols)
    chunk = min(HGRN_CHUNK, t)
    sub = min(HGRN_SUB, chunk)
    rb0 = row0 // (nb * t)

    def col_spec(c0):
        return pl.BlockSpec((nb * t, LANE), lambda bi, hh: (rb0 + bi, c0 // LANE + hh))

    st_spec = pl.BlockSpec((nb, 2, 1, dk, dv), lambda bi, hh: (bi, 0, hh, 0, 0))
    in_specs = [col_spec(cc) for cc in cols] + [pl.BlockSpec((8, LANE), lambda bi, hh: (0, hh))]
    args = [h] * len(cols) + [tab]
    if s0 is not None:
        in_specs.append(st_spec)
        args.append(s0)
    aliases = {}
    if prev is not None:
        aliases = {len(args): 0}
        in_specs.append(pl.BlockSpec(memory_space=pl.ANY))
        args.append(prev)
    out_specs = [pl.BlockSpec((nb * t, LANE), lambda bi, hh: (rb0 + bi, hh))]
    out_shape = [jax.ShapeDtypeStruct((h.shape[0], heads * dv), BF16)]
    if want_state:
        out_specs.append(st_spec)
        out_shape.append(jax.ShapeDtypeStruct((b, 2, heads, dk, dv), F32))
    est = 2 * nb * t * LANE * (5 * 4 + 2) + 7 * nb * t * LANE * 4 + (8 * nb + 6) * dk * dv * 4 + (4 << 20)
    res = pl.pallas_call(
        functools.partial(_hgrn_kernel, t=t, nb=nb, chunk=chunk, sub=sub, q_scale=dk ** -0.5,
                          has_s0=s0 is not None, want_state=want_state, aliased=prev is not None),
        grid=(b // nb, heads),
        in_specs=in_specs,
        out_specs=out_specs,
        out_shape=out_shape,
        input_output_aliases=aliases,
        scratch_shapes=[pltpu.VMEM((2 * nb, dv, dk), F32), pltpu.VMEM((2, nb * t, LANE), F32),
                        pltpu.VMEM((2, nb * t, LANE), F32), pltpu.VMEM((2, nb * t, LANE), F32),
                        pltpu.VMEM((nb * t, LANE), F32)],
        compiler_params=_cparams(("arbitrary", "arbitrary"), est),
    )(*args)
    return (res[0], res[1]) if want_state else (res[0], None)


def _rms(x, g):
    return x * lax.rsqrt(jnp.mean(x * x, axis=-1, keepdims=True) + RMS_EPS) * g


def _rope_lane_tables(t, d):
    n = d // 4
    inv = ROPE_THETA ** (-jnp.arange(n, dtype=F32) / n)
    pos = jnp.arange(t)
    rows = (pos // GRID_W).astype(F32)
    cols = (pos % GRID_W).astype(F32)
    ar = rows[:, None] * inv[None, :]
    ac = cols[:, None] * inv[None, :]
    cos = jnp.concatenate([jnp.cos(ar), jnp.cos(ar), jnp.cos(ac), jnp.cos(ac)], axis=-1)
    sin = jnp.concatenate([-jnp.sin(ar), jnp.sin(ar), -jnp.sin(ac), jnp.sin(ac)], axis=-1)
    reps = LANE // d
    return jnp.tile(cos, (1, reps)), jnp.tile(sin, (1, reps))


def kernel(x_prompt, x_sample, cache_mla_ckv, cache_mla_krope, cache_diff_k, cache_diff_v, cache_gqa_k, cache_gqa_v, state_hgrn, c, c_ctx, w_in, w_out, mla_g_cq, mla_g_ckv, mla_w_uq, mla_w_ukv, diff_lambda, diff_subln_g, gqa_g_q, gqa_g_k, hgrn_gamma, hgrn_g_norm, ada_w, ada_b, ln_g, ln_b, ffn_w1, ffn_w3, ffn_w2, moe_router, moe_w1, moe_w3, moe_w2):
    bc, tc, d = x_prompt.shape
    bl, tl, _ = x_sample.shape
    depth = w_in.shape[0]
    q_rank = mla_g_cq.shape[1]
    kv_rank = mla_g_ckv.shape[1]
    rope_d = cache_mla_krope.shape[-1]
    heads = cache_diff_k.shape[3]
    dd = cache_diff_k.shape[4] // 2
    kv_heads = cache_gqa_k.shape[3]
    gd = cache_gqa_k.shape[4]
    hk = state_hgrn.shape[4]
    hv = state_hgrn.shape[5]
    n_exp = moe_w1.shape[1]
    alpha = (2 * depth) ** 0.25
    n_ctx = bc * tc
    n_lat = bl * tl
    tok = _Tokens(n_ctx, tl, bl, d)
    mla_hw = 2 * LANE
    assert MLA_NOPE_DIM + rope_d <= mla_hw and MLA_NOPE_DIM + MLA_V_DIM == mla_hw

    unit = heads * LANE
    kvw = kv_heads * gd
    assert 2 * dd == LANE and gd == LANE and hk == LANE and hv == LANE
    names = ("cq", "ckv", "kr", "bq", "bk", "bv", "gq", "gk", "gv", "dq", "di", "df", "dfb", "dg")
    widths = (q_rank, kv_rank, rope_d, unit, unit, unit, unit, kvw, kvw, unit, unit, unit, unit, unit)
    assert sum(widths) == w_in.shape[2]
    src, o = {}, 0
    for nm, w in zip(names, widths):
        src[nm] = (o, w)
        o += w
    col, pieces, o = {}, [], 0

    def place(nm, pad_to=None):
        nonlocal o
        if nm is not None:
            s0, w = src[nm]
            col[nm] = o
            pieces.append(w_in[:, :, s0:s0 + w])
            o += w
        if pad_to is not None and o % pad_to:
            z = pad_to - o % pad_to
            pieces.append(jnp.zeros((depth, d, z), w_in.dtype))
            o += z

    place("cq")
    place("gk", pad_to=unit)
    place("ckv")
    place("gv")
    place("kr", pad_to=LANE)
    place(None, pad_to=unit)
    for nm in ("bq", "bk", "bv", "gq", "dq", "di", "df", "dfb", "dg"):
        place(nm)
    w_in_b = jnp.concatenate(pieces, axis=-1).astype(BF16)
    assert q_rank <= unit and col["gk"] % kvw == 0 and col["gv"] % kvw == 0 and col["ckv"] % kv_rank == 0
    w_uq_b = jnp.pad(mla_w_uq.reshape(depth, q_rank, heads, MLA_NOPE_DIM + rope_d),
                     ((0, 0), (0, 0), (0, 0), (0, mla_hw - MLA_NOPE_DIM - rope_d))
                     ).reshape(depth, q_rank, heads * mla_hw).astype(BF16)
    w_ukv_b = mla_w_ukv.astype(BF16)
    ffn_w2_b = ffn_w2.astype(BF16)
    e_ff = moe_w1.shape[-1]
    n_moe = moe_w1.shape[0]
    moe_w1_r = moe_w1.reshape(n_moe * n_exp, d, e_ff)
    moe_w3_r = moe_w3.reshape(n_moe * n_exp, d, e_ff)
    moe_w2_b = moe_w2.astype(BF16).reshape(n_moe * n_exp, e_ff, d)
    router_b = jnp.pad(moe_router, ((0, 0), (0, 0), (0, LANE - n_exp))).astype(BF16)

    cond = jnp.concatenate([c_ctx[None, :], c], axis=0)
    n_cond = cond.shape[0]
    cond_p = jnp.pad(jax.nn.silu(cond), ((0, -n_cond % 16), (0, 0))).astype(BF16)
    mods = []
    for l in range(depth):
        ml = matmul_few_rows(cond_p, ada_w, l)[:n_cond] + ada_b[l][None, :]
        mods.append(ml.reshape(n_cond, 6, d).transpose(1, 0, 2))
    mod = jnp.concatenate(mods, axis=0).reshape(depth * 6, n_cond, 1, d)
    ln_gb = jnp.concatenate([ln_g.reshape(depth * 2, 1, d), ln_b.reshape(depth * 2, 1, d)], axis=0)

    def lower_bounds(gamma):
        cs = jnp.cumsum(jax.nn.softmax(gamma.astype(F32), axis=0), axis=0)
        return cs - cs[0:1]

    lb_f = lower_bounds(hgrn_gamma[0])
    lb_b = lower_bounds(hgrn_gamma[1])
    n_rot_small, n_rot_big = dd // 4, gd // 4
    assert rope_d == dd
    rope_small = _rope_lane_tables(tl, dd)
    rope_big = _rope_lane_tables(tl, gd)
    n_tok = n_ctx + n_lat
    past = cache_mla_ckv.shape[2]
    cache_dk = cache_diff_k.reshape(bl, depth, past, unit)
    cache_dv = cache_diff_v.reshape(bl, depth, past, unit)
    cache_gk = cache_gqa_k.reshape(bl, depth, past, kvw)
    cache_gv = cache_gqa_v.reshape(bl, depth, past, kvw)
    hgrn_cols = [col[nm] for nm in ("dq", "di", "df", "dfb", "dg")]

    def mixers(h, q_all, kv_new, kv_cache, l, lat, prevs):
        row0, b, t = (n_ctx, bl, tl) if lat else (0, bc, tc)
        hp = 1 if lat else heads
        rb0 = row0 // t
        flags = dict(use_rope=lat, has_cache=lat, aliased=prevs is not None)
        pa, pb, pc, pd = prevs if prevs is not None else (None,) * 4

        def hcols(c0, width):
            return pl.BlockSpec((t, width), lambda bi, hb: (rb0 + bi, c0 // width + hb))

        def vec(n):
            return pl.BlockSpec((1, n), lambda bi, hb: (0, 0))

        def cache_spec(width, per_head):
            if per_head:
                return pl.BlockSpec((None, None, past, width), lambda bi, hb: (bi, l, 0, hb))
            return pl.BlockSpec((None, None, past, width), lambda bi, hb: (bi, l, 0, 0))

        wide = pl.BlockSpec((t, hp * mla_hw), lambda bi, hb: (rb0 + bi, hb))
        main = [(q_all, wide), (kv_new, wide), (h, pl.BlockSpec((t, LANE), lambda bi, hb: (rb0 + bi, col["kr"] // LANE)))]
        cache = []
        if lat:
            cache = [(kv_cache, pl.BlockSpec((past, hp * mla_hw), lambda bi, hb: (bi, hb))),
                     (cache_mla_krope, cache_spec(rope_d, False))]
        o_a = _attn_launch(
            functools.partial(_mla_fused_kernel, hp=hp, t=t, scale=(MLA_NOPE_DIM + rope_d) ** -0.5, rope_d=rope_d,
                              n_rot=n_rot_small, **flags),
            n_tok, heads, hp, row0, b, t, main, rope_small if lat else None, cache, pa,
            t * hp * mla_hw * 6 + past * hp * mla_hw * 2)

        lam_p = diff_lambda[l].astype(F32)
        lam_init = 0.8 - 0.6 * math.exp(-0.3 * l)
        lam = jnp.exp(jnp.sum(lam_p[0] * lam_p[1])) - jnp.exp(jnp.sum(lam_p[2] * lam_p[3])) + lam_init
        lam_v = jnp.full((1, LANE), lam, F32)
        g_v = (diff_subln_g[l] * (1.0 - lam_init)).reshape(1, LANE)
        w = hp * LANE
        main = [(h, hcols(col["bq"], w)), (h, hcols(col["bk"], w)), (h, hcols(col["bv"], w)),
                (lam_v, vec(LANE)), (g_v, vec(LANE))]
        cache = [(cache_dk, cache_spec(w, True)), (cache_dv, cache_spec(w, True))] if lat else []
        o_b = _attn_launch(
            functools.partial(_diff_fused_kernel, hp=hp, t=t, scale=dd ** -0.5, dd=dd, n_rot=n_rot_small, **flags),
            n_tok, heads, hp, row0, b, t, main, rope_small if lat else None, cache, pb,
            (3 * t + 2 * past) * w * 4)

        kv_group = heads // kv_heads
        if lat:
            kspec = pl.BlockSpec((t, LANE), lambda bi, hb: (rb0 + bi, col["gk"] // LANE + hb // kv_group))
            vspec = pl.BlockSpec((t, LANE), lambda bi, hb: (rb0 + bi, col["gv"] // LANE + hb // kv_group))
            cspec = pl.BlockSpec((None, None, past, LANE), lambda bi, hb: (bi, l, 0, hb // kv_group))
            cache = [(cache_gk, cspec), (cache_gv, cspec)]
        else:
            kspec = pl.BlockSpec((t, kvw), lambda bi, hb: (rb0 + bi, col["gk"] // kvw))
            vspec = pl.BlockSpec((t, kvw), lambda bi, hb: (rb0 + bi, col["gv"] // kvw))
            cache = []
        main = [(h, hcols(col["gq"], w)), (h, kspec), (h, vspec),
                (gqa_g_q[l].reshape(1, gd), vec(gd)), (gqa_g_k[l].reshape(1, gd), vec(gd))]
        o_c = _attn_launch(
            functools.partial(_gqa_fused_kernel, hp=hp, kv_group=kv_group, t=t, scale=gd ** -0.5, n_rot=n_rot_big,
                              **flags),
            n_tok, heads, hp, row0, b, t, main, rope_big if lat else None, cache, pc,
            (t + 2 * t + 2 * past) * w * 4)

        tab = jnp.stack([jnp.log(lb_f[l]), jnp.log1p(-lb_f[l]), 1.0 - lb_f[l],
                         jnp.log(lb_b[l]), jnp.log1p(-lb_b[l]), 1.0 - lb_b[l],
                         jnp.tile(hgrn_g_norm[l], heads), jnp.zeros((heads * hk,), F32)], axis=0)
        o_d, s_new = hgrn_mix(h, row0, b, t, heads, hk, hv, hgrn_cols, tab, state_hgrn[:, l] if lat else None,
                              not lat, pd)
        return (o_a, o_b, o_c, o_d), s_new

    x = jnp.concatenate([x_prompt.reshape(n_ctx, d), x_sample.reshape(n_lat, d)], axis=0)
    u = modulate(tok, x, mod, 1, 0)
    ctx_out = []
    for l in range(depth):
        h = matmul(u, w_in_b, F32, layer=l)
        q_all = rms_matmul(h, col["cq"], unit, mla_g_cq[l].reshape(1, q_rank), w_uq_b, l, F32)
        kv_new = rms_matmul(h, col["ckv"], kv_rank, mla_g_ckv[l].reshape(1, kv_rank), w_ukv_b, l, BF16)
        kv_cache = matmul(cache_mla_ckv[:, l].reshape(bl * past, kv_rank).astype(BF16), w_ukv_b, BF16, layer=l)
        parts, s_new = mixers(h, q_all, kv_new, kv_cache, l, False, None)
        parts, _ = mixers(h, q_all, kv_new, kv_cache, l, True, parts)
        hc = h[:n_ctx]

        def piece(nm, width):
            return hc[:, col[nm]:col[nm] + width]

        ctx_out.append((
            _rms(piece("ckv", kv_rank), mla_g_ckv[l]).reshape(bc, tc, kv_rank),
            piece("kr", rope_d).reshape(bc, tc, rope_d),
            piece("bk", unit).reshape(bc, tc, heads, 2 * dd),
            piece("bv", unit).reshape(bc, tc, heads, 2 * dd),
            _rms(piece("gk", kvw).reshape(bc, tc, kv_heads, gd), gqa_g_k[l]),
            piece("gv", kvw).reshape(bc, tc, kv_heads, gd),
            s_new))
        mix = out_proj(parts, w_out, l)
        x, u = residual_ln(tok, x, mix, mod, l * 6 + 2, ln_gb, 2 * l, 2 * depth + 2 * l, alpha,
                           l * 6 + 4, l * 6 + 3)
        j = l // 2
        if l % 2 == 0:
            f = down_proj(swiglu_up(u, ffn_w1, ffn_w3, j), ffn_w2_b[j])
        else:
            logits = matmul(u, router_b, F32, layer=j)[:, :n_exp]
            f = moe_experts(u, logits, moe_w1_r, moe_w3_r, moe_w2_b, j, n_exp)
        if l + 1 < depth:
            x, u = residual_ln(tok, x, f, mod, l * 6 + 5, ln_gb, 2 * l + 1, 2 * depth + 2 * l + 1, alpha,
                               (l + 1) * 6 + 1, (l + 1) * 6 + 0)
        else:
            x, _ = residual_ln(tok, x, f, mod, l * 6 + 5, ln_gb, 2 * l + 1, 2 * depth + 2 * l + 1, alpha)

    y_prompt = x[:n_ctx].reshape(bc, tc, d)
    y_sample = x[n_ctx:].reshape(bl, tl, d)
    stack = lambda i: jnp.stack([e[i] for e in ctx_out], axis=1)
    return (y_prompt, y_sample, stack(0), stack(1), stack(2), stack(3), stack(4), stack(5), stack(6))
```

```python
import functools
import math

import jax
import jax.numpy as jnp
from jax import lax
from jax.experimental import pallas as pl
from jax.experimental.pallas import tpu as pltpu

F32 = jnp.float32
BF16 = jnp.bfloat16

GRID_W = 64
ROPE_THETA = 10000.0
LN_EPS = 1e-5
RMS_EPS = 1e-6
MLA_NOPE_DIM = 128
MLA_V_DIM = 128
TOP_K = 2

LANE = 128
V7X_VMEM_BYTES = 64 * 1024 * 1024
VMEM_CAP = V7X_VMEM_BYTES - 6 * 1024 * 1024
HGRN_CHUNK = 64
HGRN_SUB = 16
HGRN_BATCHES = 4
MOE_ROWS = 512


def _cparams(sem, est_bytes):
    limit = int(min(VMEM_CAP, max(32 * 1024 * 1024, est_bytes * 5 // 4 + (4 << 20))))
    return pltpu.CompilerParams(dimension_semantics=sem, vmem_limit_bytes=limit)


def _pick_tile(n, target):
    if n <= target:
        return n
    t = (target // LANE) * LANE
    while t >= LANE:
        if n % t == 0:
            return t
        t -= LANE
    return n


def _pick_rows(n, target):
    if n <= target:
        return n
    t = target
    while t >= 8:
        if n % t == 0:
            return t
        t //= 2
    return n


def _dot_nt(a, b):
    return lax.dot_general(a, b, (((1,), (1,)), ((), ())), preferred_element_type=F32)


def _dot_tn(a, b):
    return lax.dot_general(a, b, (((0,), (0,)), ((), ())), preferred_element_type=F32)


def _mm_kernel(x_ref, w_ref, o_ref):
    o_ref[...] = jnp.dot(x_ref[...], w_ref[...], preferred_element_type=F32).astype(o_ref.dtype)


def matmul(x, w, out_dtype, layer=None, tm=1024, tn=1024):
    m, k = x.shape
    n = w.shape[-1]
    tm = _pick_rows(m, tm)
    tn = _pick_tile(n, tn)
    if w.ndim == 3:
        w_spec = pl.BlockSpec((None, k, tn), lambda j, i: (layer, 0, j))
    else:
        w_spec = pl.BlockSpec((k, tn), lambda j, i: (0, j))
    est = 2 * (tm * k * 2 + k * tn * 2 + tm * tn * jnp.dtype(out_dtype).itemsize) + tm * tn * 4
    return pl.pallas_call(
        _mm_kernel,
        grid=(n // tn, m // tm),
        in_specs=[pl.BlockSpec((tm, k), lambda j, i: (i, 0)), w_spec],
        out_specs=pl.BlockSpec((tm, tn), lambda j, i: (i, j)),
        out_shape=jax.ShapeDtypeStruct((m, n), out_dtype),
        compiler_params=_cparams(("arbitrary", "arbitrary"), est),
    )(x, w)


def _mm_castw_kernel(x_ref, w_ref, o_ref):
    o_ref[...] = jnp.dot(x_ref[...], w_ref[...].astype(BF16), preferred_element_type=F32).astype(o_ref.dtype)


def matmul_few_rows(x, w, layer, tn=1024):
    m, k = x.shape
    n = w.shape[-1]
    tn = _pick_tile(n, tn)
    est = 2 * (m * k * 2 + k * tn * 4 + m * tn * 4) + k * tn * 2
    return pl.pallas_call(
        _mm_castw_kernel,
        grid=(n // tn,),
        in_specs=[pl.BlockSpec((m, k), lambda j: (0, 0)), pl.BlockSpec((None, k, tn), lambda j: (layer, 0, j))],
        out_specs=pl.BlockSpec((m, tn), lambda j: (0, j)),
        out_shape=jax.ShapeDtypeStruct((m, n), F32),
        compiler_params=_cparams(("arbitrary",), est),
    )(x, w)


def _rms_mm_kernel(x_ref, g_ref, w_ref, o_ref):
    x = x_ref[:, :w_ref.shape[0]]
    xn = x * lax.rsqrt(jnp.mean(x * x, axis=-1, keepdims=True) + RMS_EPS) * g_ref[...]
    o_ref[...] = jnp.dot(xn.astype(BF16), w_ref[...], preferred_element_type=F32).astype(o_ref.dtype)


def rms_matmul(h, col0, kb, gain, w, layer, out_dtype, tm=1024, tn=1024):
    m = h.shape[0]
    k, n = w.shape[1], w.shape[2]
    assert col0 % kb == 0 and kb >= k
    tm = _pick_rows(m, tm)
    tn = _pick_tile(n, tn)
    est = 2 * (tm * kb * 4 + k * tn * 2 + tm * tn * 4) + 3 * tm * k * 4 + tm * tn * 4
    return pl.pallas_call(
        _rms_mm_kernel,
        grid=(n // tn, m // tm),
        in_specs=[pl.BlockSpec((tm, kb), lambda j, i: (i, col0 // kb)),
                  pl.BlockSpec((1, k), lambda j, i: (0, 0)),
                  pl.BlockSpec((None, k, tn), lambda j, i: (layer, 0, j))],
        out_specs=pl.BlockSpec((tm, tn), lambda j, i: (i, j)),
        out_shape=jax.ShapeDtypeStruct((m, n), out_dtype),
        compiler_params=_cparams(("arbitrary", "arbitrary"), est),
    )(h, gain, w)


def _out_proj_kernel(a_ref, b_ref, c_ref, d_ref, w_ref, o_ref, wb_scr):
    @pl.when(pl.program_id(1) == 0)
    def _():
        wb_scr[...] = w_ref[...].astype(BF16)

    kq = a_ref.shape[1]
    acc = jnp.dot(a_ref[...], wb_scr[0:kq, :], preferred_element_type=F32)
    for g, r in enumerate((b_ref, c_ref, d_ref), start=1):
        acc = acc + jnp.dot(r[...], wb_scr[g * kq:(g + 1) * kq, :], preferred_element_type=F32)
    o_ref[...] = acc


def out_proj(parts, w, layer, tm=1024, tn=512):
    m, kq = parts[0].shape
    k, n = w.shape[1], w.shape[2]
    assert len(parts) == 4 and 4 * kq == k
    tm = _pick_rows(m, tm)
    tn = _pick_tile(n, tn)
    est = 2 * (4 * tm * kq * 2 + k * tn * 4 + tm * tn * 4) + k * tn * 2 + 2 * tm * tn * 4
    part_spec = pl.BlockSpec((tm, kq), lambda j, i: (i, 0))
    return pl.pallas_call(
        _out_proj_kernel,
        grid=(n // tn, m // tm),
        in_specs=[part_spec] * 4 + [pl.BlockSpec((None, k, tn), lambda j, i: (layer, 0, j))],
        out_specs=pl.BlockSpec((tm, tn), lambda j, i: (i, j)),
        out_shape=jax.ShapeDtypeStruct((m, n), F32),
        scratch_shapes=[pltpu.VMEM((k, tn), BF16)],
        compiler_params=_cparams(("arbitrary", "arbitrary"), est),
    )(*parts, w)


def _swiglu_kernel(x_ref, w1_ref, w3_ref, o_ref, w1b_scr, w3b_scr):
    @pl.when(pl.program_id(1) == 0)
    def _():
        w1b_scr[...] = w1_ref[...].astype(BF16)
        w3b_scr[...] = w3_ref[...].astype(BF16)

    x = x_ref[...]
    a = jnp.dot(x, w1b_scr[...], preferred_element_type=F32)
    b = jnp.dot(x, w3b_scr[...], preferred_element_type=F32)
    o_ref[...] = (a * jax.nn.sigmoid(a) * b).astype(o_ref.dtype)


def swiglu_up(x, w1, w3, layer, tm=1024, tn=256):
    m, k = x.shape
    n = w1.shape[-1]
    tm = _pick_rows(m, tm)
    tn = _pick_tile(n, tn)
    w_spec = pl.BlockSpec((None, k, tn), lambda j, i: (layer, 0, j))
    est = 2 * (tm * k * 2 + 2 * k * tn * 4 + tm * tn * 2) + 2 * k * tn * 2 + 3 * tm * tn * 4
    return pl.pallas_call(
        _swiglu_kernel,
        grid=(n // tn, m // tm),
        in_specs=[pl.BlockSpec((tm, k), lambda j, i: (i, 0)), w_spec, w_spec],
        out_specs=pl.BlockSpec((tm, tn), lambda j, i: (i, j)),
        out_shape=jax.ShapeDtypeStruct((m, n), BF16),
        scratch_shapes=[pltpu.VMEM((k, tn), BF16), pltpu.VMEM((k, tn), BF16)],
        compiler_params=_cparams(("arbitrary", "arbitrary"), est),
    )(x, w1, w3)


def _down_kernel(x_ref, w_ref, o_ref):
    p = jnp.dot(x_ref[...], w_ref[...], preferred_element_type=F32)

    @pl.when(pl.program_id(2) == 0)
    def _():
        o_ref[...] = p

    @pl.when(pl.program_id(2) > 0)
    def _():
        o_ref[...] += p


def down_proj(x, w, tm=2048, tn=1024, tk=1792):
    m, k = x.shape
    n = w.shape[-1]
    tm = _pick_rows(m, tm)
    tn = _pick_tile(n, tn)
    tk = _pick_tile(k, tk)
    est = 2 * (tm * tk * 2 + tk * tn * 2 + tm * tn * 4) + tm * tn * 4
    return pl.pallas_call(
        _down_kernel,
        grid=(m // tm, n // tn, k // tk),
        in_specs=[pl.BlockSpec((tm, tk), lambda i, j, s: (i, s)), pl.BlockSpec((tk, tn), lambda i, j, s: (s, j))],
        out_specs=pl.BlockSpec((tm, tn), lambda i, j, s: (i, j)),
        out_shape=jax.ShapeDtypeStruct((m, n), F32),
        compiler_params=_cparams(("arbitrary", "arbitrary", "arbitrary"), est),
    )(x, w)


def _moe_up_kernel(rt_ref, nt_ref, ex_ref, ok_ref, new_ref, x_ref, w1_ref, w3_ref, o_ref, w1b_scr, w3b_scr):
    @pl.when(new_ref[pl.program_id(0)] == 1)
    def _():
        w1b_scr[...] = w1_ref[...].astype(BF16)
        w3b_scr[...] = w3_ref[...].astype(BF16)

    @pl.when(ok_ref[pl.program_id(0)] == 1)
    def _():
        x = x_ref[...]
        a = jnp.dot(x, w1b_scr[...], preferred_element_type=F32)
        b = jnp.dot(x, w3b_scr[...], preferred_element_type=F32)
        o_ref[...] = (a * jax.nn.sigmoid(a) * b).astype(o_ref.dtype)


def _moe_down_kernel(rt_ref, nt_ref, ex_ref, ok_ref, x_ref, w_ref, g_ref, o_ref):
    @pl.when(ok_ref[pl.program_id(0)] == 1)
    def _():
        o_ref[...] = jnp.dot(x_ref[...], w_ref[...], preferred_element_type=F32) * g_ref[...]


def _moe_items(tiles_e, tile0_e, nj, n_tiles):
    n_items = n_tiles * nj
    per_e = tiles_e * nj
    end_e = jnp.cumsum(per_e)
    total = end_e[-1]
    pos = jnp.arange(n_items, dtype=jnp.int32)
    idx = jnp.minimum(pos, total - 1)
    ex = jnp.searchsorted(end_e, idx, side="right").astype(jnp.int32)
    local = idx - (end_e - per_e)[ex]
    te = jnp.maximum(tiles_e[ex], 1)
    rt = tile0_e[ex] + local % te
    nt = local // te
    ok = pos < total
    new = ok & (local % te == 0)
    return rt.astype(jnp.int32), nt.astype(jnp.int32), ex, ok.astype(jnp.int32), new.astype(jnp.int32)


def moe_experts(u, logits, w1, w3, w2, group, n_exp):
    n, d = u.shape
    f = w1.shape[-1]
    tm = min(MOE_ROWS, n)
    top_v, top_i = lax.top_k(logits, TOP_K)
    wts = jax.nn.softmax(top_v, axis=-1)
    n_pairs = n * TOP_K
    n_rows = -(-n_pairs // tm) * tm + n_exp * tm
    n_tiles = n_rows // tm

    e_flat = top_i.reshape(-1).astype(jnp.int32)
    order = jnp.argsort(e_flat, stable=True)
    cnt = jnp.sum(jax.nn.one_hot(e_flat, n_exp, dtype=jnp.int32), axis=0)
    tiles_e = (cnt + tm - 1) // tm
    tile0_e = jnp.cumsum(tiles_e) - tiles_e
    start_e = jnp.cumsum(cnt) - cnt
    e_sorted = e_flat[order]
    dest_sorted = tile0_e[e_sorted] * tm + jnp.arange(n_pairs, dtype=jnp.int32) - start_e[e_sorted]
    dest = dest_sorted[jnp.argsort(order)]
    rows = jnp.arange(n_rows, dtype=jnp.int32)
    e_row = jnp.minimum(jnp.searchsorted((tile0_e + tiles_e) * tm, rows, side="right"), n_exp - 1).astype(jnp.int32)
    r_in = rows - tile0_e[e_row] * tm
    live = (r_in >= 0) & (r_in < cnt[e_row])
    pair = order[jnp.clip(start_e[e_row] + r_in, 0, n_pairs - 1)]
    row_token = jnp.where(live, pair // TOP_K, 0).astype(jnp.int32)
    row_gate = jnp.where(live, wts.reshape(-1)[pair], 0.0)

    xg = jnp.take(u, row_token, axis=0)

    tn = _pick_tile(f, 512)
    items = _moe_items(tiles_e, tile0_e, f // tn, n_tiles)
    w_spec = pl.BlockSpec((None, d, tn), lambda i, rt, nt, ex, ok, new: (group * n_exp + ex[i], 0, nt[i]))
    est = 2 * (tm * d * 2 + 2 * d * tn * 4 + tm * tn * 2) + 2 * d * tn * 2 + 3 * tm * tn * 4
    act = pl.pallas_call(
        _moe_up_kernel,
        grid_spec=pltpu.PrefetchScalarGridSpec(
            num_scalar_prefetch=5, grid=(n_tiles * (f // tn),),
            in_specs=[pl.BlockSpec((tm, d), lambda i, rt, nt, ex, ok, new: (rt[i], 0)), w_spec, w_spec],
            out_specs=pl.BlockSpec((tm, tn), lambda i, rt, nt, ex, ok, new: (rt[i], nt[i])),
            scratch_shapes=[pltpu.VMEM((d, tn), BF16), pltpu.VMEM((d, tn), BF16)]),
        out_shape=jax.ShapeDtypeStruct((n_rows, f), BF16),
        compiler_params=_cparams(("arbitrary",), est),
    )(*items, xg, w1, w3)

    tn2 = _pick_tile(d, 512)
    items2 = _moe_items(tiles_e, tile0_e, d // tn2, n_tiles)[:4]
    est = 2 * (tm * f * 2 + f * tn2 * 2 + tm * tn2 * 4 + tm * LANE * 4) + tm * tn2 * 4
    y = pl.pallas_call(
        _moe_down_kernel,
        grid_spec=pltpu.PrefetchScalarGridSpec(
            num_scalar_prefetch=4, grid=(n_tiles * (d // tn2),),
            in_specs=[pl.BlockSpec((tm, f), lambda i, rt, nt, ex, ok: (rt[i], 0)),
                      pl.BlockSpec((None, f, tn2), lambda i, rt, nt, ex, ok: (group * n_exp + ex[i], 0, nt[i])),
                      pl.BlockSpec((tm, 1), lambda i, rt, nt, ex, ok: (rt[i], 0))],
            out_specs=pl.BlockSpec((tm, tn2), lambda i, rt, nt, ex, ok: (rt[i], nt[i]))),
        out_shape=jax.ShapeDtypeStruct((n_rows, d), F32),
        compiler_params=_cparams(("arbitrary",), est),
    )(*items2, act, w2, row_gate[:, None])

    dest = dest.reshape(n, TOP_K)
    return jnp.take(y, dest[:, 0], axis=0) + jnp.take(y, dest[:, 1], axis=0)


def _modulate_kernel(x_ref, scale_ref, shift_ref, u_ref):
    u_ref[...] = (x_ref[...] * (1.0 + scale_ref[0, 0]) + shift_ref[0, 0]).astype(u_ref.dtype)


def _ln_kernel(x_ref, y_ref, gate_ref, g_ref, b_ref, scale_ref, shift_ref, xo_ref, uo_ref, *, alpha):
    z = alpha * x_ref[...] + gate_ref[0, 0] * y_ref[...]
    zc = z - jnp.mean(z, axis=-1, keepdims=True)
    yn = zc * lax.rsqrt(jnp.mean(zc * zc, axis=-1, keepdims=True) + LN_EPS)
    xn = yn * g_ref[...] + b_ref[...]
    xo_ref[...] = xn
    uo_ref[...] = (xn * (1.0 + scale_ref[0, 0]) + shift_ref[0, 0]).astype(uo_ref.dtype)


def _ln_last_kernel(x_ref, y_ref, gate_ref, g_ref, b_ref, xo_ref, *, alpha):
    z = alpha * x_ref[...] + gate_ref[0, 0] * y_ref[...]
    zc = z - jnp.mean(z, axis=-1, keepdims=True)
    yn = zc * lax.rsqrt(jnp.mean(zc * zc, axis=-1, keepdims=True) + LN_EPS)
    xo_ref[...] = yn * g_ref[...] + b_ref[...]


class _Tokens:
    def __init__(self, n_ctx, t_lat, n_lat_batches, d):
        self.n_ctx, self.t_lat, self.d = n_ctx, t_lat, d
        self.n = n_ctx + t_lat * n_lat_batches
        tm = 256
        while n_ctx % tm or t_lat % tm:
            tm //= 2
        self.tm = tm
        self.ctx_tiles = n_ctx // tm
        self.tiles_per_lat = t_lat // tm

    def mod_row(self, i):
        return jnp.where(i < self.ctx_tiles, 0, 1 + (i - self.ctx_tiles) // self.tiles_per_lat)

    def mod_spec(self, idx):
        return pl.BlockSpec((1, 1, 1, self.d), lambda i: (idx, self.mod_row(i), 0, 0))

    def row_spec(self):
        return pl.BlockSpec((self.tm, self.d), lambda i: (i, 0))

    def vec_spec(self, idx):
        return pl.BlockSpec((None, 1, self.d), lambda i: (idx, 0, 0))


def modulate(tok, x, mod, idx_scale, idx_shift):
    est = 2 * tok.tm * tok.d * 6 + 4 * tok.tm * tok.d * 4
    return pl.pallas_call(
        _modulate_kernel,
        grid=(tok.n // tok.tm,),
        in_specs=[tok.row_spec(), tok.mod_spec(idx_scale), tok.mod_spec(idx_shift)],
        out_specs=tok.row_spec(),
        out_shape=jax.ShapeDtypeStruct((tok.n, tok.d), BF16),
        compiler_params=_cparams(("arbitrary",), est),
    )(x, mod, mod)


def residual_ln(tok, x, y, mod, idx_gate, ln_gb, idx_g, idx_b, alpha, idx_scale=None, idx_shift=None):
    est = 2 * tok.tm * tok.d * (4 + 4 + 4 + 2) + 6 * tok.tm * tok.d * 4
    in_specs = [tok.row_spec(), tok.row_spec(), tok.mod_spec(idx_gate), tok.vec_spec(idx_g), tok.vec_spec(idx_b)]
    args = [x, y, mod, ln_gb, ln_gb]
    if idx_scale is None:
        return pl.pallas_call(
            functools.partial(_ln_last_kernel, alpha=alpha),
            grid=(tok.n // tok.tm,),
            in_specs=in_specs,
            out_specs=tok.row_spec(),
            out_shape=jax.ShapeDtypeStruct((tok.n, tok.d), F32),
            compiler_params=_cparams(("arbitrary",), est),
        )(*args), None
    in_specs += [tok.mod_spec(idx_scale), tok.mod_spec(idx_shift)]
    args += [mod, mod]
    return pl.pallas_call(
        functools.partial(_ln_kernel, alpha=alpha),
        grid=(tok.n // tok.tm,),
        in_specs=in_specs,
        out_specs=[tok.row_spec(), tok.row_spec()],
        out_shape=[jax.ShapeDtypeStruct((tok.n, tok.d), F32), jax.ShapeDtypeStruct((tok.n, tok.d), BF16)],
        compiler_params=_cparams(("arbitrary",), est),
    )(*args)


ATTN_Q_ROWS = 512


def _softmax_pv(scores, vals):
    m = jnp.max(scores[0], axis=-1, keepdims=True)
    for s in scores[1:]:
        m = jnp.maximum(m, jnp.max(s, axis=-1, keepdims=True))
    acc = None
    den = None
    for s, v in zip(scores, vals):
        p = jnp.exp(s - m)
        ps = jnp.sum(p, axis=-1, keepdims=True)
        pv = jnp.dot(p.astype(BF16), v, preferred_element_type=F32)
        acc = pv if acc is None else acc + pv
        den = ps if den is None else den + ps
    return acc / den


def _rope_lanes(x, cos, sin_signed, n):
    lane = lax.broadcasted_iota(jnp.int32, x.shape, 1)
    first = ((lane // n) % 2) == 0
    partner = jnp.where(first, pltpu.roll(x, LANE - n, 1), pltpu.roll(x, n, 1))
    return x * cos + partner * sin_signed


def _rms_rows(x, g):
    return x * lax.rsqrt(jnp.mean(x * x, axis=-1, keepdims=True) + RMS_EPS) * g


def _attn_refs(refs, n_main, use_rope, has_cache, n_cache, aliased):
    it = iter(refs)
    main = [next(it) for _ in range(n_main)]
    rope = (next(it), next(it)) if use_rope else (None, None)
    cache = [next(it) for _ in range(n_cache)] if has_cache else [None] * n_cache
    if aliased:
        next(it)
    return main, rope, cache, next(it)


def _gqa_fused_kernel(*refs, hp, kv_group, t, scale, n_rot, use_rope, has_cache, aliased):
    (q_ref, k_ref, v_ref, gq_ref, gk_ref), (cos_ref, sin_ref), (ck_ref, cv_ref), o_ref = _attn_refs(
        refs, 5, use_rope, has_cache, 2, aliased)
    tq = min(ATTN_Q_ROWS, t)
    n_kv = max(1, hp // kv_group)
    keys, vals = [], []
    for j in range(n_kv):
        sl = slice(j * LANE, (j + 1) * LANE)
        k = _rms_rows(k_ref[:, sl], gk_ref[...])
        if use_rope:
            k = _rope_lanes(k, cos_ref[...], sin_ref[...], n_rot)
        kj, vj = [k.astype(BF16)], [v_ref[:, sl].astype(BF16)]
        if has_cache:
            kj.insert(0, ck_ref[:, sl].astype(BF16))
            vj.insert(0, cv_ref[:, sl].astype(BF16))
        keys.append(kj)
        vals.append(vj)
    for hh in range(hp):
        j = hh // kv_group if n_kv > 1 else 0
        for r0 in range(0, t, tq):
            q = _rms_rows(q_ref[r0:r0 + tq, hh * LANE:(hh + 1) * LANE], gq_ref[...])
            if use_rope:
                q = _rope_lanes(q, cos_ref[r0:r0 + tq, :], sin_ref[r0:r0 + tq, :], n_rot)
            qb = q.astype(BF16)
            o = _softmax_pv([_dot_nt(qb, k) * scale for k in keys[j]], vals[j])
            o_ref[r0:r0 + tq, hh * LANE:(hh + 1) * LANE] = o.astype(o_ref.dtype)


def _diff_fused_kernel(*refs, hp, t, scale, dd, n_rot, use_rope, has_cache, aliased):
    (q_ref, k_ref, v_ref, lam_ref, g_ref), (cos_ref, sin_ref), (ck_ref, cv_ref), o_ref = _attn_refs(
        refs, 5, use_rope, has_cache, 2, aliased)
    tq = min(ATTN_Q_ROWS, t)
    for hh in range(hp):
        sl = slice(hh * LANE, (hh + 1) * LANE)
        k = k_ref[:, sl]
        if use_rope:
            k = _rope_lanes(k, cos_ref[...], sin_ref[...], n_rot)
        keys, vals = [k.astype(BF16)], [v_ref[:, sl].astype(BF16)]
        if has_cache:
            keys.insert(0, ck_ref[:, sl].astype(BF16))
            vals.insert(0, cv_ref[:, sl].astype(BF16))
        for r0 in range(0, t, tq):
            q = q_ref[r0:r0 + tq, sl]
            if use_rope:
                q = _rope_lanes(q, cos_ref[r0:r0 + tq, :], sin_ref[r0:r0 + tq, :], n_rot)
            lane = lax.broadcasted_iota(jnp.int32, q.shape, 1)
            q1 = jnp.where(lane < dd, q, 0.0).astype(BF16)
            q2 = jnp.where(lane < dd, 0.0, q).astype(BF16)
            a1 = _softmax_pv([_dot_nt(q1, kk) * scale for kk in keys], vals)
            a2 = _softmax_pv([_dot_nt(q2, kk) * scale for kk in keys], vals)
            d = a1 - lam_ref[...] * a2
            o_ref[r0:r0 + tq, sl] = _rms_rows(d, g_ref[...]).astype(o_ref.dtype)


def _mla_fused_kernel(*refs, hp, t, scale, rope_d, n_rot, use_rope, has_cache, aliased):
    (q_ref, kv_ref, kr_ref), (cos_ref, sin_ref), (kvc_ref, krc_ref), o_ref = _attn_refs(
        refs, 3, use_rope, has_cache, 2, aliased)
    tq = min(ATTN_Q_ROWS, t)
    hw = 2 * LANE
    kr = kr_ref[...]
    if use_rope:
        kr = _rope_lanes(kr, cos_ref[...], sin_ref[...], n_rot)
    krs = [kr[:, :rope_d].astype(BF16)]
    if has_cache:
        krs.insert(0, krc_ref[...].astype(BF16))
    for hh in range(hp):
        c0 = hh * hw
        kns, vals = [kv_ref[:, c0:c0 + LANE]], [kv_ref[:, c0 + LANE:c0 + hw]]
        if has_cache:
            kns.insert(0, kvc_ref[:, c0:c0 + LANE])
            vals.insert(0, kvc_ref[:, c0 + LANE:c0 + hw])
        for r0 in range(0, t, tq):
            qn = q_ref[r0:r0 + tq, c0:c0 + LANE].astype(BF16)
            qr = q_ref[r0:r0 + tq, c0 + LANE:c0 + hw]
            if use_rope:
                qr = _rope_lanes(qr, cos_ref[r0:r0 + tq, :], sin_ref[r0:r0 + tq, :], n_rot)
            qr = qr[:, :rope_d].astype(BF16)
            scores = [(_dot_nt(qn, kn) + _dot_nt(qr, kk)) * scale for kn, kk in zip(kns, krs)]
            o_ref[r0:r0 + tq, hh * LANE:(hh + 1) * LANE] = _softmax_pv(scores, vals).astype(o_ref.dtype)


def _attn_launch(body, n_rows, heads, hp, row0, b, t, main, rope, cache, prev, width_est):
    rb0 = row0 // t
    in_specs = [s for _, s in main]
    args = [a for a, _ in main]
    if rope is not None:
        in_specs += [pl.BlockSpec((t, LANE), lambda bi, hb: (0, 0))] * 2
        args += list(rope)
    in_specs += [s for _, s in cache]
    args += [a for a, _ in cache]
    aliases = {}
    if prev is not None:
        aliases = {len(args): 0}
        in_specs.append(pl.BlockSpec(memory_space=pl.ANY))
        args.append(prev)
    n_cached = cache[1][0].shape[2] if cache else 0
    est = 2 * width_est + 8 * min(ATTN_Q_ROWS, t) * (t + n_cached) * 4 + (4 << 20)
    return pl.pallas_call(
        body,
        grid=(b, heads // hp),
        in_specs=in_specs,
        out_specs=pl.BlockSpec((t, hp * LANE), lambda bi, hb: (rb0 + bi, hb)),
        out_shape=jax.ShapeDtypeStruct((n_rows, heads * LANE), BF16),
        input_output_aliases=aliases,
        compiler_params=_cparams(("arbitrary", "arbitrary"), est),
    )(*args)


def _cache_kernel(*refs, n_kv, rope_d, aliased):
    ckv_ref, g_ckv_ref, kr_ref, bk_ref, bv_ref, gk_ref, g_gk_ref, gv_ref = refs[:8]
    o_ckv, o_kr, o_dk, o_dv, o_gk, o_gv = refs[-6:]
    o_ckv[...] = _rms_rows(ckv_ref[...], g_ckv_ref[...])
    o_kr[...] = kr_ref[:, :rope_d]
    o_dk[...] = bk_ref[...]
    o_dv[...] = bv_ref[...]
    for j in range(n_kv):
        sl = slice(j * LANE, (j + 1) * LANE)
        o_gk[:, sl] = _rms_rows(gk_ref[:, sl], g_gk_ref[...])
    o_gv[...] = gv_ref[...]


def write_caches(h, col, l, depth, b, t, kv_rank, rope_d, unit, kvw, g_ckv, g_gk, prev):
    def hcols(c0, width):
        return pl.BlockSpec((t, width), lambda bi: (bi, c0 // width))

    def vec(n):
        return pl.BlockSpec((1, n), lambda bi: (0, 0))

    widths = (kv_rank, rope_d, unit, unit, kvw, kvw)
    in_specs = [hcols(col["ckv"], kv_rank), vec(kv_rank), hcols(col["kr"], LANE), hcols(col["bk"], unit),
                hcols(col["bv"], unit), hcols(col["gk"], kvw), vec(LANE), hcols(col["gv"], kvw)]
    args = [h, g_ckv, h, h, h, h, g_gk, h]
    aliases = {}
    if prev is not None:
        aliases = {len(args) + i: i for i in range(6)}
        in_specs += [pl.BlockSpec(memory_space=pl.ANY)] * 6
        args += list(prev)
    est = 4 * t * (kv_rank + LANE + 2 * unit + 2 * kvw) * 4 * 2 + (4 << 20)
    return pl.pallas_call(
        functools.partial(_cache_kernel, n_kv=kvw // LANE, rope_d=rope_d, aliased=prev is not None),
        grid=(b,),
        in_specs=in_specs,
        out_specs=[pl.BlockSpec((None, None, t, w), lambda bi: (bi, l, 0, 0)) for w in widths],
        out_shape=[jax.ShapeDtypeStruct((b, depth, t, w), F32) for w in widths],
        input_output_aliases=aliases,
        compiler_params=_cparams(("arbitrary",), est),
    )(*args)


def _hgrn_kernel(*refs, t, nb, chunk, sub, q_scale, has_s0, want_state, aliased):
    it = iter(refs)
    dq_ref, di_ref, df_ref, dfb_ref, dg_ref, tab_ref = (next(it) for _ in range(6))
    s0_ref = next(it) if has_s0 else None
    for _ in range(aliased):
        next(it)
    o_ref = next(it)
    st_out_ref = next(it) if want_state else None
    st_scr, o_scr, cum_scr, k_scr, q_scr = (next(it) for _ in range(5))

    n_chunks = t // chunk
    n_sub = chunk // sub
    half = sub // 2
    half_row = lax.broadcasted_iota(jnp.int32, (half, 1), 0)
    row = lax.broadcasted_iota(jnp.int32, (chunk, chunk), 0)
    col = lax.broadcasted_iota(jnp.int32, (chunk, chunk), 1)
    key_row = lax.broadcasted_iota(jnp.int32, (chunk, 1), 0)
    neg_inf = jnp.float32(-jnp.inf)
    tab = tab_ref[...]
    tri_f = jnp.where(row >= col, 1.0, 0.0).astype(F32)
    tri_b = jnp.where(row <= col, 1.0, 0.0).astype(F32)
    same_sub = (row // sub) == (col // sub)

    for bb in range(nb):
        for direction in range(2):
            if has_s0:
                st_scr[2 * bb + direction] = s0_ref[bb, direction, 0].T
            else:
                st_scr[2 * bb + direction] = jnp.zeros(st_scr.shape[1:], F32)

    def prepare(ci, carry):
        r0 = pl.multiple_of(ci * chunk, chunk)
        zq = dq_ref[pl.ds(r0, chunk), :]
        q_scr[pl.ds(r0, chunk), :] = zq * jax.nn.sigmoid(zq) * q_scale
        for direction, z_ref in enumerate((df_ref, dfb_ref)):
            log_lb = tab[3 * direction:3 * direction + 1, :]
            log_1m = tab[3 * direction + 1:3 * direction + 2, :]
            one_m = tab[3 * direction + 2:3 * direction + 3, :]
            z = z_ref[pl.ds(r0, chunk), :]
            e = jnp.exp(-jnp.abs(z))
            d1 = 1.0 + e
            t2 = log_1m + (jnp.minimum(z, 0.0) - jnp.log(d1))
            g = jnp.maximum(log_lb, t2) + jnp.log(1.0 + jnp.exp(-jnp.abs(log_lb - t2)))
            k_scr[direction, pl.ds(r0, chunk), :] = one_m * (jnp.where(z >= 0.0, e, 1.0) / d1)
            cum_scr[direction, pl.ds(r0, chunk), :] = jnp.dot(
                tri_b if direction else tri_f, g, precision=lax.Precision.HIGHEST, preferred_element_type=F32)
        return carry

    lax.fori_loop(0, nb * n_chunks, prepare, 0, unroll=2)

    def chain_step(ci, bb, direction):
        rev = direction == 1
        r0 = pl.multiple_of(bb * t + ((n_chunks - 1 - ci) if rev else ci) * chunk, chunk)
        q = q_scr[pl.ds(r0, chunk), :]
        k = k_scr[direction, pl.ds(r0, chunk), :]
        cum = cum_scr[direction, pl.ds(r0, chunk), :]
        v = di_ref[pl.ds(r0, chunk), :]
        edge = cum[0:1, :] if rev else cum[chunk - 1:chunk, :]
        slot = 2 * bb + direction
        st = st_scr[slot]
        vb = v.astype(BF16)
        inter = _dot_nt((q * jnp.exp(cum)).astype(BF16), st.astype(BF16))

        mid = [cum[i * sub + half:i * sub + half + 1, :] if rev else cum[i * sub + half - 1:i * sub + half, :]
               for i in range(n_sub)]
        base2 = jnp.concatenate([jnp.broadcast_to(m, (sub, LANE)) for m in mid], axis=0)
        late = (key_row % sub) >= half
        q_side, k_side = (~late, late) if rev else (late, ~late)
        q2 = (q * jnp.exp(jnp.where(q_side, cum - base2, neg_inf))).astype(BF16)
        k2 = (k * jnp.exp(jnp.where(k_side, base2 - cum, neg_inf))).astype(BF16)
        att2 = jnp.where(same_sub, _dot_nt(q2, k2), 0.0)
        near = inter + jnp.dot(att2.astype(BF16), vb, preferred_element_type=F32)

        outs = []
        for i in range(n_sub):
            lo = i * sub
            acc = near[lo:lo + sub]
            if (i < n_sub - 1) if rev else (i > 0):
                base = cum[lo + sub:lo + sub + 1, :] if rev else cum[lo - 1:lo, :]
                seen = (key_row >= lo + sub) if rev else (key_row < lo)
                qn = (q[lo:lo + sub] * jnp.exp(cum[lo:lo + sub] - base)).astype(BF16)
                kn = (k * jnp.exp(jnp.where(seen, base - cum, neg_inf))).astype(BF16)
                att = _dot_nt(qn, kn)
                acc = acc + jnp.dot(att.astype(BF16), vb, preferred_element_type=F32)
            for hb in range(sub // half):
                lo8 = lo + hb * half
                q8 = q[lo8:lo8 + half]
                cum8 = cum[lo8:lo8 + half]
                acc8 = acc[hb * half:(hb + 1) * half]
                for s in range(half):
                    r = lo8 + s
                    live = (half_row <= s) if rev else (half_row >= s)
                    w = jnp.exp(jnp.where(live, cum8 - cum_scr[direction, pl.ds(r0 + r, 1), :], neg_inf))
                    colv = jnp.sum(q8 * k_scr[direction, pl.ds(r0 + r, 1), :] * w, axis=-1, keepdims=True)
                    acc8 = acc8 + colv * di_ref[pl.ds(r0 + r, 1), :]
                outs.append(acc8)
        o_scr[direction, pl.ds(r0, chunk), :] = jnp.concatenate(outs, axis=0)
        kl = (k * jnp.exp(edge - cum)).astype(BF16)
        st_scr[2 * bb + direction] = st * jnp.exp(edge) + _dot_tn(vb, kl)

    def body(ci, carry):
        for bb in range(nb):
            for direction in range(2):
                chain_step(ci, bb, direction)
        return carry

    lax.fori_loop(0, n_chunks, body, 0)

    tot = o_scr[0] + o_scr[1]
    zg = dg_ref[...]
    y = tot * lax.rsqrt(jnp.mean(tot * tot, axis=-1, keepdims=True) + RMS_EPS) * tab[6:7, :]
    o_ref[...] = (y * (zg * jax.nn.sigmoid(zg))).astype(o_ref.dtype)
    if want_state:
        for bb in range(nb):
            for direction in range(2):
                st_out_ref[bb, direction, 0] = st_scr[2 * bb + direction].T


def hgrn_mix(h, row0, b, t, heads, dk, dv, cols, tab, s0, state_into, prev):
    assert dk == LANE and dv == LANE
    nb = HGRN_BATCHES if (b % HGRN_BATCHES == 0 and row0 % (HGRN_BATCHES * t) == 0) else 1
    assert row0 % (nb * t) == 0 and all(cc % LANE == 0 for cc in cols)
    chunk = min(HGRN_CHUNK, t)
    sub = min(HGRN_SUB, chunk)
    rb0 = row0 // (nb * t)

    def col_spec(c0):
        return pl.BlockSpec((nb * t, LANE), lambda bi, hh: (rb0 + bi, c0 // LANE + hh))

    st_spec = pl.BlockSpec((nb, 2, 1, dk, dv), lambda bi, hh: (bi, 0, hh, 0, 0))
    in_specs = [col_spec(cc) for cc in cols] + [pl.BlockSpec((8, LANE), lambda bi, hh: (0, hh))]
    args = [h] * len(cols) + [tab]
    if s0 is not None:
        in_specs.append(st_spec)
        args.append(s0)
    want_state = state_into is not None
    aliases = {}
    if prev is not None:
        aliases[len(args)] = 0
        in_specs.append(pl.BlockSpec(memory_space=pl.ANY))
        args.append(prev)
    if want_state and state_into[2] is not None:
        aliases[len(args)] = 1
        in_specs.append(pl.BlockSpec(memory_space=pl.ANY))
        args.append(state_into[2])
    out_specs = [pl.BlockSpec((nb * t, LANE), lambda bi, hh: (rb0 + bi, hh))]
    out_shape = [jax.ShapeDtypeStruct((h.shape[0], heads * dv), BF16)]
    if want_state:
        layer, depth = state_into[0], state_into[1]
        out_specs.append(pl.BlockSpec((nb, None, 2, 1, dk, dv), lambda bi, hh: (bi, layer, 0, hh, 0, 0)))
        out_shape.append(jax.ShapeDtypeStruct((b, depth, 2, heads, dk, dv), F32))
    est = 2 * nb * t * LANE * (5 * 4 + 2) + 7 * nb * t * LANE * 4 + (8 * nb + 6) * dk * dv * 4 + (4 << 20)
    res = pl.pallas_call(
        functools.partial(_hgrn_kernel, t=t, nb=nb, chunk=chunk, sub=sub, q_scale=dk ** -0.5,
                          has_s0=s0 is not None, want_state=want_state, aliased=len(aliases)),
        grid=(b // nb, heads),
        in_specs=in_specs,
        out_specs=out_specs,
        out_shape=out_shape,
        input_output_aliases=aliases,
        scratch_shapes=[pltpu.VMEM((2 * nb, dv, dk), F32), pltpu.VMEM((2, nb * t, LANE), F32),
                        pltpu.VMEM((2, nb * t, LANE), F32), pltpu.VMEM((2, nb * t, LANE), F32),
                        pltpu.VMEM((nb * t, LANE), F32)],
        compiler_params=_cparams(("arbitrary", "arbitrary"), est),
    )(*args)
    return (res[0], res[1]) if want_state else (res[0], None)


def _rope_lane_tables(t, d):
    n = d // 4
    inv = ROPE_THETA ** (-jnp.arange(n, dtype=F32) / n)
    pos = jnp.arange(t)
    rows = (pos // GRID_W).astype(F32)
    cols = (pos % GRID_W).astype(F32)
    ar = rows[:, None] * inv[None, :]
    ac = cols[:, None] * inv[None, :]
    cos = jnp.concatenate([jnp.cos(ar), jnp.cos(ar), jnp.cos(ac), jnp.cos(ac)], axis=-1)
    sin = jnp.concatenate([-jnp.sin(ar), jnp.sin(ar), -jnp.sin(ac), jnp.sin(ac)], axis=-1)
    reps = LANE // d
    return jnp.tile(cos, (1, reps)), jnp.tile(sin, (1, reps))


def kernel(x_prompt, x_sample, cache_mla_ckv, cache_mla_krope, cache_diff_k, cache_diff_v, cache_gqa_k, cache_gqa_v, state_hgrn, c, c_ctx, w_in, w_out, mla_g_cq, mla_g_ckv, mla_w_uq, mla_w_ukv, diff_lambda, diff_subln_g, gqa_g_q, gqa_g_k, hgrn_gamma, hgrn_g_norm, ada_w, ada_b, ln_g, ln_b, ffn_w1, ffn_w3, ffn_w2, moe_router, moe_w1, moe_w3, moe_w2):
    bc, tc, d = x_prompt.shape
    bl, tl, _ = x_sample.shape
    depth = w_in.shape[0]
    q_rank = mla_g_cq.shape[1]
    kv_rank = mla_g_ckv.shape[1]
    rope_d = cache_mla_krope.shape[-1]
    heads = cache_diff_k.shape[3]
    dd = cache_diff_k.shape[4] // 2
    kv_heads = cache_gqa_k.shape[3]
    gd = cache_gqa_k.shape[4]
    hk = state_hgrn.shape[4]
    hv = state_hgrn.shape[5]
    n_exp = moe_w1.shape[1]
    alpha = (2 * depth) ** 0.25
    n_ctx = bc * tc
    n_lat = bl * tl
    tok = _Tokens(n_ctx, tl, bl, d)
    mla_hw = 2 * LANE
    assert MLA_NOPE_DIM + rope_d <= mla_hw and MLA_NOPE_DIM + MLA_V_DIM == mla_hw

    unit = heads * LANE
    kvw = kv_heads * gd
    assert 2 * dd == LANE and gd == LANE and hk == LANE and hv == LANE
    names = ("cq", "ckv", "kr", "bq", "bk", "bv", "gq", "gk", "gv", "dq", "di", "df", "dfb", "dg")
    widths = (q_rank, kv_rank, rope_d, unit, unit, unit, unit, kvw, kvw, unit, unit, unit, unit, unit)
    assert sum(widths) == w_in.shape[2]
    src, o = {}, 0
    for nm, w in zip(names, widths):
        src[nm] = (o, w)
        o += w
    col, pieces, o = {}, [], 0

    def place(nm, pad_to=None):
        nonlocal o
        if nm is not None:
            s0, w = src[nm]
            col[nm] = o
            pieces.append(w_in[:, :, s0:s0 + w])
            o += w
        if pad_to is not None and o % pad_to:
            z = pad_to - o % pad_to
            pieces.append(jnp.zeros((depth, d, z), w_in.dtype))
            o += z

    place("cq")
    place("gk", pad_to=unit)
    place("ckv")
    place("gv")
    place("kr", pad_to=LANE)
    place(None, pad_to=unit)
    for nm in ("bq", "bk", "bv", "gq", "dq", "di", "df", "dfb", "dg"):
        place(nm)
    w_in_b = jnp.concatenate(pieces, axis=-1).astype(BF16)
    assert q_rank <= unit and col["gk"] % kvw == 0 and col["gv"] % kvw == 0 and col["ckv"] % kv_rank == 0
    w_uq_b = jnp.pad(mla_w_uq.reshape(depth, q_rank, heads, MLA_NOPE_DIM + rope_d),
                     ((0, 0), (0, 0), (0, 0), (0, mla_hw - MLA_NOPE_DIM - rope_d))
                     ).reshape(depth, q_rank, heads * mla_hw).astype(BF16)
    w_ukv_b = mla_w_ukv.astype(BF16)
    ffn_w2_b = ffn_w2.astype(BF16)
    e_ff = moe_w1.shape[-1]
    n_moe = moe_w1.shape[0]
    moe_w1_r = moe_w1.reshape(n_moe * n_exp, d, e_ff)
    moe_w3_r = moe_w3.reshape(n_moe * n_exp, d, e_ff)
    moe_w2_b = moe_w2.astype(BF16).reshape(n_moe * n_exp, e_ff, d)
    router_b = jnp.pad(moe_router, ((0, 0), (0, 0), (0, LANE - n_exp))).astype(BF16)

    cond = jnp.concatenate([c_ctx[None, :], c], axis=0)
    n_cond = cond.shape[0]
    cond_p = jnp.pad(jax.nn.silu(cond), ((0, -n_cond % 16), (0, 0))).astype(BF16)
    mods = []
    for l in range(depth):
        ml = matmul_few_rows(cond_p, ada_w, l)[:n_cond] + ada_b[l][None, :]
        mods.append(ml.reshape(n_cond, 6, d).transpose(1, 0, 2))
    mod = jnp.concatenate(mods, axis=0).reshape(depth * 6, n_cond, 1, d)
    ln_gb = jnp.concatenate([ln_g.reshape(depth * 2, 1, d), ln_b.reshape(depth * 2, 1, d)], axis=0)

    def lower_bounds(gamma):
        cs = jnp.cumsum(jax.nn.softmax(gamma.astype(F32), axis=0), axis=0)
        return cs - cs[0:1]

    lb_f = lower_bounds(hgrn_gamma[0])
    lb_b = lower_bounds(hgrn_gamma[1])
    n_rot_small, n_rot_big = dd // 4, gd // 4
    assert rope_d == dd
    rope_small = _rope_lane_tables(tl, dd)
    rope_big = _rope_lane_tables(tl, gd)
    n_tok = n_ctx + n_lat
    past = cache_mla_ckv.shape[2]
    cache_dk = cache_diff_k.reshape(bl, depth, past, unit)
    cache_dv = cache_diff_v.reshape(bl, depth, past, unit)
    cache_gk = cache_gqa_k.reshape(bl, depth, past, kvw)
    cache_gv = cache_gqa_v.reshape(bl, depth, past, kvw)
    hgrn_cols = [col[nm] for nm in ("dq", "di", "df", "dfb", "dg")]

    def mixers(h, q_all, kv_new, kv_cache, l, lat, prevs, new_states=None):
        row0, b, t = (n_ctx, bl, tl) if lat else (0, bc, tc)
        hp = 1 if lat else heads
        rb0 = row0 // t
        flags = dict(use_rope=lat, has_cache=lat, aliased=prevs is not None)
        pa, pb, pc, pd = prevs if prevs is not None else (None,) * 4

        def hcols(c0, width):
            return pl.BlockSpec((t, width), lambda bi, hb: (rb0 + bi, c0 // width + hb))

        def vec(n):
            return pl.BlockSpec((1, n), lambda bi, hb: (0, 0))

        def cache_spec(width, per_head):
            if per_head:
                return pl.BlockSpec((None, None, past, width), lambda bi, hb: (bi, l, 0, hb))
            return pl.BlockSpec((None, None, past, width), lambda bi, hb: (bi, l, 0, 0))

        wide = pl.BlockSpec((t, hp * mla_hw), lambda bi, hb: (rb0 + bi, hb))
        main = [(q_all, wide), (kv_new, wide), (h, pl.BlockSpec((t, LANE), lambda bi, hb: (rb0 + bi, col["kr"] // LANE)))]
        cache = []
        if lat:
            cache = [(kv_cache, pl.BlockSpec((past, hp * mla_hw), lambda bi, hb: (bi, hb))),
                     (cache_mla_krope, cache_spec(rope_d, False))]
        o_a = _attn_launch(
            functools.partial(_mla_fused_kernel, hp=hp, t=t, scale=(MLA_NOPE_DIM + rope_d) ** -0.5, rope_d=rope_d,
                              n_rot=n_rot_small, **flags),
            n_tok, heads, hp, row0, b, t, main, rope_small if lat else None, cache, pa,
            t * hp * mla_hw * 6 + past * hp * mla_hw * 2)

        lam_p = diff_lambda[l].astype(F32)
        lam_init = 0.8 - 0.6 * math.exp(-0.3 * l)
        lam = jnp.exp(jnp.sum(lam_p[0] * lam_p[1])) - jnp.exp(jnp.sum(lam_p[2] * lam_p[3])) + lam_init
        lam_v = jnp.full((1, LANE), lam, F32)
        g_v = (diff_subln_g[l] * (1.0 - lam_init)).reshape(1, LANE)
        w = hp * LANE
        main = [(h, hcols(col["bq"], w)), (h, hcols(col["bk"], w)), (h, hcols(col["bv"], w)),
                (lam_v, vec(LANE)), (g_v, vec(LANE))]
        cache = [(cache_dk, cache_spec(w, True)), (cache_dv, cache_spec(w, True))] if lat else []
        o_b = _attn_launch(
            functools.partial(_diff_fused_kernel, hp=hp, t=t, scale=dd ** -0.5, dd=dd, n_rot=n_rot_small, **flags),
            n_tok, heads, hp, row0, b, t, main, rope_small if lat else None, cache, pb,
            (3 * t + 2 * past) * w * 4)

        kv_group = heads // kv_heads
        if lat:
            kspec = pl.BlockSpec((t, LANE), lambda bi, hb: (rb0 + bi, col["gk"] // LANE + hb // kv_group))
            vspec = pl.BlockSpec((t, LANE), lambda bi, hb: (rb0 + bi, col["gv"] // LANE + hb // kv_group))
            cspec = pl.BlockSpec((None, None, past, LANE), lambda bi, hb: (bi, l, 0, hb // kv_group))
            cache = [(cache_gk, cspec), (cache_gv, cspec)]
        else:
            kspec = pl.BlockSpec((t, kvw), lambda bi, hb: (rb0 + bi, col["gk"] // kvw))
            vspec = pl.BlockSpec((t, kvw), lambda bi, hb: (rb0 + bi, col["gv"] // kvw))
            cache = []
        main = [(h, hcols(col["gq"], w)), (h, kspec), (h, vspec),
                (gqa_g_q[l].reshape(1, gd), vec(gd)), (gqa_g_k[l].reshape(1, gd), vec(gd))]
        o_c = _attn_launch(
            functools.partial(_gqa_fused_kernel, hp=hp, kv_group=kv_group, t=t, scale=gd ** -0.5, n_rot=n_rot_big,
                              **flags),
            n_tok, heads, hp, row0, b, t, main, rope_big if lat else None, cache, pc,
            (t + 2 * t + 2 * past) * w * 4)

        tab = jnp.stack([jnp.log(lb_f[l]), jnp.log1p(-lb_f[l]), 1.0 - lb_f[l],
                         jnp.log(lb_b[l]), jnp.log1p(-lb_b[l]), 1.0 - lb_b[l],
                         jnp.tile(hgrn_g_norm[l], heads), jnp.zeros((heads * hk,), F32)], axis=0)
        o_d, states = hgrn_mix(h, row0, b, t, heads, hk, hv, hgrn_cols, tab, state_hgrn[:, l] if lat else None,
                               None if lat else (l, depth, new_states), pd)
        return (o_a, o_b, o_c, o_d), states

    x = jnp.concatenate([x_prompt.reshape(n_ctx, d), x_sample.reshape(n_lat, d)], axis=0)
    u = modulate(tok, x, mod, 1, 0)
    caches, new_states = None, None
    for l in range(depth):
        h = matmul(u, w_in_b, F32, layer=l)
        q_all = rms_matmul(h, col["cq"], unit, mla_g_cq[l].reshape(1, q_rank), w_uq_b, l, F32)
        kv_new = rms_matmul(h, col["ckv"], kv_rank, mla_g_ckv[l].reshape(1, kv_rank), w_ukv_b, l, BF16)
        kv_cache = matmul(cache_mla_ckv[:, l].reshape(bl * past, kv_rank).astype(BF16), w_ukv_b, BF16, layer=l)
        parts, new_states = mixers(h, q_all, kv_new, kv_cache, l, False, None, new_states)
        parts, _ = mixers(h, q_all, kv_new, kv_cache, l, True, parts)
        caches = write_caches(h, col, l, depth, bc, tc, kv_rank, rope_d, unit, kvw, mla_g_ckv[l].reshape(1, kv_rank),
                              gqa_g_k[l].reshape(1, gd), caches)
        mix = out_proj(parts, w_out, l)
        x, u = residual_ln(tok, x, mix, mod, l * 6 + 2, ln_gb, 2 * l, 2 * depth + 2 * l, alpha,
                           l * 6 + 4, l * 6 + 3)
        j = l // 2
        if l % 2 == 0:
            f = down_proj(swiglu_up(u, ffn_w1, ffn_w3, j), ffn_w2_b[j])
        else:
            logits = matmul(u, router_b, F32, layer=j)[:, :n_exp]
            f = moe_experts(u, logits, moe_w1_r, moe_w3_r, moe_w2_b, j, n_exp)
        if l + 1 < depth:
            x, u = residual_ln(tok, x, f, mod, l * 6 + 5, ln_gb, 2 * l + 1, 2 * depth + 2 * l + 1, alpha,
                               (l + 1) * 6 + 1, (l + 1) * 6 + 0)
        else:
            x, _ = residual_ln(tok, x, f, mod, l * 6 + 5, ln_gb, 2 * l + 1, 2 * depth + 2 * l + 1, alpha)

    y_prompt = x[:n_ctx].reshape(bc, tc, d)
    y_sample = x[n_ctx:].reshape(bl, tl, d)
    new_ckv, new_kr, new_dk, new_dv, new_gk, new_gv = caches
    return (y_prompt, y_sample, new_ckv, new_kr,
            new_dk.reshape(bc, depth, tc, heads, 2 * dd), new_dv.reshape(bc, depth, tc, heads, 2 * dd),
            new_gk.reshape(bc, depth, tc, kv_heads, gd), new_gv.reshape(bc, depth, tc, kv_heads, gd), new_states)
```

```python
import functools
import math

import jax
import jax.numpy as jnp
from jax import lax
from jax.experimental import pallas as pl
from jax.experimental.pallas import tpu as pltpu

F32 = jnp.float32
BF16 = jnp.bfloat16

GRID_W = 64
ROPE_THETA = 10000.0
LN_EPS = 1e-5
RMS_EPS = 1e-6
MLA_NOPE_DIM = 128
MLA_V_DIM = 128
TOP_K = 2

LANE = 128
V7X_VMEM_BYTES = 64 * 1024 * 1024
VMEM_CAP = V7X_VMEM_BYTES - 6 * 1024 * 1024
HGRN_CHUNK = 64
HGRN_SUB = 16
HGRN_BATCHES = 4
MOE_ROWS = 512


def _cparams(sem, est_bytes):
    limit = int(min(VMEM_CAP, max(32 * 1024 * 1024, est_bytes * 5 // 4 + (4 << 20))))
    return pltpu.CompilerParams(dimension_semantics=sem, vmem_limit_bytes=limit)


def _pick_tile(n, target):
    if n <= target:
        return n
    t = (target // LANE) * LANE
    while t >= LANE:
        if n % t == 0:
            return t
        t -= LANE
    return n


def _pick_rows(n, target):
    if n <= target:
        return n
    t = target
    while t >= 8:
        if n % t == 0:
            return t
        t //= 2
    return n


def _dot_nt(a, b):
    return lax.dot_general(a, b, (((1,), (1,)), ((), ())), preferred_element_type=F32)


def _dot_tn(a, b):
    return lax.dot_general(a, b, (((0,), (0,)), ((), ())), preferred_element_type=F32)


def _mm_kernel(x_ref, w_ref, o_ref):
    o_ref[...] = jnp.dot(x_ref[...], w_ref[...], preferred_element_type=F32).astype(o_ref.dtype)


def matmul(x, w, out_dtype, layer=None, tm=1024, tn=1024):
    m, k = x.shape
    n = w.shape[-1]
    tm = _pick_rows(m, tm)
    tn = _pick_tile(n, tn)
    if w.ndim == 3:
        w_spec = pl.BlockSpec((None, k, tn), lambda j, i: (layer, 0, j))
    else:
        w_spec = pl.BlockSpec((k, tn), lambda j, i: (0, j))
    est = 2 * (tm * k * 2 + k * tn * 2 + tm * tn * jnp.dtype(out_dtype).itemsize) + tm * tn * 4
    return pl.pallas_call(
        _mm_kernel,
        grid=(n // tn, m // tm),
        in_specs=[pl.BlockSpec((tm, k), lambda j, i: (i, 0)), w_spec],
        out_specs=pl.BlockSpec((tm, tn), lambda j, i: (i, j)),
        out_shape=jax.ShapeDtypeStruct((m, n), out_dtype),
        compiler_params=_cparams(("arbitrary", "arbitrary"), est),
    )(x, w)


def _mm_castw_kernel(x_ref, w_ref, o_ref):
    o_ref[...] = jnp.dot(x_ref[...], w_ref[...].astype(BF16), preferred_element_type=F32).astype(o_ref.dtype)


def matmul_few_rows(x, w, layer, tn=1024):
    m, k = x.shape
    n = w.shape[-1]
    tn = _pick_tile(n, tn)
    est = 2 * (m * k * 2 + k * tn * 4 + m * tn * 4) + k * tn * 2
    return pl.pallas_call(
        _mm_castw_kernel,
        grid=(n // tn,),
        in_specs=[pl.BlockSpec((m, k), lambda j: (0, 0)), pl.BlockSpec((None, k, tn), lambda j: (layer, 0, j))],
        out_specs=pl.BlockSpec((m, tn), lambda j: (0, j)),
        out_shape=jax.ShapeDtypeStruct((m, n), F32),
        compiler_params=_cparams(("arbitrary",), est),
    )(x, w)


def _rms_mm_kernel(x_ref, g_ref, w_ref, o_ref):
    x = x_ref[:, :w_ref.shape[0]]
    xn = x * lax.rsqrt(jnp.mean(x * x, axis=-1, keepdims=True) + RMS_EPS) * g_ref[...]
    o_ref[...] = jnp.dot(xn.astype(BF16), w_ref[...], preferred_element_type=F32).astype(o_ref.dtype)


def rms_matmul(h, col0, kb, gain, w, layer, out_dtype, tm=1024, tn=1024):
    m = h.shape[0]
    k, n = w.shape[1], w.shape[2]
    assert col0 % kb == 0 and kb >= k
    tm = _pick_rows(m, tm)
    tn = _pick_tile(n, tn)
    est = 2 * (tm * kb * 4 + k * tn * 2 + tm * tn * 4) + 3 * tm * k * 4 + tm * tn * 4
    return pl.pallas_call(
        _rms_mm_kernel,
        grid=(n // tn, m // tm),
        in_specs=[pl.BlockSpec((tm, kb), lambda j, i: (i, col0 // kb)),
                  pl.BlockSpec((1, k), lambda j, i: (0, 0)),
                  pl.BlockSpec((None, k, tn), lambda j, i: (layer, 0, j))],
        out_specs=pl.BlockSpec((tm, tn), lambda j, i: (i, j)),
        out_shape=jax.ShapeDtypeStruct((m, n), out_dtype),
        compiler_params=_cparams(("arbitrary", "arbitrary"), est),
    )(h, gain, w)


def _out_proj_kernel(a_ref, b_ref, c_ref, d_ref, w_ref, o_ref, wb_scr):
    @pl.when(pl.program_id(1) == 0)
    def _():
        wb_scr[...] = w_ref[...].astype(BF16)

    kq = a_ref.shape[1]
    acc = jnp.dot(a_ref[...], wb_scr[0:kq, :], preferred_element_type=F32)
    for g, r in enumerate((b_ref, c_ref, d_ref), start=1):
        acc = acc + jnp.dot(r[...], wb_scr[g * kq:(g + 1) * kq, :], preferred_element_type=F32)
    o_ref[...] = acc


def out_proj(parts, w, layer, tm=1024, tn=512):
    m, kq = parts[0].shape
    k, n = w.shape[1], w.shape[2]
    assert len(parts) == 4 and 4 * kq == k
    tm = _pick_rows(m, tm)
    tn = _pick_tile(n, tn)
    est = 2 * (4 * tm * kq * 2 + k * tn * 4 + tm * tn * 4) + k * tn * 2 + 2 * tm * tn * 4
    part_spec = pl.BlockSpec((tm, kq), lambda j, i: (i, 0))
    return pl.pallas_call(
        _out_proj_kernel,
        grid=(n // tn, m // tm),
        in_specs=[part_spec] * 4 + [pl.BlockSpec((None, k, tn), lambda j, i: (layer, 0, j))],
        out_specs=pl.BlockSpec((tm, tn), lambda j, i: (i, j)),
        out_shape=jax.ShapeDtypeStruct((m, n), F32),
        scratch_shapes=[pltpu.VMEM((k, tn), BF16)],
        compiler_params=_cparams(("arbitrary", "arbitrary"), est),
    )(*parts, w)


def _swiglu_kernel(x_ref, w1_ref, w3_ref, o_ref, w1b_scr, w3b_scr):
    @pl.when(pl.program_id(1) == 0)
    def _():
        w1b_scr[...] = w1_ref[...].astype(BF16)
        w3b_scr[...] = w3_ref[...].astype(BF16)

    x = x_ref[...]
    a = jnp.dot(x, w1b_scr[...], preferred_element_type=F32)
    b = jnp.dot(x, w3b_scr[...], preferred_element_type=F32)
    o_ref[...] = (a * jax.nn.sigmoid(a) * b).astype(o_ref.dtype)


def swiglu_up(x, w1, w3, layer, tm=512, tn=512):
    m, k = x.shape
    n = w1.shape[-1]
    tm = _pick_rows(m, tm)
    tn = _pick_tile(n, tn)
    w_spec = pl.BlockSpec((None, k, tn), lambda j, i: (layer, 0, j))
    est = 2 * (tm * k * 2 + 2 * k * tn * 4 + tm * tn * 2) + 2 * k * tn * 2 + 3 * tm * tn * 4
    return pl.pallas_call(
        _swiglu_kernel,
        grid=(n // tn, m // tm),
        in_specs=[pl.BlockSpec((tm, k), lambda j, i: (i, 0)), w_spec, w_spec],
        out_specs=pl.BlockSpec((tm, tn), lambda j, i: (i, j)),
        out_shape=jax.ShapeDtypeStruct((m, n), BF16),
        scratch_shapes=[pltpu.VMEM((k, tn), BF16), pltpu.VMEM((k, tn), BF16)],
        compiler_params=_cparams(("arbitrary", "arbitrary"), est),
    )(x, w1, w3)


def _down_kernel(x_ref, w_ref, o_ref):
    p = jnp.dot(x_ref[...], w_ref[...], preferred_element_type=F32)

    @pl.when(pl.program_id(2) == 0)
    def _():
        o_ref[...] = p

    @pl.when(pl.program_id(2) > 0)
    def _():
        o_ref[...] += p


def down_proj(x, w, layer, tm=2048, tn=1024, tk=1792):
    m, k = x.shape
    n = w.shape[-1]
    tm = _pick_rows(m, tm)
    tn = _pick_tile(n, tn)
    tk = _pick_tile(k, tk)
    est = 2 * (tm * tk * 2 + tk * tn * 2 + tm * tn * 4) + tm * tn * 4
    return pl.pallas_call(
        _down_kernel,
        grid=(m // tm, n // tn, k // tk),
        in_specs=[pl.BlockSpec((tm, tk), lambda i, j, s: (i, s)),
                  pl.BlockSpec((None, tk, tn), lambda i, j, s: (layer, s, j))],
        out_specs=pl.BlockSpec((tm, tn), lambda i, j, s: (i, j)),
        out_shape=jax.ShapeDtypeStruct((m, n), F32),
        compiler_params=_cparams(("arbitrary", "arbitrary", "arbitrary"), est),
    )(x, w)


def _moe_up_kernel(rt_ref, nt_ref, ex_ref, ok_ref, new_ref, x_ref, w1_ref, w3_ref, o_ref, w1b_scr, w3b_scr):
    @pl.when(new_ref[pl.program_id(0)] == 1)
    def _():
        w1b_scr[...] = w1_ref[...].astype(BF16)
        w3b_scr[...] = w3_ref[...].astype(BF16)

    @pl.when(ok_ref[pl.program_id(0)] == 1)
    def _():
        x = x_ref[...]
        a = jnp.dot(x, w1b_scr[...], preferred_element_type=F32)
        b = jnp.dot(x, w3b_scr[...], preferred_element_type=F32)
        o_ref[...] = (a * jax.nn.sigmoid(a) * b).astype(o_ref.dtype)


def _moe_down_kernel(rt_ref, nt_ref, ex_ref, ok_ref, x_ref, w_ref, g_ref, o_ref):
    @pl.when(ok_ref[pl.program_id(0)] == 1)
    def _():
        o_ref[...] = jnp.dot(x_ref[...], w_ref[...], preferred_element_type=F32) * g_ref[...]


def _moe_items(tiles_e, tile0_e, nj, n_tiles):
    n_items = n_tiles * nj
    per_e = tiles_e * nj
    end_e = jnp.cumsum(per_e)
    total = end_e[-1]
    pos = jnp.arange(n_items, dtype=jnp.int32)
    idx = jnp.minimum(pos, total - 1)
    ex = jnp.searchsorted(end_e, idx, side="right").astype(jnp.int32)
    local = idx - (end_e - per_e)[ex]
    te = jnp.maximum(tiles_e[ex], 1)
    rt = tile0_e[ex] + local % te
    nt = local // te
    ok = pos < total
    new = ok & (local % te == 0)
    return rt.astype(jnp.int32), nt.astype(jnp.int32), ex, ok.astype(jnp.int32), new.astype(jnp.int32)


def moe_experts(u, logits, w1, w3, w2, group, n_exp):
    n, d = u.shape
    f = w1.shape[-1]
    tm = min(MOE_ROWS, n)
    top_v, top_i = lax.top_k(logits, TOP_K)
    wts = jax.nn.softmax(top_v, axis=-1)
    n_pairs = n * TOP_K
    n_rows = -(-n_pairs // tm) * tm + n_exp * tm
    n_tiles = n_rows // tm

    e_flat = top_i.reshape(-1).astype(jnp.int32)
    order = jnp.argsort(e_flat, stable=True)
    cnt = jnp.sum(jax.nn.one_hot(e_flat, n_exp, dtype=jnp.int32), axis=0)
    tiles_e = (cnt + tm - 1) // tm
    tile0_e = jnp.cumsum(tiles_e) - tiles_e
    start_e = jnp.cumsum(cnt) - cnt
    e_sorted = e_flat[order]
    dest_sorted = tile0_e[e_sorted] * tm + jnp.arange(n_pairs, dtype=jnp.int32) - start_e[e_sorted]
    dest = dest_sorted[jnp.argsort(order)]
    rows = jnp.arange(n_rows, dtype=jnp.int32)
    e_row = jnp.minimum(jnp.searchsorted((tile0_e + tiles_e) * tm, rows, side="right"), n_exp - 1).astype(jnp.int32)
    r_in = rows - tile0_e[e_row] * tm
    live = (r_in >= 0) & (r_in < cnt[e_row])
    pair = order[jnp.clip(start_e[e_row] + r_in, 0, n_pairs - 1)]
    row_token = jnp.where(live, pair // TOP_K, 0).astype(jnp.int32)
    row_gate = jnp.where(live, wts.reshape(-1)[pair], 0.0)

    xg = jnp.take(u, row_token, axis=0, mode="clip")

    tn = _pick_tile(f, 512)
    items = _moe_items(tiles_e, tile0_e, f // tn, n_tiles)
    w_spec = pl.BlockSpec((None, d, tn), lambda i, rt, nt, ex, ok, new: (group * n_exp + ex[i], 0, nt[i]))
    est = 2 * (tm * d * 2 + 2 * d * tn * 4 + tm * tn * 2) + 2 * d * tn * 2 + 3 * tm * tn * 4
    act = pl.pallas_call(
        _moe_up_kernel,
        grid_spec=pltpu.PrefetchScalarGridSpec(
            num_scalar_prefetch=5, grid=(n_tiles * (f // tn),),
            in_specs=[pl.BlockSpec((tm, d), lambda i, rt, nt, ex, ok, new: (rt[i], 0)), w_spec, w_spec],
            out_specs=pl.BlockSpec((tm, tn), lambda i, rt, nt, ex, ok, new: (rt[i], nt[i])),
            scratch_shapes=[pltpu.VMEM((d, tn), BF16), pltpu.VMEM((d, tn), BF16)]),
        out_shape=jax.ShapeDtypeStruct((n_rows, f), BF16),
        compiler_params=_cparams(("arbitrary",), est),
    )(*items, xg, w1, w3)

    tn2 = _pick_tile(d, 512)
    items2 = _moe_items(tiles_e, tile0_e, d // tn2, n_tiles)[:4]
    est = 2 * (tm * f * 2 + f * tn2 * 2 + tm * tn2 * 4 + tm * LANE * 4) + tm * tn2 * 4
    y = pl.pallas_call(
        _moe_down_kernel,
        grid_spec=pltpu.PrefetchScalarGridSpec(
            num_scalar_prefetch=4, grid=(n_tiles * (d // tn2),),
            in_specs=[pl.BlockSpec((tm, f), lambda i, rt, nt, ex, ok: (rt[i], 0)),
                      pl.BlockSpec((None, f, tn2), lambda i, rt, nt, ex, ok: (group * n_exp + ex[i], 0, nt[i])),
                      pl.BlockSpec((tm, 1), lambda i, rt, nt, ex, ok: (rt[i], 0))],
            out_specs=pl.BlockSpec((tm, tn2), lambda i, rt, nt, ex, ok: (rt[i], nt[i]))),
        out_shape=jax.ShapeDtypeStruct((n_rows, d), F32),
        compiler_params=_cparams(("arbitrary",), est),
    )(*items2, act, w2, row_gate[:, None])

    dest = dest.reshape(n, TOP_K)
    return jnp.take(y, dest[:, 0], axis=0, mode="clip") + jnp.take(y, dest[:, 1], axis=0, mode="clip")


def _modulate_kernel(x_ref, scale_ref, shift_ref, u_ref):
    u_ref[...] = (x_ref[...] * (1.0 + scale_ref[0, 0]) + shift_ref[0, 0]).astype(u_ref.dtype)


def _ln_kernel(x_ref, y_ref, gate_ref, g_ref, b_ref, scale_ref, shift_ref, xo_ref, uo_ref, *, alpha):
    z = alpha * x_ref[...] + gate_ref[0, 0] * y_ref[...]
    zc = z - jnp.mean(z, axis=-1, keepdims=True)
    yn = zc * lax.rsqrt(jnp.mean(zc * zc, axis=-1, keepdims=True) + LN_EPS)
    xn = yn * g_ref[...] + b_ref[...]
    xo_ref[...] = xn
    uo_ref[...] = (xn * (1.0 + scale_ref[0, 0]) + shift_ref[0, 0]).astype(uo_ref.dtype)


def _ln_last_kernel(x_ref, y_ref, gate_ref, g_ref, b_ref, xo_ref, *, alpha):
    z = alpha * x_ref[...] + gate_ref[0, 0] * y_ref[...]
    zc = z - jnp.mean(z, axis=-1, keepdims=True)
    yn = zc * lax.rsqrt(jnp.mean(zc * zc, axis=-1, keepdims=True) + LN_EPS)
    xo_ref[...] = yn * g_ref[...] + b_ref[...]


class _Tokens:
    def __init__(self, n_ctx, t_lat, n_lat_batches, d):
        self.n_ctx, self.t_lat, self.d = n_ctx, t_lat, d
        self.n = n_ctx + t_lat * n_lat_batches
        tm = 256
        while n_ctx % tm or t_lat % tm:
            tm //= 2
        self.tm = tm
        self.ctx_tiles = n_ctx // tm
        self.tiles_per_lat = t_lat // tm

    def mod_row(self, i):
        return jnp.where(i < self.ctx_tiles, 0, 1 + (i - self.ctx_tiles) // self.tiles_per_lat)

    def mod_spec(self, idx):
        return pl.BlockSpec((1, 1, 1, self.d), lambda i: (idx, self.mod_row(i), 0, 0))

    def row_spec(self):
        return pl.BlockSpec((self.tm, self.d), lambda i: (i, 0))

    def vec_spec(self, idx):
        return pl.BlockSpec((None, 1, self.d), lambda i: (idx, 0, 0))


def modulate(tok, x, mod, idx_scale, idx_shift):
    est = 2 * tok.tm * tok.d * 6 + 4 * tok.tm * tok.d * 4
    return pl.pallas_call(
        _modulate_kernel,
        grid=(tok.n // tok.tm,),
        in_specs=[tok.row_spec(), tok.mod_spec(idx_scale), tok.mod_spec(idx_shift)],
        out_specs=tok.row_spec(),
        out_shape=jax.ShapeDtypeStruct((tok.n, tok.d), BF16),
        compiler_params=_cparams(("arbitrary",), est),
    )(x, mod, mod)


def residual_ln(tok, x, y, mod, idx_gate, ln_gb, idx_g, idx_b, alpha, idx_scale=None, idx_shift=None):
    est = 2 * tok.tm * tok.d * (4 + 4 + 4 + 2) + 6 * tok.tm * tok.d * 4
    in_specs = [tok.row_spec(), tok.row_spec(), tok.mod_spec(idx_gate), tok.vec_spec(idx_g), tok.vec_spec(idx_b)]
    args = [x, y, mod, ln_gb, ln_gb]
    if idx_scale is None:
        return pl.pallas_call(
            functools.partial(_ln_last_kernel, alpha=alpha),
            grid=(tok.n // tok.tm,),
            in_specs=in_specs,
            out_specs=tok.row_spec(),
            out_shape=jax.ShapeDtypeStruct((tok.n, tok.d), F32),
            compiler_params=_cparams(("arbitrary",), est),
        )(*args), None
    in_specs += [tok.mod_spec(idx_scale), tok.mod_spec(idx_shift)]
    args += [mod, mod]
    return pl.pallas_call(
        functools.partial(_ln_kernel, alpha=alpha),
        grid=(tok.n // tok.tm,),
        in_specs=in_specs,
        out_specs=[tok.row_spec(), tok.row_spec()],
        out_shape=[jax.ShapeDtypeStruct((tok.n, tok.d), F32), jax.ShapeDtypeStruct((tok.n, tok.d), BF16)],
        compiler_params=_cparams(("arbitrary",), est),
    )(*args)


ATTN_Q_ROWS = 512


def _softmax_pv(scores, vals):
    m = jnp.max(scores[0], axis=-1, keepdims=True)
    for s in scores[1:]:
        m = jnp.maximum(m, jnp.max(s, axis=-1, keepdims=True))
    acc = None
    den = None
    for s, v in zip(scores, vals):
        p = jnp.exp(s - m)
        ps = jnp.sum(p, axis=-1, keepdims=True)
        pv = jnp.dot(p.astype(BF16), v, preferred_element_type=F32)
        acc = pv if acc is None else acc + pv
        den = ps if den is None else den + ps
    return acc / den


def _rope_lanes(x, cos, sin_signed, n):
    lane = lax.broadcasted_iota(jnp.int32, x.shape, 1)
    first = ((lane // n) % 2) == 0
    partner = jnp.where(first, pltpu.roll(x, LANE - n, 1), pltpu.roll(x, n, 1))
    return x * cos + partner * sin_signed


def _rms_rows(x, g):
    return x * lax.rsqrt(jnp.mean(x * x, axis=-1, keepdims=True) + RMS_EPS) * g


def _attn_refs(refs, n_main, use_rope, has_cache, n_cache, aliased):
    it = iter(refs)
    main = [next(it) for _ in range(n_main)]
    rope = (next(it), next(it)) if use_rope else (None, None)
    cache = [next(it) for _ in range(n_cache)] if has_cache else [None] * n_cache
    if aliased:
        next(it)
    return main, rope, cache, next(it)


def _gqa_fused_kernel(*refs, hp, kv_group, t, scale, n_rot, use_rope, has_cache, aliased):
    (q_ref, k_ref, v_ref, gq_ref, gk_ref), (cos_ref, sin_ref), (ck_ref, cv_ref), o_ref = _attn_refs(
        refs, 5, use_rope, has_cache, 2, aliased)
    tq = min(ATTN_Q_ROWS, t)
    n_kv = max(1, hp // kv_group)
    keys, vals = [], []
    for j in range(n_kv):
        sl = slice(j * LANE, (j + 1) * LANE)
        k = _rms_rows(k_ref[:, sl], gk_ref[...])
        if use_rope:
            k = _rope_lanes(k, cos_ref[...], sin_ref[...], n_rot)
        kj, vj = [k.astype(BF16)], [v_ref[:, sl].astype(BF16)]
        if has_cache:
            kj.insert(0, ck_ref[:, sl].astype(BF16))
            vj.insert(0, cv_ref[:, sl].astype(BF16))
        keys.append(kj)
        vals.append(vj)
    for hh in range(hp):
        j = hh // kv_group if n_kv > 1 else 0
        for r0 in range(0, t, tq):
            q = _rms_rows(q_ref[r0:r0 + tq, hh * LANE:(hh + 1) * LANE], gq_ref[...])
            if use_rope:
                q = _rope_lanes(q, cos_ref[r0:r0 + tq, :], sin_ref[r0:r0 + tq, :], n_rot)
            qb = q.astype(BF16)
            o = _softmax_pv([_dot_nt(qb, k) * scale for k in keys[j]], vals[j])
            o_ref[r0:r0 + tq, hh * LANE:(hh + 1) * LANE] = o.astype(o_ref.dtype)


def _diff_fused_kernel(*refs, hp, t, scale, dd, n_rot, use_rope, has_cache, aliased):
    (q_ref, k_ref, v_ref, lam_ref, g_ref), (cos_ref, sin_ref), (ck_ref, cv_ref), o_ref = _attn_refs(
        refs, 5, use_rope, has_cache, 2, aliased)
    tq = min(ATTN_Q_ROWS, t)
    for hh in range(hp):
        sl = slice(hh * LANE, (hh + 1) * LANE)
        k = k_ref[:, sl]
        if use_rope:
            k = _rope_lanes(k, cos_ref[...], sin_ref[...], n_rot)
        keys, vals = [k.astype(BF16)], [v_ref[:, sl].astype(BF16)]
        if has_cache:
            keys.insert(0, ck_ref[:, sl].astype(BF16))
            vals.insert(0, cv_ref[:, sl].astype(BF16))
        for r0 in range(0, t, tq):
            q = q_ref[r0:r0 + tq, sl]
            if use_rope:
                q = _rope_lanes(q, cos_ref[r0:r0 + tq, :], sin_ref[r0:r0 + tq, :], n_rot)
            lane = lax.broadcasted_iota(jnp.int32, q.shape, 1)
            q1 = jnp.where(lane < dd, q, 0.0).astype(BF16)
            q2 = jnp.where(lane < dd, 0.0, q).astype(BF16)
            a1 = _softmax_pv([_dot_nt(q1, kk) * scale for kk in keys], vals)
            a2 = _softmax_pv([_dot_nt(q2, kk) * scale for kk in keys], vals)
            d = a1 - lam_ref[...] * a2
            o_ref[r0:r0 + tq, sl] = _rms_rows(d, g_ref[...]).astype(o_ref.dtype)


def _mla_fused_kernel(*refs, hp, t, scale, rope_d, n_rot, use_rope, has_cache, aliased):
    (q_ref, kv_ref, kr_ref), (cos_ref, sin_ref), (kvc_ref, krc_ref), o_ref = _attn_refs(
        refs, 3, use_rope, has_cache, 2, aliased)
    tq = min(ATTN_Q_ROWS, t)
    hw = 2 * LANE
    kr = kr_ref[...]
    if use_rope:
        kr = _rope_lanes(kr, cos_ref[...], sin_ref[...], n_rot)
    krs = [kr[:, :rope_d].astype(BF16)]
    if has_cache:
        krs.insert(0, krc_ref[...].astype(BF16))
    for hh in range(hp):
        c0 = hh * hw
        kns, vals = [kv_ref[:, c0:c0 + LANE]], [kv_ref[:, c0 + LANE:c0 + hw]]
        if has_cache:
            kns.insert(0, kvc_ref[:, c0:c0 + LANE])
            vals.insert(0, kvc_ref[:, c0 + LANE:c0 + hw])
        for r0 in range(0, t, tq):
            qn = q_ref[r0:r0 + tq, c0:c0 + LANE].astype(BF16)
            qr = q_ref[r0:r0 + tq, c0 + LANE:c0 + hw]
            if use_rope:
                qr = _rope_lanes(qr, cos_ref[r0:r0 + tq, :], sin_ref[r0:r0 + tq, :], n_rot)
            qr = qr[:, :rope_d].astype(BF16)
            scores = [(_dot_nt(qn, kn) + _dot_nt(qr, kk)) * scale for kn, kk in zip(kns, krs)]
            o_ref[r0:r0 + tq, hh * LANE:(hh + 1) * LANE] = _softmax_pv(scores, vals).astype(o_ref.dtype)


def _attn_launch(body, n_rows, heads, hp, row0, b, t, main, rope, cache, prev, width_est):
    rb0 = row0 // t
    in_specs = [s for _, s in main]
    args = [a for a, _ in main]
    if rope is not None:
        in_specs += [pl.BlockSpec((t, LANE), lambda bi, hb: (0, 0))] * 2
        args += list(rope)
    in_specs += [s for _, s in cache]
    args += [a for a, _ in cache]
    aliases = {}
    if prev is not None:
        aliases = {len(args): 0}
        in_specs.append(pl.BlockSpec(memory_space=pl.ANY))
        args.append(prev)
    n_cached = cache[1][0].shape[2] if cache else 0
    est = 2 * width_est + 8 * min(ATTN_Q_ROWS, t) * (t + n_cached) * 4 + (4 << 20)
    return pl.pallas_call(
        body,
        grid=(b, heads // hp),
        in_specs=in_specs,
        out_specs=pl.BlockSpec((t, hp * LANE), lambda bi, hb: (rb0 + bi, hb)),
        out_shape=jax.ShapeDtypeStruct((n_rows, heads * LANE), BF16),
        input_output_aliases=aliases,
        compiler_params=_cparams(("arbitrary", "arbitrary"), est),
    )(*args)


def _cache_kernel(*refs, n_kv, rope_d, aliased):
    ckv_ref, g_ckv_ref, kr_ref, bk_ref, bv_ref, gk_ref, g_gk_ref, gv_ref = refs[:8]
    o_ckv, o_kr, o_dk, o_dv, o_gk, o_gv = refs[-6:]
    o_ckv[...] = _rms_rows(ckv_ref[...], g_ckv_ref[...])
    o_kr[...] = kr_ref[:, :rope_d]
    o_dk[...] = bk_ref[...]
    o_dv[...] = bv_ref[...]
    for j in range(n_kv):
        sl = slice(j * LANE, (j + 1) * LANE)
        o_gk[:, sl] = _rms_rows(gk_ref[:, sl], g_gk_ref[...])
    o_gv[...] = gv_ref[...]


def write_caches(h, col, l, depth, b, t, kv_rank, rope_d, unit, kvw, g_ckv, g_gk, prev):
    def hcols(c0, width):
        return pl.BlockSpec((t, width), lambda bi: (bi, c0 // width))

    def vec(n):
        return pl.BlockSpec((1, n), lambda bi: (0, 0))

    widths = (kv_rank, rope_d, unit, unit, kvw, kvw)
    in_specs = [hcols(col["ckv"], kv_rank), vec(kv_rank), hcols(col["kr"], LANE), hcols(col["bk"], unit),
                hcols(col["bv"], unit), hcols(col["gk"], kvw), vec(LANE), hcols(col["gv"], kvw)]
    args = [h, g_ckv, h, h, h, h, g_gk, h]
    aliases = {}
    if prev is not None:
        aliases = {len(args) + i: i for i in range(6)}
        in_specs += [pl.BlockSpec(memory_space=pl.ANY)] * 6
        args += list(prev)
    est = 4 * t * (kv_rank + LANE + 2 * unit + 2 * kvw) * 4 * 2 + (4 << 20)
    return pl.pallas_call(
        functools.partial(_cache_kernel, n_kv=kvw // LANE, rope_d=rope_d, aliased=prev is not None),
        grid=(b,),
        in_specs=in_specs,
        out_specs=[pl.BlockSpec((None, None, t, w), lambda bi: (bi, l, 0, 0)) for w in widths],
        out_shape=[jax.ShapeDtypeStruct((b, depth, t, w), F32) for w in widths],
        input_output_aliases=aliases,
        compiler_params=_cparams(("arbitrary",), est),
    )(*args)


def _hgrn_kernel(*refs, t, nb, chunk, sub, q_scale, has_s0, want_state, aliased):
    it = iter(refs)
    dq_ref, di_ref, df_ref, dfb_ref, dg_ref, tab_ref = (next(it) for _ in range(6))
    s0_ref = next(it) if has_s0 else None
    for _ in range(aliased):
        next(it)
    o_ref = next(it)
    st_out_ref = next(it) if want_state else None
    st_scr, o_scr, cum_scr, k_scr, q_scr = (next(it) for _ in range(5))

    n_chunks = t // chunk
    n_sub = chunk // sub
    half = sub // 2
    half_row = lax.broadcasted_iota(jnp.int32, (half, 1), 0)
    row = lax.broadcasted_iota(jnp.int32, (chunk, chunk), 0)
    col = lax.broadcasted_iota(jnp.int32, (chunk, chunk), 1)
    key_row = lax.broadcasted_iota(jnp.int32, (chunk, 1), 0)
    neg_inf = jnp.float32(-jnp.inf)
    tab = tab_ref[...]
    tri_f = jnp.where(row >= col, 1.0, 0.0).astype(F32)
    tri_b = jnp.where(row <= col, 1.0, 0.0).astype(F32)
    same_sub = (row // sub) == (col // sub)

    for bb in range(nb):
        for direction in range(2):
            if has_s0:
                st_scr[2 * bb + direction] = s0_ref[bb, direction, 0].T
            else:
                st_scr[2 * bb + direction] = jnp.zeros(st_scr.shape[1:], F32)

    def prepare(ci, carry):
        r0 = pl.multiple_of(ci * chunk, chunk)
        zq = dq_ref[pl.ds(r0, chunk), :]
        q_scr[pl.ds(r0, chunk), :] = zq * jax.nn.sigmoid(zq) * q_scale
        for direction, z_ref in enumerate((df_ref, dfb_ref)):
            log_lb = tab[3 * direction:3 * direction + 1, :]
            log_1m = tab[3 * direction + 1:3 * direction + 2, :]
            one_m = tab[3 * direction + 2:3 * direction + 3, :]
            z = z_ref[pl.ds(r0, chunk), :]
            e = jnp.exp(-jnp.abs(z))
            d1 = 1.0 + e
            t2 = log_1m + (jnp.minimum(z, 0.0) - jnp.log(d1))
            g = jnp.maximum(log_lb, t2) + jnp.log(1.0 + jnp.exp(-jnp.abs(log_lb - t2)))
            k_scr[direction, pl.ds(r0, chunk), :] = one_m * (jnp.where(z >= 0.0, e, 1.0) / d1)
            cum_scr[direction, pl.ds(r0, chunk), :] = jnp.dot(
                tri_b if direction else tri_f, g, precision=lax.Precision.HIGHEST, preferred_element_type=F32)
        return carry

    lax.fori_loop(0, nb * n_chunks, prepare, 0, unroll=2)

    def chain_step(ci, bb, direction):
        rev = direction == 1
        r0 = pl.multiple_of(bb * t + ((n_chunks - 1 - ci) if rev else ci) * chunk, chunk)
        q = q_scr[pl.ds(r0, chunk), :]
        k = k_scr[direction, pl.ds(r0, chunk), :]
        cum = cum_scr[direction, pl.ds(r0, chunk), :]
        v = di_ref[pl.ds(r0, chunk), :]
        edge = cum[0:1, :] if rev else cum[chunk - 1:chunk, :]
        slot = 2 * bb + direction
        st = st_scr[slot]
        vb = v.astype(BF16)
        inter = _dot_nt((q * jnp.exp(cum)).astype(BF16), st.astype(BF16))

        mid = [cum[i * sub + half:i * sub + half + 1, :] if rev else cum[i * sub + half - 1:i * sub + half, :]
               for i in range(n_sub)]
        base2 = jnp.concatenate([jnp.broadcast_to(m, (sub, LANE)) for m in mid], axis=0)
        late = (key_row % sub) >= half
        q_side, k_side = (~late, late) if rev else (late, ~late)
        q2 = (q * jnp.exp(jnp.where(q_side, cum - base2, neg_inf))).astype(BF16)
        k2 = (k * jnp.exp(jnp.where(k_side, base2 - cum, neg_inf))).astype(BF16)
        att2 = jnp.where(same_sub, _dot_nt(q2, k2), 0.0)
        near = inter + jnp.dot(att2.astype(BF16), vb, preferred_element_type=F32)

        outs = []
        for i in range(n_sub):
            lo = i * sub
            acc = near[lo:lo + sub]
            if (i < n_sub - 1) if rev else (i > 0):
                base = cum[lo + sub:lo + sub + 1, :] if rev else cum[lo - 1:lo, :]
                seen = (key_row >= lo + sub) if rev else (key_row < lo)
                qn = (q[lo:lo + sub] * jnp.exp(cum[lo:lo + sub] - base)).astype(BF16)
                kn = (k * jnp.exp(jnp.where(seen, base - cum, neg_inf))).astype(BF16)
                att = _dot_nt(qn, kn)
                acc = acc + jnp.dot(att.astype(BF16), vb, preferred_element_type=F32)
            for hb in range(sub // half):
                lo8 = lo + hb * half
                q8 = q[lo8:lo8 + half]
                cum8 = cum[lo8:lo8 + half]
                acc8 = acc[hb * half:(hb + 1) * half]
                for s in range(half):
                    r = lo8 + s
                    live = (half_row <= s) if rev else (half_row >= s)
                    w = jnp.exp(jnp.where(live, cum8 - cum_scr[direction, pl.ds(r0 + r, 1), :], neg_inf))
                    colv = jnp.sum(q8 * k_scr[direction, pl.ds(r0 + r, 1), :] * w, axis=-1, keepdims=True)
                    acc8 = acc8 + colv * di_ref[pl.ds(r0 + r, 1), :]
                outs.append(acc8)
        o_scr[direction, pl.ds(r0, chunk), :] = jnp.concatenate(outs, axis=0)
        kl = (k * jnp.exp(edge - cum)).astype(BF16)
        st_scr[2 * bb + direction] = st * jnp.exp(edge) + _dot_tn(vb, kl)

    def body(ci, carry):
        for bb in range(nb):
            for direction in range(2):
                chain_step(ci, bb, direction)
        return carry

    lax.fori_loop(0, n_chunks, body, 0)

    tot = o_scr[0] + o_scr[1]
    zg = dg_ref[...]
    y = tot * lax.rsqrt(jnp.mean(tot * tot, axis=-1, keepdims=True) + RMS_EPS) * tab[6:7, :]
    o_ref[...] = (y * (zg * jax.nn.sigmoid(zg))).astype(o_ref.dtype)
    if want_state:
        for bb in range(nb):
            for direction in range(2):
                st_out_ref[bb, direction, 0] = st_scr[2 * bb + direction].T


def hgrn_mix(h, row0, b, t, heads, dk, dv, cols, tab, s0, state_into, prev):
    assert dk == LANE and dv == LANE
    nb = HGRN_BATCHES if (b % HGRN_BATCHES == 0 and row0 % (HGRN_BATCHES * t) == 0) else 1
    assert row0 % (nb * t) == 0 and all(cc % LANE == 0 for cc in cols)
    chunk = min(HGRN_CHUNK, t)
    sub = min(HGRN_SUB, chunk)
    rb0 = row0 // (nb * t)

    def col_spec(c0):
        return pl.BlockSpec((nb * t, LANE), lambda bi, hh: (rb0 + bi, c0 // LANE + hh))

    st_spec = pl.BlockSpec((nb, 2, 1, dk, dv), lambda bi, hh: (bi, 0, hh, 0, 0))
    in_specs = [col_spec(cc) for cc in cols] + [pl.BlockSpec((8, LANE), lambda bi, hh: (0, hh))]
    args = [h] * len(cols) + [tab]
    if s0 is not None:
        in_specs.append(st_spec)
        args.append(s0)
    want_state = state_into is not None
    aliases = {}
    if prev is not None:
        aliases[len(args)] = 0
        in_specs.append(pl.BlockSpec(memory_space=pl.ANY))
        args.append(prev)
    if want_state and state_into[2] is not None:
        aliases[len(args)] = 1
        in_specs.append(pl.BlockSpec(memory_space=pl.ANY))
        args.append(state_into[2])
    out_specs = [pl.BlockSpec((nb * t, LANE), lambda bi, hh: (rb0 + bi, hh))]
    out_shape = [jax.ShapeDtypeStruct((h.shape[0], heads * dv), BF16)]
    if want_state:
        layer, depth = state_into[0], state_into[1]
        out_specs.append(pl.BlockSpec((nb, None, 2, 1, dk, dv), lambda bi, hh: (bi, layer, 0, hh, 0, 0)))
        out_shape.append(jax.ShapeDtypeStruct((b, depth, 2, heads, dk, dv), F32))
    est = 2 * nb * t * LANE * (5 * 4 + 2) + 7 * nb * t * LANE * 4 + (8 * nb + 6) * dk * dv * 4 + (4 << 20)
    res = pl.pallas_call(
        functools.partial(_hgrn_kernel, t=t, nb=nb, chunk=chunk, sub=sub, q_scale=dk ** -0.5,
                          has_s0=s0 is not None, want_state=want_state, aliased=len(aliases)),
        grid=(b // nb, heads),
        in_specs=in_specs,
        out_specs=out_specs,
        out_shape=out_shape,
        input_output_aliases=aliases,
        scratch_shapes=[pltpu.VMEM((2 * nb, dv, dk), F32), pltpu.VMEM((2, nb * t, LANE), F32),
                        pltpu.VMEM((2, nb * t, LANE), F32), pltpu.VMEM((2, nb * t, LANE), F32),
                        pltpu.VMEM((nb * t, LANE), F32)],
        compiler_params=_cparams(("arbitrary", "arbitrary"), est),
    )(*args)
    return (res[0], res[1]) if want_state else (res[0], None)


def _rope_lane_tables(t, d):
    n = d // 4
    inv = ROPE_THETA ** (-jnp.arange(n, dtype=F32) / n)
    pos = jnp.arange(t)
    rows = (pos // GRID_W).astype(F32)
    cols = (pos % GRID_W).astype(F32)
    ar = rows[:, None] * inv[None, :]
    ac = cols[:, None] * inv[None, :]
    cos = jnp.concatenate([jnp.cos(ar), jnp.cos(ar), jnp.cos(ac), jnp.cos(ac)], axis=-1)
    sin = jnp.concatenate([-jnp.sin(ar), jnp.sin(ar), -jnp.sin(ac), jnp.sin(ac)], axis=-1)
    reps = LANE // d
    return jnp.tile(cos, (1, reps)), jnp.tile(sin, (1, reps))


def kernel(x_prompt, x_sample, cache_mla_ckv, cache_mla_krope, cache_diff_k, cache_diff_v, cache_gqa_k, cache_gqa_v, state_hgrn, c, c_ctx, w_in, w_out, mla_g_cq, mla_g_ckv, mla_w_uq, mla_w_ukv, diff_lambda, diff_subln_g, gqa_g_q, gqa_g_k, hgrn_gamma, hgrn_g_norm, ada_w, ada_b, ln_g, ln_b, ffn_w1, ffn_w3, ffn_w2, moe_router, moe_w1, moe_w3, moe_w2):
    bc, tc, d = x_prompt.shape
    bl, tl, _ = x_sample.shape
    depth = w_in.shape[0]
    q_rank = mla_g_cq.shape[1]
    kv_rank = mla_g_ckv.shape[1]
    rope_d = cache_mla_krope.shape[-1]
    heads = cache_diff_k.shape[3]
    dd = cache_diff_k.shape[4] // 2
    kv_heads = cache_gqa_k.shape[3]
    gd = cache_gqa_k.shape[4]
    hk = state_hgrn.shape[4]
    hv = state_hgrn.shape[5]
    n_exp = moe_w1.shape[1]
    alpha = (2 * depth) ** 0.25
    n_ctx = bc * tc
    n_lat = bl * tl
    tok = _Tokens(n_ctx, tl, bl, d)
    mla_hw = 2 * LANE
    assert MLA_NOPE_DIM + rope_d <= mla_hw and MLA_NOPE_DIM + MLA_V_DIM == mla_hw

    unit = heads * LANE
    kvw = kv_heads * gd
    assert 2 * dd == LANE and gd == LANE and hk == LANE and hv == LANE
    names = ("cq", "ckv", "kr", "bq", "bk", "bv", "gq", "gk", "gv", "dq", "di", "df", "dfb", "dg")
    widths = (q_rank, kv_rank, rope_d, unit, unit, unit, unit, kvw, kvw, unit, unit, unit, unit, unit)
    assert sum(widths) == w_in.shape[2]
    src, o = {}, 0
    for nm, w in zip(names, widths):
        src[nm] = (o, w)
        o += w
    col, pieces, o = {}, [], 0

    def place(nm, pad_to=None):
        nonlocal o
        if nm is not None:
            s0, w = src[nm]
            col[nm] = o
            pieces.append(w_in[:, :, s0:s0 + w])
            o += w
        if pad_to is not None and o % pad_to:
            z = pad_to - o % pad_to
            pieces.append(jnp.zeros((depth, d, z), w_in.dtype))
            o += z

    place("cq")
    place("gk", pad_to=unit)
    place("ckv")
    place("gv")
    place("kr", pad_to=LANE)
    place(None, pad_to=unit)
    for nm in ("bq", "bk", "bv", "gq", "dq", "di", "df", "dfb", "dg"):
        place(nm)
    w_in_b = jnp.concatenate(pieces, axis=-1).astype(BF16)
    assert q_rank <= unit and col["gk"] % kvw == 0 and col["gv"] % kvw == 0 and col["ckv"] % kv_rank == 0
    w_uq_b = jnp.pad(mla_w_uq.reshape(depth, q_rank, heads, MLA_NOPE_DIM + rope_d),
                     ((0, 0), (0, 0), (0, 0), (0, mla_hw - MLA_NOPE_DIM - rope_d))
                     ).reshape(depth, q_rank, heads * mla_hw).astype(BF16)
    w_ukv_b = mla_w_ukv.astype(BF16)
    ffn_w2_b = ffn_w2.astype(BF16)
    e_ff = moe_w1.shape[-1]
    n_moe = moe_w1.shape[0]
    moe_w1_r = moe_w1.reshape(n_moe * n_exp, d, e_ff)
    moe_w3_r = moe_w3.reshape(n_moe * n_exp, d, e_ff)
    moe_w2_b = moe_w2.astype(BF16).reshape(n_moe * n_exp, e_ff, d)
    router_b = jnp.pad(moe_router, ((0, 0), (0, 0), (0, LANE - n_exp))).astype(BF16)

    cond = jnp.concatenate([c_ctx[None, :], c], axis=0)
    n_cond = cond.shape[0]
    cond_p = jnp.pad(jax.nn.silu(cond), ((0, -n_cond % 16), (0, 0))).astype(BF16)
    mods = []
    for l in range(depth):
        ml = matmul_few_rows(cond_p, ada_w, l)[:n_cond] + ada_b[l][None, :]
        mods.append(ml.reshape(n_cond, 6, d).transpose(1, 0, 2))
    mod = jnp.concatenate(mods, axis=0).reshape(depth * 6, n_cond, 1, d)
    ln_gb = jnp.concatenate([ln_g.reshape(depth * 2, 1, d), ln_b.reshape(depth * 2, 1, d)], axis=0)

    def lower_bounds(gamma):
        cs = jnp.cumsum(jax.nn.softmax(gamma.astype(F32), axis=0), axis=0)
        return cs - cs[0:1]

    lb_f = lower_bounds(hgrn_gamma[0])
    lb_b = lower_bounds(hgrn_gamma[1])
    n_rot_small, n_rot_big = dd // 4, gd // 4
    assert rope_d == dd
    rope_small = _rope_lane_tables(tl, dd)
    rope_big = _rope_lane_tables(tl, gd)
    n_tok = n_ctx + n_lat
    past = cache_mla_ckv.shape[2]
    cache_dk = cache_diff_k.reshape(bl, depth, past, unit)
    cache_dv = cache_diff_v.reshape(bl, depth, past, unit)
    cache_gk = cache_gqa_k.reshape(bl, depth, past, kvw)
    cache_gv = cache_gqa_v.reshape(bl, depth, past, kvw)
    hgrn_cols = [col[nm] for nm in ("dq", "di", "df", "dfb", "dg")]

    def mixers(h, q_all, kv_new, kv_cache, l, lat, prevs, new_states=None):
        row0, b, t = (n_ctx, bl, tl) if lat else (0, bc, tc)
        hp = 1 if lat else heads
        rb0 = row0 // t
        flags = dict(use_rope=lat, has_cache=lat, aliased=prevs is not None)
        pa, pb, pc, pd = prevs if prevs is not None else (None,) * 4

        def hcols(c0, width):
            return pl.BlockSpec((t, width), lambda bi, hb: (rb0 + bi, c0 // width + hb))

        def vec(n):
            return pl.BlockSpec((1, n), lambda bi, hb: (0, 0))

        def cache_spec(width, per_head):
            if per_head:
                return pl.BlockSpec((None, None, past, width), lambda bi, hb: (bi, l, 0, hb))
            return pl.BlockSpec((None, None, past, width), lambda bi, hb: (bi, l, 0, 0))

        wide = pl.BlockSpec((t, hp * mla_hw), lambda bi, hb: (rb0 + bi, hb))
        main = [(q_all, wide), (kv_new, wide), (h, pl.BlockSpec((t, LANE), lambda bi, hb: (rb0 + bi, col["kr"] // LANE)))]
        cache = []
        if lat:
            cache = [(kv_cache, pl.BlockSpec((past, hp * mla_hw), lambda bi, hb: (bi, hb))),
                     (cache_mla_krope, cache_spec(rope_d, False))]
        o_a = _attn_launch(
            functools.partial(_mla_fused_kernel, hp=hp, t=t, scale=(MLA_NOPE_DIM + rope_d) ** -0.5, rope_d=rope_d,
                              n_rot=n_rot_small, **flags),
            n_tok, heads, hp, row0, b, t, main, rope_small if lat else None, cache, pa,
            t * hp * mla_hw * 6 + past * hp * mla_hw * 2)

        lam_p = diff_lambda[l].astype(F32)
        lam_init = 0.8 - 0.6 * math.exp(-0.3 * l)
        lam = jnp.exp(jnp.sum(lam_p[0] * lam_p[1])) - jnp.exp(jnp.sum(lam_p[2] * lam_p[3])) + lam_init
        lam_v = jnp.full((1, LANE), lam, F32)
        g_v = (diff_subln_g[l] * (1.0 - lam_init)).reshape(1, LANE)
        w = hp * LANE
        main = [(h, hcols(col["bq"], w)), (h, hcols(col["bk"], w)), (h, hcols(col["bv"], w)),
                (lam_v, vec(LANE)), (g_v, vec(LANE))]
        cache = [(cache_dk, cache_spec(w, True)), (cache_dv, cache_spec(w, True))] if lat else []
        o_b = _attn_launch(
            functools.partial(_diff_fused_kernel, hp=hp, t=t, scale=dd ** -0.5, dd=dd, n_rot=n_rot_small, **flags),
            n_tok, heads, hp, row0, b, t, main, rope_small if lat else None, cache, pb,
            (3 * t + 2 * past) * w * 4)

        kv_group = heads // kv_heads
        if lat:
            kspec = pl.BlockSpec((t, LANE), lambda bi, hb: (rb0 + bi, col["gk"] // LANE + hb // kv_group))
            vspec = pl.BlockSpec((t, LANE), lambda bi, hb: (rb0 + bi, col["gv"] // LANE + hb // kv_group))
            cspec = pl.BlockSpec((None, None, past, LANE), lambda bi, hb: (bi, l, 0, hb // kv_group))
            cache = [(cache_gk, cspec), (cache_gv, cspec)]
        else:
            kspec = pl.BlockSpec((t, kvw), lambda bi, hb: (rb0 + bi, col["gk"] // kvw))
            vspec = pl.BlockSpec((t, kvw), lambda bi, hb: (rb0 + bi, col["gv"] // kvw))
            cache = []
        main = [(h, hcols(col["gq"], w)), (h, kspec), (h, vspec),
                (gqa_g_q[l].reshape(1, gd), vec(gd)), (gqa_g_k[l].reshape(1, gd), vec(gd))]
        o_c = _attn_launch(
            functools.partial(_gqa_fused_kernel, hp=hp, kv_group=kv_group, t=t, scale=gd ** -0.5, n_rot=n_rot_big,
                              **flags),
            n_tok, heads, hp, row0, b, t, main, rope_big if lat else None, cache, pc,
            (t + 2 * t + 2 * past) * w * 4)

        tab = jnp.stack([jnp.log(lb_f[l]), jnp.log1p(-lb_f[l]), 1.0 - lb_f[l],
                         jnp.log(lb_b[l]), jnp.log1p(-lb_b[l]), 1.0 - lb_b[l],
                         jnp.tile(hgrn_g_norm[l], heads), jnp.zeros((heads * hk,), F32)], axis=0)
        o_d, states = hgrn_mix(h, row0, b, t, heads, hk, hv, hgrn_cols, tab, state_hgrn[:, l] if lat else None,
                               None if lat else (l, depth, new_states), pd)
        return (o_a, o_b, o_c, o_d), states

    x = jnp.concatenate([x_prompt.reshape(n_ctx, d), x_sample.reshape(n_lat, d)], axis=0)
    u = modulate(tok, x, mod, 1, 0)
    caches, new_states = None, None
    for l in range(depth):
        h = matmul(u, w_in_b, F32, layer=l)
        q_all = rms_matmul(h, col["cq"], unit, mla_g_cq[l].reshape(1, q_rank), w_uq_b, l, F32)
        kv_new = rms_matmul(h, col["ckv"], kv_rank, mla_g_ckv[l].reshape(1, kv_rank), w_ukv_b, l, BF16)
        kv_cache = matmul(cache_mla_ckv[:, l].reshape(bl * past, kv_rank).astype(BF16), w_ukv_b, BF16, layer=l)
        parts, new_states = mixers(h, q_all, kv_new, kv_cache, l, False, None, new_states)
        parts, _ = mixers(h, q_all, kv_new, kv_cache, l, True, parts)
        caches = write_caches(h, col, l, depth, bc, tc, kv_rank, rope_d, unit, kvw, mla_g_ckv[l].reshape(1, kv_rank),
                              gqa_g_k[l].reshape(1, gd), caches)
        mix = out_proj(parts, w_out, l)
        x, u = residual_ln(tok, x, mix, mod, l * 6 + 2, ln_gb, 2 * l, 2 * depth + 2 * l, alpha,
                           l * 6 + 4, l * 6 + 3)
        j = l // 2
        if l % 2 == 0:
            f = down_proj(swiglu_up(u, ffn_w1, ffn_w3, j), ffn_w2_b, j)
        else:
            logits = matmul(u, router_b, F32, layer=j)[:, :n_exp]
            f = moe_experts(u, logits, moe_w1_r, moe_w3_r, moe_w2_b, j, n_exp)
        if l + 1 < depth:
            x, u = residual_ln(tok, x, f, mod, l * 6 + 5, ln_gb, 2 * l + 1, 2 * depth + 2 * l + 1, alpha,
                               (l + 1) * 6 + 1, (l + 1) * 6 + 0)
        else:
            x, _ = residual_ln(tok, x, f, mod, l * 6 + 5, ln_gb, 2 * l + 1, 2 * depth + 2 * l + 1, alpha)

    y_prompt = x[:n_ctx].reshape(bc, tc, d)
    y_sample = x[n_ctx:].reshape(bl, tl, d)
    new_ckv, new_kr, new_dk, new_dv, new_gk, new_gv = caches
    return (y_prompt, y_sample, new_ckv, new_kr,
            new_dk.reshape(bc, depth, tc, heads, 2 * dd), new_dv.reshape(bc, depth, tc, heads, 2 * dd),
            new_gk.reshape(bc, depth, tc, kv_heads, gd), new_gv.reshape(bc, depth, tc, kv_heads, gd), new_states)
```

```python
import functools
import math

import jax
import jax.numpy as jnp
from jax import lax
from jax.experimental import pallas as pl
from jax.experimental.pallas import tpu as pltpu

F32 = jnp.float32
BF16 = jnp.bfloat16

GRID_W = 64
ROPE_THETA = 10000.0
LN_EPS = 1e-5
RMS_EPS = 1e-6
MLA_NOPE_DIM = 128
MLA_V_DIM = 128
TOP_K = 2

LANE = 128
V7X_VMEM_BYTES = 64 * 1024 * 1024
VMEM_CAP = V7X_VMEM_BYTES - 6 * 1024 * 1024
HGRN_CHUNK = 64
HGRN_SUB = 16
HGRN_BATCHES = 4
MOE_ROWS = 512


def _cparams(sem, est_bytes):
    limit = int(min(VMEM_CAP, max(32 * 1024 * 1024, est_bytes * 5 // 4 + (4 << 20))))
    return pltpu.CompilerParams(dimension_semantics=sem, vmem_limit_bytes=limit)


def _pick_tile(n, target):
    if n <= target:
        return n
    t = (target // LANE) * LANE
    while t >= LANE:
        if n % t == 0:
            return t
        t -= LANE
    return n


def _pick_rows(n, target):
    if n <= target:
        return n
    t = target
    while t >= 8:
        if n % t == 0:
            return t
        t //= 2
    return n


def _dot_nt(a, b):
    return lax.dot_general(a, b, (((1,), (1,)), ((), ())), preferred_element_type=F32)


def _dot_tn(a, b):
    return lax.dot_general(a, b, (((0,), (0,)), ((), ())), preferred_element_type=F32)


def _mm_kernel(x_ref, w_ref, o_ref):
    o_ref[...] = jnp.dot(x_ref[...], w_ref[...], preferred_element_type=F32).astype(o_ref.dtype)


def matmul(x, w, out_dtype, layer=None, tm=1024, tn=1024):
    m, k = x.shape
    n = w.shape[-1]
    tm = _pick_rows(m, tm)
    tn = _pick_tile(n, tn)
    if w.ndim == 3:
        w_spec = pl.BlockSpec((None, k, tn), lambda j, i: (layer, 0, j))
    else:
        w_spec = pl.BlockSpec((k, tn), lambda j, i: (0, j))
    est = 2 * (tm * k * 2 + k * tn * 2 + tm * tn * jnp.dtype(out_dtype).itemsize) + tm * tn * 4
    return pl.pallas_call(
        _mm_kernel,
        grid=(n // tn, m // tm),
        in_specs=[pl.BlockSpec((tm, k), lambda j, i: (i, 0)), w_spec],
        out_specs=pl.BlockSpec((tm, tn), lambda j, i: (i, j)),
        out_shape=jax.ShapeDtypeStruct((m, n), out_dtype),
        compiler_params=_cparams(("arbitrary", "arbitrary"), est),
    )(x, w)


def _regroup_kernel(w_ref, o_ref, *, pieces):
    x = w_ref[...]
    parts = [jnp.zeros((x.shape[0], w), F32) if s0 is None else x[:, s0:s0 + w] for s0, w in pieces]
    o_ref[...] = jnp.concatenate(parts, axis=1).astype(o_ref.dtype)


def regroup_columns(w, pieces, rows=256):
    depth, k, n = w.shape
    n_out = sum(wd for _, wd in pieces)
    rows = _pick_rows(k, rows)
    est = 2 * rows * (n * 4 + n_out * 2) + 3 * rows * n_out * 4
    return pl.pallas_call(
        functools.partial(_regroup_kernel, pieces=pieces),
        grid=(depth, k // rows),
        in_specs=[pl.BlockSpec((None, rows, n), lambda l, i: (l, i, 0))],
        out_specs=pl.BlockSpec((None, rows, n_out), lambda l, i: (l, i, 0)),
        out_shape=jax.ShapeDtypeStruct((depth, k, n_out), BF16),
        compiler_params=_cparams(("arbitrary", "arbitrary"), est),
    )(w)


def _mm_castw_kernel(x_ref, w_ref, o_ref):
    o_ref[...] = jnp.dot(x_ref[...], w_ref[...].astype(BF16), preferred_element_type=F32).astype(o_ref.dtype)


def matmul_few_rows(x, w, layer, tn=1024):
    m, k = x.shape
    n = w.shape[-1]
    tn = _pick_tile(n, tn)
    est = 2 * (m * k * 2 + k * tn * 4 + m * tn * 4) + k * tn * 2
    return pl.pallas_call(
        _mm_castw_kernel,
        grid=(n // tn,),
        in_specs=[pl.BlockSpec((m, k), lambda j: (0, 0)), pl.BlockSpec((None, k, tn), lambda j: (layer, 0, j))],
        out_specs=pl.BlockSpec((m, tn), lambda j: (0, j)),
        out_shape=jax.ShapeDtypeStruct((m, n), F32),
        compiler_params=_cparams(("arbitrary",), est),
    )(x, w)


def _rms_mm_kernel(x_ref, g_ref, w_ref, o_ref):
    x = x_ref[:, :w_ref.shape[0]]
    xn = x * lax.rsqrt(jnp.mean(x * x, axis=-1, keepdims=True) + RMS_EPS) * g_ref[...]
    o_ref[...] = jnp.dot(xn.astype(BF16), w_ref[...], preferred_element_type=F32).astype(o_ref.dtype)


def rms_matmul(h, col0, kb, gain, w, layer, out_dtype, tm=1024, tn=1024):
    m = h.shape[0]
    k, n = w.shape[1], w.shape[2]
    assert col0 % kb == 0 and kb >= k
    tm = _pick_rows(m, tm)
    tn = _pick_tile(n, tn)
    est = 2 * (tm * kb * 4 + k * tn * 2 + tm * tn * 4) + 3 * tm * k * 4 + tm * tn * 4
    return pl.pallas_call(
        _rms_mm_kernel,
        grid=(n // tn, m // tm),
        in_specs=[pl.BlockSpec((tm, kb), lambda j, i: (i, col0 // kb)),
                  pl.BlockSpec((1, k), lambda j, i: (0, 0)),
                  pl.BlockSpec((None, k, tn), lambda j, i: (layer, 0, j))],
        out_specs=pl.BlockSpec((tm, tn), lambda j, i: (i, j)),
        out_shape=jax.ShapeDtypeStruct((m, n), out_dtype),
        compiler_params=_cparams(("arbitrary", "arbitrary"), est),
    )(h, gain, w)


def _out_proj_kernel(a_ref, b_ref, c_ref, d_ref, w_ref, o_ref, wb_scr):
    @pl.when(pl.program_id(1) == 0)
    def _():
        wb_scr[...] = w_ref[...].astype(BF16)

    kq = a_ref.shape[1]
    acc = jnp.dot(a_ref[...], wb_scr[0:kq, :], preferred_element_type=F32)
    for g, r in enumerate((b_ref, c_ref, d_ref), start=1):
        acc = acc + jnp.dot(r[...], wb_scr[g * kq:(g + 1) * kq, :], preferred_element_type=F32)
    o_ref[...] = acc


def out_proj(parts, w, layer, tm=1024, tn=512):
    m, kq = parts[0].shape
    k, n = w.shape[1], w.shape[2]
    assert len(parts) == 4 and 4 * kq == k
    tm = _pick_rows(m, tm)
    tn = _pick_tile(n, tn)
    est = 2 * (4 * tm * kq * 2 + k * tn * 4 + tm * tn * 4) + k * tn * 2 + 2 * tm * tn * 4
    part_spec = pl.BlockSpec((tm, kq), lambda j, i: (i, 0))
    return pl.pallas_call(
        _out_proj_kernel,
        grid=(n // tn, m // tm),
        in_specs=[part_spec] * 4 + [pl.BlockSpec((None, k, tn), lambda j, i: (layer, 0, j))],
        out_specs=pl.BlockSpec((tm, tn), lambda j, i: (i, j)),
        out_shape=jax.ShapeDtypeStruct((m, n), F32),
        scratch_shapes=[pltpu.VMEM((k, tn), BF16)],
        compiler_params=_cparams(("arbitrary", "arbitrary"), est),
    )(*parts, w)


def _swiglu_kernel(x_ref, w1_ref, w3_ref, o_ref, w1b_scr, w3b_scr):
    @pl.when(pl.program_id(1) == 0)
    def _():
        w1b_scr[...] = w1_ref[...].astype(BF16)
        w3b_scr[...] = w3_ref[...].astype(BF16)

    x = x_ref[...]
    a = jnp.dot(x, w1b_scr[...], preferred_element_type=F32)
    b = jnp.dot(x, w3b_scr[...], preferred_element_type=F32)
    o_ref[...] = (a * jax.nn.sigmoid(a) * b).astype(o_ref.dtype)


def swiglu_up(x, w1, w3, layer, tm=512, tn=512):
    m, k = x.shape
    n = w1.shape[-1]
    tm = _pick_rows(m, tm)
    tn = _pick_tile(n, tn)
    w_spec = pl.BlockSpec((None, k, tn), lambda j, i: (layer, 0, j))
    est = 2 * (tm * k * 2 + 2 * k * tn * 4 + tm * tn * 2) + 2 * k * tn * 2 + 3 * tm * tn * 4
    return pl.pallas_call(
        _swiglu_kernel,
        grid=(n // tn, m // tm),
        in_specs=[pl.BlockSpec((tm, k), lambda j, i: (i, 0)), w_spec, w_spec],
        out_specs=pl.BlockSpec((tm, tn), lambda j, i: (i, j)),
        out_shape=jax.ShapeDtypeStruct((m, n), BF16),
        scratch_shapes=[pltpu.VMEM((k, tn), BF16), pltpu.VMEM((k, tn), BF16)],
        compiler_params=_cparams(("arbitrary", "arbitrary"), est),
    )(x, w1, w3)


def _down_kernel(x_ref, w_ref, o_ref):
    p = jnp.dot(x_ref[...], w_ref[...], preferred_element_type=F32)

    @pl.when(pl.program_id(2) == 0)
    def _():
        o_ref[...] = p

    @pl.when(pl.program_id(2) > 0)
    def _():
        o_ref[...] += p


def down_proj(x, w, layer, tm=2048, tn=1024, tk=1792):
    m, k = x.shape
    n = w.shape[-1]
    tm = _pick_rows(m, tm)
    tn = _pick_tile(n, tn)
    tk = _pick_tile(k, tk)
    est = 2 * (tm * tk * 2 + tk * tn * 2 + tm * tn * 4) + tm * tn * 4
    return pl.pallas_call(
        _down_kernel,
        grid=(m // tm, n // tn, k // tk),
        in_specs=[pl.BlockSpec((tm, tk), lambda i, j, s: (i, s)),
                  pl.BlockSpec((None, tk, tn), lambda i, j, s: (layer, s, j))],
        out_specs=pl.BlockSpec((tm, tn), lambda i, j, s: (i, j)),
        out_shape=jax.ShapeDtypeStruct((m, n), F32),
        compiler_params=_cparams(("arbitrary", "arbitrary", "arbitrary"), est),
    )(x, w)


def _moe_up_kernel(rt_ref, nt_ref, ex_ref, ok_ref, new_ref, x_ref, w1_ref, w3_ref, o_ref, w1b_scr, w3b_scr):
    @pl.when(new_ref[pl.program_id(0)] == 1)
    def _():
        w1b_scr[...] = w1_ref[...].astype(BF16)
        w3b_scr[...] = w3_ref[...].astype(BF16)

    @pl.when(ok_ref[pl.program_id(0)] == 1)
    def _():
        x = x_ref[...]
        a = jnp.dot(x, w1b_scr[...], preferred_element_type=F32)
        b = jnp.dot(x, w3b_scr[...], preferred_element_type=F32)
        o_ref[...] = (a * jax.nn.sigmoid(a) * b).astype(o_ref.dtype)


def _moe_down_kernel(rt_ref, nt_ref, ex_ref, ok_ref, x_ref, w_ref, g_ref, o_ref):
    @pl.when(ok_ref[pl.program_id(0)] == 1)
    def _():
        o_ref[...] = jnp.dot(x_ref[...], w_ref[...], preferred_element_type=F32) * g_ref[...]


def _moe_items(tiles_e, tile0_e, nj, n_tiles):
    n_items = n_tiles * nj
    per_e = tiles_e * nj
    end_e = jnp.cumsum(per_e)
    total = end_e[-1]
    pos = jnp.arange(n_items, dtype=jnp.int32)
    idx = jnp.minimum(pos, total - 1)
    ex = jnp.searchsorted(end_e, idx, side="right").astype(jnp.int32)
    local = idx - (end_e - per_e)[ex]
    te = jnp.maximum(tiles_e[ex], 1)
    rt = tile0_e[ex] + local % te
    nt = local // te
    ok = pos < total
    new = ok & (local % te == 0)
    return rt.astype(jnp.int32), nt.astype(jnp.int32), ex, ok.astype(jnp.int32), new.astype(jnp.int32)


def moe_experts(u, logits, w1, w3, w2, group, n_exp):
    n, d = u.shape
    f = w1.shape[-1]
    tm = min(MOE_ROWS, n)
    top_v, top_i = lax.top_k(logits, TOP_K)
    wts = jax.nn.softmax(top_v, axis=-1)
    n_pairs = n * TOP_K
    n_rows = -(-n_pairs // tm) * tm + n_exp * tm
    n_tiles = n_rows // tm

    e_flat = top_i.reshape(-1).astype(jnp.int32)
    order = jnp.argsort(e_flat, stable=True)
    cnt = jnp.sum(jax.nn.one_hot(e_flat, n_exp, dtype=jnp.int32), axis=0)
    tiles_e = (cnt + tm - 1) // tm
    tile0_e = jnp.cumsum(tiles_e) - tiles_e
    start_e = jnp.cumsum(cnt) - cnt
    e_sorted = e_flat[order]
    dest_sorted = tile0_e[e_sorted] * tm + jnp.arange(n_pairs, dtype=jnp.int32) - start_e[e_sorted]
    dest = dest_sorted[jnp.argsort(order)]
    rows = jnp.arange(n_rows, dtype=jnp.int32)
    e_row = jnp.minimum(jnp.searchsorted((tile0_e + tiles_e) * tm, rows, side="right"), n_exp - 1).astype(jnp.int32)
    r_in = rows - tile0_e[e_row] * tm
    live = (r_in >= 0) & (r_in < cnt[e_row])
    pair = order[jnp.clip(start_e[e_row] + r_in, 0, n_pairs - 1)]
    row_token = jnp.where(live, pair // TOP_K, 0).astype(jnp.int32)
    row_gate = jnp.where(live, wts.reshape(-1)[pair], 0.0)

    xg = jnp.take(u, row_token, axis=0, mode="clip")

    tn = _pick_tile(f, 512)
    items = _moe_items(tiles_e, tile0_e, f // tn, n_tiles)
    w_spec = pl.BlockSpec((None, d, tn), lambda i, rt, nt, ex, ok, new: (group * n_exp + ex[i], 0, nt[i]))
    est = 2 * (tm * d * 2 + 2 * d * tn * 4 + tm * tn * 2) + 2 * d * tn * 2 + 3 * tm * tn * 4
    act = pl.pallas_call(
        _moe_up_kernel,
        grid_spec=pltpu.PrefetchScalarGridSpec(
            num_scalar_prefetch=5, grid=(n_tiles * (f // tn),),
            in_specs=[pl.BlockSpec((tm, d), lambda i, rt, nt, ex, ok, new: (rt[i], 0)), w_spec, w_spec],
            out_specs=pl.BlockSpec((tm, tn), lambda i, rt, nt, ex, ok, new: (rt[i], nt[i])),
            scratch_shapes=[pltpu.VMEM((d, tn), BF16), pltpu.VMEM((d, tn), BF16)]),
        out_shape=jax.ShapeDtypeStruct((n_rows, f), BF16),
        compiler_params=_cparams(("arbitrary",), est),
    )(*items, xg, w1, w3)

    tn2 = _pick_tile(d, 512)
    items2 = _moe_items(tiles_e, tile0_e, d // tn2, n_tiles)[:4]
    est = 2 * (tm * f * 2 + f * tn2 * 2 + tm * tn2 * 4 + tm * LANE * 4) + tm * tn2 * 4
    y = pl.pallas_call(
        _moe_down_kernel,
        grid_spec=pltpu.PrefetchScalarGridSpec(
            num_scalar_prefetch=4, grid=(n_tiles * (d // tn2),),
            in_specs=[pl.BlockSpec((tm, f), lambda i, rt, nt, ex, ok: (rt[i], 0)),
                      pl.BlockSpec((None, f, tn2), lambda i, rt, nt, ex, ok: (group * n_exp + ex[i], 0, nt[i])),
                      pl.BlockSpec((tm, 1), lambda i, rt, nt, ex, ok: (rt[i], 0))],
            out_specs=pl.BlockSpec((tm, tn2), lambda i, rt, nt, ex, ok: (rt[i], nt[i]))),
        out_shape=jax.ShapeDtypeStruct((n_rows, d), F32),
        compiler_params=_cparams(("arbitrary",), est),
    )(*items2, act, w2, row_gate[:, None])

    dest = dest.reshape(n, TOP_K)
    return jnp.take(y, dest[:, 0], axis=0, mode="clip"), jnp.take(y, dest[:, 1], axis=0, mode="clip")


def _modulate_kernel(x_ref, scale_ref, shift_ref, u_ref):
    u_ref[...] = (x_ref[...] * (1.0 + scale_ref[0, 0]) + shift_ref[0, 0]).astype(u_ref.dtype)


def _ln_kernel(*refs, alpha, n_y, with_next):
    x_ref = refs[0]
    y = refs[1][...]
    if n_y == 2:
        y = y + refs[2][...]
    gate_ref, g_ref, b_ref = refs[1 + n_y:4 + n_y]
    z = alpha * x_ref[...] + gate_ref[0, 0] * y
    zc = z - jnp.mean(z, axis=-1, keepdims=True)
    yn = zc * lax.rsqrt(jnp.mean(zc * zc, axis=-1, keepdims=True) + LN_EPS)
    xn = yn * g_ref[...] + b_ref[...]
    if with_next:
        scale_ref, shift_ref, xo_ref, uo_ref = refs[4 + n_y:]
        xo_ref[...] = xn
        uo_ref[...] = (xn * (1.0 + scale_ref[0, 0]) + shift_ref[0, 0]).astype(uo_ref.dtype)
    else:
        refs[4 + n_y][...] = xn


class _Tokens:
    def __init__(self, n_ctx, t_lat, n_lat_batches, d):
        self.n_ctx, self.t_lat, self.d = n_ctx, t_lat, d
        self.n = n_ctx + t_lat * n_lat_batches
        tm = 256
        while n_ctx % tm or t_lat % tm:
            tm //= 2
        self.tm = tm
        self.ctx_tiles = n_ctx // tm
        self.tiles_per_lat = t_lat // tm

    def mod_row(self, i):
        return jnp.where(i < self.ctx_tiles, 0, 1 + (i - self.ctx_tiles) // self.tiles_per_lat)

    def mod_spec(self, idx):
        return pl.BlockSpec((1, 1, 1, self.d), lambda i: (idx, self.mod_row(i), 0, 0))

    def row_spec(self):
        return pl.BlockSpec((self.tm, self.d), lambda i: (i, 0))

    def vec_spec(self, idx):
        return pl.BlockSpec((None, 1, self.d), lambda i: (idx, 0, 0))


def modulate(tok, x, mod, idx_scale, idx_shift):
    est = 2 * tok.tm * tok.d * 6 + 4 * tok.tm * tok.d * 4
    return pl.pallas_call(
        _modulate_kernel,
        grid=(tok.n // tok.tm,),
        in_specs=[tok.row_spec(), tok.mod_spec(idx_scale), tok.mod_spec(idx_shift)],
        out_specs=tok.row_spec(),
        out_shape=jax.ShapeDtypeStruct((tok.n, tok.d), BF16),
        compiler_params=_cparams(("arbitrary",), est),
    )(x, mod, mod)


def residual_ln(tok, x, y, mod, idx_gate, ln_gb, idx_g, idx_b, alpha, idx_scale=None, idx_shift=None):
    ys = list(y) if isinstance(y, (tuple, list)) else [y]
    est = 2 * tok.tm * tok.d * (4 + 4 * len(ys) + 4 + 2) + 6 * tok.tm * tok.d * 4
    in_specs = ([tok.row_spec()] * (1 + len(ys))
                + [tok.mod_spec(idx_gate), tok.vec_spec(idx_g), tok.vec_spec(idx_b)])
    args = [x] + ys + [mod, ln_gb, ln_gb]
    if idx_scale is None:
        return pl.pallas_call(
            functools.partial(_ln_kernel, alpha=alpha, n_y=len(ys), with_next=False),
            grid=(tok.n // tok.tm,),
            in_specs=in_specs,
            out_specs=tok.row_spec(),
            out_shape=jax.ShapeDtypeStruct((tok.n, tok.d), F32),
            compiler_params=_cparams(("arbitrary",), est),
        )(*args), None
    in_specs += [tok.mod_spec(idx_scale), tok.mod_spec(idx_shift)]
    args += [mod, mod]
    return pl.pallas_call(
        functools.partial(_ln_kernel, alpha=alpha, n_y=len(ys), with_next=True),
        grid=(tok.n // tok.tm,),
        in_specs=in_specs,
        out_specs=[tok.row_spec(), tok.row_spec()],
        out_shape=[jax.ShapeDtypeStruct((tok.n, tok.d), F32), jax.ShapeDtypeStruct((tok.n, tok.d), BF16)],
        compiler_params=_cparams(("arbitrary",), est),
    )(*args)


ATTN_Q_ROWS = 512


def _softmax_pv(scores, vals):
    m = jnp.max(scores[0], axis=-1, keepdims=True)
    for s in scores[1:]:
        m = jnp.maximum(m, jnp.max(s, axis=-1, keepdims=True))
    acc = None
    den = None
    for s, v in zip(scores, vals):
        p = jnp.exp(s - m)
        ps = jnp.sum(p, axis=-1, keepdims=True)
        pv = jnp.dot(p.astype(BF16), v, preferred_element_type=F32)
        acc = pv if acc is None else acc + pv
        den = ps if den is None else den + ps
    return acc / den


def _rope_lanes(x, cos, sin_signed, n):
    lane = lax.broadcasted_iota(jnp.int32, x.shape, 1)
    first = ((lane // n) % 2) == 0
    partner = jnp.where(first, pltpu.roll(x, LANE - n, 1), pltpu.roll(x, n, 1))
    return x * cos + partner * sin_signed


def _rms_rows(x, g):
    return x * lax.rsqrt(jnp.mean(x * x, axis=-1, keepdims=True) + RMS_EPS) * g


def _attn_refs(refs, n_main, use_rope, has_cache, n_cache, aliased):
    it = iter(refs)
    main = [next(it) for _ in range(n_main)]
    rope = (next(it), next(it)) if use_rope else (None, None)
    cache = [next(it) for _ in range(n_cache)] if has_cache else [None] * n_cache
    if aliased:
        next(it)
    return main, rope, cache, next(it)


def _gqa_fused_kernel(*refs, hp, kv_group, t, scale, n_rot, use_rope, has_cache, aliased):
    (q_ref, k_ref, v_ref, gq_ref, gk_ref), (cos_ref, sin_ref), (ck_ref, cv_ref), o_ref = _attn_refs(
        refs, 5, use_rope, has_cache, 2, aliased)
    tq = min(ATTN_Q_ROWS, t)
    n_kv = max(1, hp // kv_group)
    keys, vals = [], []
    for j in range(n_kv):
        sl = slice(j * LANE, (j + 1) * LANE)
        k = _rms_rows(k_ref[:, sl], gk_ref[...])
        if use_rope:
            k = _rope_lanes(k, cos_ref[...], sin_ref[...], n_rot)
        kj, vj = [k.astype(BF16)], [v_ref[:, sl].astype(BF16)]
        if has_cache:
            kj.insert(0, ck_ref[:, sl].astype(BF16))
            vj.insert(0, cv_ref[:, sl].astype(BF16))
        keys.append(kj)
        vals.append(vj)
    for hh in range(hp):
        j = hh // kv_group if n_kv > 1 else 0
        for r0 in range(0, t, tq):
            q = _rms_rows(q_ref[r0:r0 + tq, hh * LANE:(hh + 1) * LANE], gq_ref[...])
            if use_rope:
                q = _rope_lanes(q, cos_ref[r0:r0 + tq, :], sin_ref[r0:r0 + tq, :], n_rot)
            qb = q.astype(BF16)
            o = _softmax_pv([_dot_nt(qb, k) * scale for k in keys[j]], vals[j])
            o_ref[r0:r0 + tq, hh * LANE:(hh + 1) * LANE] = o.astype(o_ref.dtype)


def _diff_fused_kernel(*refs, hp, t, scale, dd, n_rot, use_rope, has_cache, aliased):
    (q_ref, k_ref, v_ref, lam_ref, g_ref), (cos_ref, sin_ref), (ck_ref, cv_ref), o_ref = _attn_refs(
        refs, 5, use_rope, has_cache, 2, aliased)
    tq = min(ATTN_Q_ROWS, t)
    for hh in range(hp):
        sl = slice(hh * LANE, (hh + 1) * LANE)
        k = k_ref[:, sl]
        if use_rope:
            k = _rope_lanes(k, cos_ref[...], sin_ref[...], n_rot)
        keys, vals = [k.astype(BF16)], [v_ref[:, sl].astype(BF16)]
        if has_cache:
            keys.insert(0, ck_ref[:, sl].astype(BF16))
            vals.insert(0, cv_ref[:, sl].astype(BF16))
        for r0 in range(0, t, tq):
            q = q_ref[r0:r0 + tq, sl]
            if use_rope:
                q = _rope_lanes(q, cos_ref[r0:r0 + tq, :], sin_ref[r0:r0 + tq, :], n_rot)
            lane = lax.broadcasted_iota(jnp.int32, q.shape, 1)
            q1 = jnp.where(lane < dd, q, 0.0).astype(BF16)
            q2 = jnp.where(lane < dd, 0.0, q).astype(BF16)
            a1 = _softmax_pv([_dot_nt(q1, kk) * scale for kk in keys], vals)
            a2 = _softmax_pv([_dot_nt(q2, kk) * scale for kk in keys], vals)
            d = a1 - lam_ref[...] * a2
            o_ref[r0:r0 + tq, sl] = _rms_rows(d, g_ref[...]).astype(o_ref.dtype)


def _mla_fused_kernel(*refs, hp, t, scale, rope_d, n_rot, use_rope, has_cache, aliased):
    (q_ref, kv_ref, kr_ref), (cos_ref, sin_ref), (kvc_ref, krc_ref), o_ref = _attn_refs(
        refs, 3, use_rope, has_cache, 2, aliased)
    tq = min(ATTN_Q_ROWS, t)
    hw = 2 * LANE
    kr = kr_ref[...]
    if use_rope:
        kr = _rope_lanes(kr, cos_ref[...], sin_ref[...], n_rot)
    krs = [kr[:, :rope_d].astype(BF16)]
    if has_cache:
        krs.insert(0, krc_ref[...].astype(BF16))
    for hh in range(hp):
        c0 = hh * hw
        kns, vals = [kv_ref[:, c0:c0 + LANE]], [kv_ref[:, c0 + LANE:c0 + hw]]
        if has_cache:
            kns.insert(0, kvc_ref[:, c0:c0 + LANE])
            vals.insert(0, kvc_ref[:, c0 + LANE:c0 + hw])
        for r0 in range(0, t, tq):
            qn = q_ref[r0:r0 + tq, c0:c0 + LANE].astype(BF16)
            qr = q_ref[r0:r0 + tq, c0 + LANE:c0 + hw]
            if use_rope:
                qr = _rope_lanes(qr, cos_ref[r0:r0 + tq, :], sin_ref[r0:r0 + tq, :], n_rot)
            qr = qr[:, :rope_d].astype(BF16)
            scores = [(_dot_nt(qn, kn) + _dot_nt(qr, kk)) * scale for kn, kk in zip(kns, krs)]
            o_ref[r0:r0 + tq, hh * LANE:(hh + 1) * LANE] = _softmax_pv(scores, vals).astype(o_ref.dtype)


def _attn_launch(body, n_rows, heads, hp, row0, b, t, main, rope, cache, prev, width_est):
    rb0 = row0 // t
    in_specs = [s for _, s in main]
    args = [a for a, _ in main]
    if rope is not None:
        in_specs += [pl.BlockSpec((t, LANE), lambda bi, hb: (0, 0))] * 2
        args += list(rope)
    in_specs += [s for _, s in cache]
    args += [a for a, _ in cache]
    aliases = {}
    if prev is not None:
        aliases = {len(args): 0}
        in_specs.append(pl.BlockSpec(memory_space=pl.ANY))
        args.append(prev)
    n_cached = cache[1][0].shape[2] if cache else 0
    est = 2 * width_est + 8 * min(ATTN_Q_ROWS, t) * (t + n_cached) * 4 + (4 << 20)
    return pl.pallas_call(
        body,
        grid=(b, heads // hp),
        in_specs=in_specs,
        out_specs=pl.BlockSpec((t, hp * LANE), lambda bi, hb: (rb0 + bi, hb)),
        out_shape=jax.ShapeDtypeStruct((n_rows, heads * LANE), BF16),
        input_output_aliases=aliases,
        compiler_params=_cparams(("arbitrary", "arbitrary"), est),
    )(*args)


def _cache_kernel(*refs, n_kv, rope_d, aliased):
    ckv_ref, g_ckv_ref, kr_ref, bk_ref, bv_ref, gk_ref, g_gk_ref, gv_ref = refs[:8]
    o_ckv, o_kr, o_dk, o_dv, o_gk, o_gv = refs[-6:]
    o_ckv[...] = _rms_rows(ckv_ref[...], g_ckv_ref[...])
    o_kr[...] = kr_ref[:, :rope_d]
    o_dk[...] = bk_ref[...]
    o_dv[...] = bv_ref[...]
    for j in range(n_kv):
        sl = slice(j * LANE, (j + 1) * LANE)
        o_gk[:, sl] = _rms_rows(gk_ref[:, sl], g_gk_ref[...])
    o_gv[...] = gv_ref[...]


def write_caches(h, col, l, depth, b, t, kv_rank, rope_d, unit, kvw, g_ckv, g_gk, prev):
    def hcols(c0, width):
        return pl.BlockSpec((t, width), lambda bi: (bi, c0 // width))

    def vec(n):
        return pl.BlockSpec((1, n), lambda bi: (0, 0))

    widths = (kv_rank, rope_d, unit, unit, kvw, kvw)
    in_specs = [hcols(col["ckv"], kv_rank), vec(kv_rank), hcols(col["kr"], LANE), hcols(col["bk"], unit),
                hcols(col["bv"], unit), hcols(col["gk"], kvw), vec(LANE), hcols(col["gv"], kvw)]
    args = [h, g_ckv, h, h, h, h, g_gk, h]
    aliases = {}
    if prev is not None:
        aliases = {len(args) + i: i for i in range(6)}
        in_specs += [pl.BlockSpec(memory_space=pl.ANY)] * 6
        args += list(prev)
    est = 4 * t * (kv_rank + LANE + 2 * unit + 2 * kvw) * 4 * 2 + (4 << 20)
    return pl.pallas_call(
        functools.partial(_cache_kernel, n_kv=kvw // LANE, rope_d=rope_d, aliased=prev is not None),
        grid=(b,),
        in_specs=in_specs,
        out_specs=[pl.BlockSpec((None, None, t, w), lambda bi: (bi, l, 0, 0)) for w in widths],
        out_shape=[jax.ShapeDtypeStruct((b, depth, t, w), F32) for w in widths],
        input_output_aliases=aliases,
        compiler_params=_cparams(("arbitrary",), est),
    )(*args)


def _hgrn_kernel(*refs, t, nb, chunk, sub, q_scale, has_s0, want_state, aliased):
    it = iter(refs)
    dq_ref, di_ref, df_ref, dfb_ref, dg_ref, tab_ref = (next(it) for _ in range(6))
    s0_ref = next(it) if has_s0 else None
    for _ in range(aliased):
        next(it)
    o_ref = next(it)
    st_out_ref = next(it) if want_state else None
    st_scr, o_scr, cum_scr, k_scr, q_scr = (next(it) for _ in range(5))

    n_chunks = t // chunk
    n_sub = chunk // sub
    half = sub // 2
    half_row = lax.broadcasted_iota(jnp.int32, (half, 1), 0)
    row = lax.broadcasted_iota(jnp.int32, (chunk, chunk), 0)
    col = lax.broadcasted_iota(jnp.int32, (chunk, chunk), 1)
    key_row = lax.broadcasted_iota(jnp.int32, (chunk, 1), 0)
    neg_inf = jnp.float32(-jnp.inf)
    tab = tab_ref[...]
    tri_f = jnp.where(row >= col, 1.0, 0.0).astype(F32)
    tri_b = jnp.where(row <= col, 1.0, 0.0).astype(F32)
    same_sub = (row // sub) == (col // sub)

    for bb in range(nb):
        for direction in range(2):
            if has_s0:
                st_scr[2 * bb + direction] = s0_ref[bb, direction, 0].T
            else:
                st_scr[2 * bb + direction] = jnp.zeros(st_scr.shape[1:], F32)

    def prepare(ci, carry):
        r0 = pl.multiple_of(ci * chunk, chunk)
        zq = dq_ref[pl.ds(r0, chunk), :]
        q_scr[pl.ds(r0, chunk), :] = zq * jax.nn.sigmoid(zq) * q_scale
        for direction, z_ref in enumerate((df_ref, dfb_ref)):
            log_lb = tab[3 * direction:3 * direction + 1, :]
            log_1m = tab[3 * direction + 1:3 * direction + 2, :]
            one_m = tab[3 * direction + 2:3 * direction + 3, :]
            z = z_ref[pl.ds(r0, chunk), :]
            e = jnp.exp(-jnp.abs(z))
            d1 = 1.0 + e
            t2 = log_1m + (jnp.minimum(z, 0.0) - jnp.log(d1))
            g = jnp.maximum(log_lb, t2) + jnp.log(1.0 + jnp.exp(-jnp.abs(log_lb - t2)))
            k_scr[direction, pl.ds(r0, chunk), :] = one_m * (jnp.where(z >= 0.0, e, 1.0) / d1)
            cum_scr[direction, pl.ds(r0, chunk), :] = jnp.dot(
                tri_b if direction else tri_f, g, precision=lax.Precision.HIGHEST, preferred_element_type=F32)
        return carry

    lax.fori_loop(0, nb * n_chunks, prepare, 0, unroll=2)

    def chain_step(ci, bb, direction):
        rev = direction == 1
        r0 = pl.multiple_of(bb * t + ((n_chunks - 1 - ci) if rev else ci) * chunk, chunk)
        q = q_scr[pl.ds(r0, chunk), :]
        k = k_scr[direction, pl.ds(r0, chunk), :]
        cum = cum_scr[direction, pl.ds(r0, chunk), :]
        v = di_ref[pl.ds(r0, chunk), :]
        edge = cum[0:1, :] if rev else cum[chunk - 1:chunk, :]
        slot = 2 * bb + direction
        st = st_scr[slot]
        vb = v.astype(BF16)
        inter = _dot_nt((q * jnp.exp(cum)).astype(BF16), st.astype(BF16))

        mid = [cum[i * sub + half:i * sub + half + 1, :] if rev else cum[i * sub + half - 1:i * sub + half, :]
               for i in range(n_sub)]
        base2 = jnp.concatenate([jnp.broadcast_to(m, (sub, LANE)) for m in mid], axis=0)
        late = (key_row % sub) >= half
        q_side, k_side = (~late, late) if rev else (late, ~late)
        q2 = (q * jnp.exp(jnp.where(q_side, cum - base2, neg_inf))).astype(BF16)
        k2 = (k * jnp.exp(jnp.where(k_side, base2 - cum, neg_inf))).astype(BF16)
        att2 = jnp.where(same_sub, _dot_nt(q2, k2), 0.0)
        near = inter + jnp.dot(att2.astype(BF16), vb, preferred_element_type=F32)

        outs = []
        for i in range(n_sub):
            lo = i * sub
            acc = near[lo:lo + sub]
            if (i < n_sub - 1) if rev else (i > 0):
                base = cum[lo + sub:lo + sub + 1, :] if rev else cum[lo - 1:lo, :]
                seen = (key_row >= lo + sub) if rev else (key_row < lo)
                qn = (q[lo:lo + sub] * jnp.exp(cum[lo:lo + sub] - base)).astype(BF16)
                kn = (k * jnp.exp(jnp.where(seen, base - cum, neg_inf))).astype(BF16)
                att = _dot_nt(qn, kn)
                acc = acc + jnp.dot(att.astype(BF16), vb, preferred_element_type=F32)
            for hb in range(sub // half):
                lo8 = lo + hb * half
                q8 = q[lo8:lo8 + half]
                cum8 = cum[lo8:lo8 + half]
                acc8 = acc[hb * half:(hb + 1) * half]
                for s in range(half):
                    r = lo8 + s
                    live = (half_row <= s) if rev else (half_row >= s)
                    w = jnp.exp(jnp.where(live, cum8 - cum_scr[direction, pl.ds(r0 + r, 1), :], neg_inf))
                    colv = jnp.sum(q8 * k_scr[direction, pl.ds(r0 + r, 1), :] * w, axis=-1, keepdims=True)
                    acc8 = acc8 + colv * di_ref[pl.ds(r0 + r, 1), :]
                outs.append(acc8)
        o_scr[direction, pl.ds(r0, chunk), :] = jnp.concatenate(outs, axis=0)
        kl = (k * jnp.exp(edge - cum)).astype(BF16)
        st_scr[2 * bb + direction] = st * jnp.exp(edge) + _dot_tn(vb, kl)

    def body(ci, carry):
        for bb in range(nb):
            for direction in range(2):
                chain_step(ci, bb, direction)
        return carry

    lax.fori_loop(0, n_chunks, body, 0)

    tot = o_scr[0] + o_scr[1]
    zg = dg_ref[...]
    y = tot * lax.rsqrt(jnp.mean(tot * tot, axis=-1, keepdims=True) + RMS_EPS) * tab[6:7, :]
    o_ref[...] = (y * (zg * jax.nn.sigmoid(zg))).astype(o_ref.dtype)
    if want_state:
        for bb in range(nb):
            for direction in range(2):
                st_out_ref[bb, direction, 0] = st_scr[2 * bb + direction].T


def hgrn_mix(h, row0, b, t, heads, dk, dv, cols, tab, s0, state_into, prev):
    assert dk == LANE and dv == LANE
    nb = HGRN_BATCHES if (b % HGRN_BATCHES == 0 and row0 % (HGRN_BATCHES * t) == 0) else 1
    assert row0 % (nb * t) == 0 and all(cc % LANE == 0 for cc in cols)
    chunk = min(HGRN_CHUNK, t)
    sub = min(HGRN_SUB, chunk)
    rb0 = row0 // (nb * t)

    def col_spec(c0):
        return pl.BlockSpec((nb * t, LANE), lambda bi, hh: (rb0 + bi, c0 // LANE + hh))

    st_spec = pl.BlockSpec((nb, 2, 1, dk, dv), lambda bi, hh: (bi, 0, hh, 0, 0))
    in_specs = [col_spec(cc) for cc in cols] + [pl.BlockSpec((8, LANE), lambda bi, hh: (0, hh))]
    args = [h] * len(cols) + [tab]
    if s0 is not None:
        in_specs.append(st_spec)
        args.append(s0)
    want_state = state_into is not None
    aliases = {}
    if prev is not None:
        aliases[len(args)] = 0
        in_specs.append(pl.BlockSpec(memory_space=pl.ANY))
        args.append(prev)
    if want_state and state_into[2] is not None:
        aliases[len(args)] = 1
        in_specs.append(pl.BlockSpec(memory_space=pl.ANY))
        args.append(state_into[2])
    out_specs = [pl.BlockSpec((nb * t, LANE), lambda bi, hh: (rb0 + bi, hh))]
    out_shape = [jax.ShapeDtypeStruct((h.shape[0], heads * dv), BF16)]
    if want_state:
        layer, depth = state_into[0], state_into[1]
        out_specs.append(pl.BlockSpec((nb, None, 2, 1, dk, dv), lambda bi, hh: (bi, layer, 0, hh, 0, 0)))
        out_shape.append(jax.ShapeDtypeStruct((b, depth, 2, heads, dk, dv), F32))
    est = 2 * nb * t * LANE * (5 * 4 + 2) + 7 * nb * t * LANE * 4 + (8 * nb + 6) * dk * dv * 4 + (4 << 20)
    res = pl.pallas_call(
        functools.partial(_hgrn_kernel, t=t, nb=nb, chunk=chunk, sub=sub, q_scale=dk ** -0.5,
                          has_s0=s0 is not None, want_state=want_state, aliased=len(aliases)),
        grid=(b // nb, heads),
        in_specs=in_specs,
        out_specs=out_specs,
        out_shape=out_shape,
        input_output_aliases=aliases,
        scratch_shapes=[pltpu.VMEM((2 * nb, dv, dk), F32), pltpu.VMEM((2, nb * t, LANE), F32),
                        pltpu.VMEM((2, nb * t, LANE), F32), pltpu.VMEM((2, nb * t, LANE), F32),
                        pltpu.VMEM((nb * t, LANE), F32)],
        compiler_params=_cparams(("arbitrary", "arbitrary"), est),
    )(*args)
    return (res[0], res[1]) if want_state else (res[0], None)


def _rope_lane_tables(t, d):
    n = d // 4
    inv = ROPE_THETA ** (-jnp.arange(n, dtype=F32) / n)
    pos = jnp.arange(t)
    rows = (pos // GRID_W).astype(F32)
    cols = (pos % GRID_W).astype(F32)
    ar = rows[:, None] * inv[None, :]
    ac = cols[:, None] * inv[None, :]
    cos = jnp.concatenate([jnp.cos(ar), jnp.cos(ar), jnp.cos(ac), jnp.cos(ac)], axis=-1)
    sin = jnp.concatenate([-jnp.sin(ar), jnp.sin(ar), -jnp.sin(ac), jnp.sin(ac)], axis=-1)
    reps = LANE // d
    return jnp.tile(cos, (1, reps)), jnp.tile(sin, (1, reps))


def kernel(x_prompt, x_sample, cache_mla_ckv, cache_mla_krope, cache_diff_k, cache_diff_v, cache_gqa_k, cache_gqa_v, state_hgrn, c, c_ctx, w_in, w_out, mla_g_cq, mla_g_ckv, mla_w_uq, mla_w_ukv, diff_lambda, diff_subln_g, gqa_g_q, gqa_g_k, hgrn_gamma, hgrn_g_norm, ada_w, ada_b, ln_g, ln_b, ffn_w1, ffn_w3, ffn_w2, moe_router, moe_w1, moe_w3, moe_w2):
    bc, tc, d = x_prompt.shape
    bl, tl, _ = x_sample.shape
    depth = w_in.shape[0]
    q_rank = mla_g_cq.shape[1]
    kv_rank = mla_g_ckv.shape[1]
    rope_d = cache_mla_krope.shape[-1]
    heads = cache_diff_k.shape[3]
    dd = cache_diff_k.shape[4] // 2
    kv_heads = cache_gqa_k.shape[3]
    gd = cache_gqa_k.shape[4]
    hk = state_hgrn.shape[4]
    hv = state_hgrn.shape[5]
    n_exp = moe_w1.shape[1]
    alpha = (2 * depth) ** 0.25
    n_ctx = bc * tc
    n_lat = bl * tl
    tok = _Tokens(n_ctx, tl, bl, d)
    mla_hw = 2 * LANE
    assert MLA_NOPE_DIM + rope_d <= mla_hw and MLA_NOPE_DIM + MLA_V_DIM == mla_hw

    unit = heads * LANE
    kvw = kv_heads * gd
    assert 2 * dd == LANE and gd == LANE and hk == LANE and hv == LANE
    names = ("cq", "ckv", "kr", "bq", "bk", "bv", "gq", "gk", "gv", "dq", "di", "df", "dfb", "dg")
    widths = (q_rank, kv_rank, rope_d, unit, unit, unit, unit, kvw, kvw, unit, unit, unit, unit, unit)
    assert sum(widths) == w_in.shape[2]
    src, o = {}, 0
    for nm, w in zip(names, widths):
        src[nm] = (o, w)
        o += w
    col, pieces, o = {}, [], 0

    def place(nm, pad_to=None):
        nonlocal o
        if nm is not None:
            s0, w = src[nm]
            col[nm] = o
            pieces.append((s0, w))
            o += w
        if pad_to is not None and o % pad_to:
            z = pad_to - o % pad_to
            pieces.append((None, z))
            o += z

    place("cq")
    place("gk", pad_to=unit)
    place("ckv")
    place("gv")
    place("kr", pad_to=LANE)
    place(None, pad_to=unit)
    for nm in ("bq", "bk", "bv", "gq", "dq", "di", "df", "dfb", "dg"):
        place(nm)
    w_in_b = regroup_columns(w_in, tuple(pieces))
    assert q_rank <= unit and col["gk"] % kvw == 0 and col["gv"] % kvw == 0 and col["ckv"] % kv_rank == 0
    w_uq_b = jnp.pad(mla_w_uq.reshape(depth, q_rank, heads, MLA_NOPE_DIM + rope_d),
                     ((0, 0), (0, 0), (0, 0), (0, mla_hw - MLA_NOPE_DIM - rope_d))
                     ).reshape(depth, q_rank, heads * mla_hw).astype(BF16)
    w_ukv_b = mla_w_ukv.astype(BF16)
    ffn_w2_b = ffn_w2.astype(BF16)
    e_ff = moe_w1.shape[-1]
    n_moe = moe_w1.shape[0]
    moe_w1_r = moe_w1.reshape(n_moe * n_exp, d, e_ff)
    moe_w3_r = moe_w3.reshape(n_moe * n_exp, d, e_ff)
    moe_w2_b = moe_w2.astype(BF16).reshape(n_moe * n_exp, e_ff, d)
    router_b = jnp.pad(moe_router, ((0, 0), (0, 0), (0, LANE - n_exp))).astype(BF16)

    cond = jnp.concatenate([c_ctx[None, :], c], axis=0)
    n_cond = cond.shape[0]
    cond_p = jnp.pad(jax.nn.silu(cond), ((0, -n_cond % 16), (0, 0))).astype(BF16)
    mods = []
    for l in range(depth):
        ml = matmul_few_rows(cond_p, ada_w, l)[:n_cond] + ada_b[l][None, :]
        mods.append(ml.reshape(n_cond, 6, d).transpose(1, 0, 2))
    mod = jnp.concatenate(mods, axis=0).reshape(depth * 6, n_cond, 1, d)
    ln_gb = jnp.concatenate([ln_g.reshape(depth * 2, 1, d), ln_b.reshape(depth * 2, 1, d)], axis=0)

    def lower_bounds(gamma):
        cs = jnp.cumsum(jax.nn.softmax(gamma.astype(F32), axis=0), axis=0)
        return cs - cs[0:1]

    lb_f = lower_bounds(hgrn_gamma[0])
    lb_b = lower_bounds(hgrn_gamma[1])
    n_rot_small, n_rot_big = dd // 4, gd // 4
    assert rope_d == dd
    rope_small = _rope_lane_tables(tl, dd)
    rope_big = _rope_lane_tables(tl, gd)
    n_tok = n_ctx + n_lat
    past = cache_mla_ckv.shape[2]
    cache_dk = cache_diff_k.reshape(bl, depth, past, unit)
    cache_dv = cache_diff_v.reshape(bl, depth, past, unit)
    cache_gk = cache_gqa_k.reshape(bl, depth, past, kvw)
    cache_gv = cache_gqa_v.reshape(bl, depth, past, kvw)
    hgrn_cols = [col[nm] for nm in ("dq", "di", "df", "dfb", "dg")]

    def mixers(h, q_all, kv_new, kv_cache, l, lat, prevs, new_states=None):
        row0, b, t = (n_ctx, bl, tl) if lat else (0, bc, tc)
        hp = 1 if lat else heads
        rb0 = row0 // t
        flags = dict(use_rope=lat, has_cache=lat, aliased=prevs is not None)
        pa, pb, pc, pd = prevs if prevs is not None else (None,) * 4

        def hcols(c0, width):
            return pl.BlockSpec((t, width), lambda bi, hb: (rb0 + bi, c0 // width + hb))

        def vec(n):
            return pl.BlockSpec((1, n), lambda bi, hb: (0, 0))

        def cache_spec(width, per_head):
            if per_head:
                return pl.BlockSpec((None, None, past, width), lambda bi, hb: (bi, l, 0, hb))
            return pl.BlockSpec((None, None, past, width), lambda bi, hb: (bi, l, 0, 0))

        wide = pl.BlockSpec((t, hp * mla_hw), lambda bi, hb: (rb0 + bi, hb))
        main = [(q_all, wide), (kv_new, wide), (h, pl.BlockSpec((t, LANE), lambda bi, hb: (rb0 + bi, col["kr"] // LANE)))]
        cache = []
        if lat:
            cache = [(kv_cache, pl.BlockSpec((past, hp * mla_hw), lambda bi, hb: (bi, hb))),
                     (cache_mla_krope, cache_spec(rope_d, False))]
        o_a = _attn_launch(
            functools.partial(_mla_fused_kernel, hp=hp, t=t, scale=(MLA_NOPE_DIM + rope_d) ** -0.5, rope_d=rope_d,
                              n_rot=n_rot_small, **flags),
            n_tok, heads, hp, row0, b, t, main, rope_small if lat else None, cache, pa,
            t * hp * mla_hw * 6 + past * hp * mla_hw * 2)

        lam_p = diff_lambda[l].astype(F32)
        lam_init = 0.8 - 0.6 * math.exp(-0.3 * l)
        lam = jnp.exp(jnp.sum(lam_p[0] * lam_p[1])) - jnp.exp(jnp.sum(lam_p[2] * lam_p[3])) + lam_init
        lam_v = jnp.full((1, LANE), lam, F32)
        g_v = (diff_subln_g[l] * (1.0 - lam_init)).reshape(1, LANE)
        w = hp * LANE
        main = [(h, hcols(col["bq"], w)), (h, hcols(col["bk"], w)), (h, hcols(col["bv"], w)),
                (lam_v, vec(LANE)), (g_v, vec(LANE))]
        cache = [(cache_dk, cache_spec(w, True)), (cache_dv, cache_spec(w, True))] if lat else []
        o_b = _attn_launch(
            functools.partial(_diff_fused_kernel, hp=hp, t=t, scale=dd ** -0.5, dd=dd, n_rot=n_rot_small, **flags),
            n_tok, heads, hp, row0, b, t, main, rope_small if lat else None, cache, pb,
            (3 * t + 2 * past) * w * 4)

        kv_group = heads // kv_heads
        if lat:
            kspec = pl.BlockSpec((t, LANE), lambda bi, hb: (rb0 + bi, col["gk"] // LANE + hb // kv_group))
            vspec = pl.BlockSpec((t, LANE), lambda bi, hb: (rb0 + bi, col["gv"] // LANE + hb // kv_group))
            cspec = pl.BlockSpec((None, None, past, LANE), lambda bi, hb: (bi, l, 0, hb // kv_group))
            cache = [(cache_gk, cspec), (cache_gv, cspec)]
        else:
            kspec = pl.BlockSpec((t, kvw), lambda bi, hb: (rb0 + bi, col["gk"] // kvw))
            vspec = pl.BlockSpec((t, kvw), lambda bi, hb: (rb0 + bi, col["gv"] // kvw))
            cache = []
        main = [(h, hcols(col["gq"], w)), (h, kspec), (h, vspec),
                (gqa_g_q[l].reshape(1, gd), vec(gd)), (gqa_g_k[l].reshape(1, gd), vec(gd))]
        o_c = _attn_launch(
            functools.partial(_gqa_fused_kernel, hp=hp, kv_group=kv_group, t=t, scale=gd ** -0.5, n_rot=n_rot_big,
                              **flags),
            n_tok, heads, hp, row0, b, t, main, rope_big if lat else None, cache, pc,
            (t + 2 * t + 2 * past) * w * 4)

        tab = jnp.stack([jnp.log(lb_f[l]), jnp.log1p(-lb_f[l]), 1.0 - lb_f[l],
                         jnp.log(lb_b[l]), jnp.log1p(-lb_b[l]), 1.0 - lb_b[l],
                         jnp.tile(hgrn_g_norm[l], heads), jnp.zeros((heads * hk,), F32)], axis=0)
        o_d, states = hgrn_mix(h, row0, b, t, heads, hk, hv, hgrn_cols, tab, state_hgrn[:, l] if lat else None,
                               None if lat else (l, depth, new_states), pd)
        return (o_a, o_b, o_c, o_d), states

    x = jnp.concatenate([x_prompt.reshape(n_ctx, d), x_sample.reshape(n_lat, d)], axis=0)
    u = modulate(tok, x, mod, 1, 0)
    caches, new_states = None, None
    for l in range(depth):
        h = matmul(u, w_in_b, F32, layer=l)
        q_all = rms_matmul(h, col["cq"], unit, mla_g_cq[l].reshape(1, q_rank), w_uq_b, l, F32)
        kv_new = rms_matmul(h, col["ckv"], kv_rank, mla_g_ckv[l].reshape(1, kv_rank), w_ukv_b, l, BF16)
        kv_cache = matmul(cache_mla_ckv[:, l].reshape(bl * past, kv_rank).astype(BF16), w_ukv_b, BF16, layer=l)
        parts, new_states = mixers(h, q_all, kv_new, kv_cache, l, False, None, new_states)
        parts, _ = mixers(h, q_all, kv_new, kv_cache, l, True, parts)
        caches = write_caches(h, col, l, depth, bc, tc, kv_rank, rope_d, unit, kvw, mla_g_ckv[l].reshape(1, kv_rank),
                              gqa_g_k[l].reshape(1, gd), caches)
        mix = out_proj(parts, w_out, l)
        x, u = residual_ln(tok, x, mix, mod, l * 6 + 2, ln_gb, 2 * l, 2 * depth + 2 * l, alpha,
                           l * 6 + 4, l * 6 + 3)
        j = l // 2
        if l % 2 == 0:
            f = down_proj(swiglu_up(u, ffn_w1, ffn_w3, j), ffn_w2_b, j)
        else:
            logits = matmul(u, router_b, F32, layer=j)[:, :n_exp]
            f = moe_experts(u, logits, moe_w1_r, moe_w3_r, moe_w2_b, j, n_exp)
        if l + 1 < depth:
            x, u = residual_ln(tok, x, f, mod, l * 6 + 5, ln_gb, 2 * l + 1, 2 * depth + 2 * l + 1, alpha,
                               (l + 1) * 6 + 1, (l + 1) * 6 + 0)
        else:
            x, _ = residual_ln(tok, x, f, mod, l * 6 + 5, ln_gb, 2 * l + 1, 2 * depth + 2 * l + 1, alpha)

    y_prompt = x[:n_ctx].reshape(bc, tc, d)
    y_sample = x[n_ctx:].reshape(bl, tl, d)
    new_ckv, new_kr, new_dk, new_dv, new_gk, new_gv = caches
    return (y_prompt, y_sample, new_ckv, new_kr,
            new_dk.reshape(bc, depth, tc, heads, 2 * dd), new_dv.reshape(bc, depth, tc, heads, 2 * dd),
            new_gk.reshape(bc, depth, tc, kv_heads, gd), new_gv.reshape(bc, depth, tc, kv_heads, gd), new_states)
```

```python
import functools
import math

import jax
import jax.numpy as jnp
from jax import lax
from jax.experimental import pallas as pl
from jax.experimental.pallas import tpu as pltpu

F32 = jnp.float32
BF16 = jnp.bfloat16

GRID_W = 64
ROPE_THETA = 10000.0
LN_EPS = 1e-5
RMS_EPS = 1e-6
MLA_NOPE_DIM = 128
MLA_V_DIM = 128
TOP_K = 2

LANE = 128
V7X_VMEM_BYTES = 64 * 1024 * 1024
VMEM_CAP = V7X_VMEM_BYTES - 6 * 1024 * 1024
HGRN_CHUNK = 64
HGRN_SUB = 16
HGRN_BATCHES = 4
MOE_ROWS = 512


def _cparams(sem, est_bytes):
    limit = int(min(VMEM_CAP, max(32 * 1024 * 1024, est_bytes * 5 // 4 + (4 << 20))))
    return pltpu.CompilerParams(dimension_semantics=sem, vmem_limit_bytes=limit)


def _pick_tile(n, target):
    if n <= target:
        return n
    t = (target // LANE) * LANE
    while t >= LANE:
        if n % t == 0:
            return t
        t -= LANE
    return n


def _pick_rows(n, target):
    if n <= target:
        return n
    t = target
    while t >= 8:
        if n % t == 0:
            return t
        t //= 2
    return n


def _dot_nt(a, b):
    return lax.dot_general(a, b, (((1,), (1,)), ((), ())), preferred_element_type=F32)


def _dot_tn(a, b):
    return lax.dot_general(a, b, (((0,), (0,)), ((), ())), preferred_element_type=F32)


def _mm_kernel(x_ref, w_ref, o_ref):
    o_ref[...] = jnp.dot(x_ref[...], w_ref[...], preferred_element_type=F32).astype(o_ref.dtype)


def matmul(x, w, out_dtype, layer=None, tm=1024, tn=1024):
    m, k = x.shape
    n = w.shape[-1]
    tm = _pick_rows(m, tm)
    tn = _pick_tile(n, tn)
    if w.ndim == 3:
        w_spec = pl.BlockSpec((None, k, tn), lambda j, i: (layer, 0, j))
    else:
        w_spec = pl.BlockSpec((k, tn), lambda j, i: (0, j))
    est = 2 * (tm * k * 2 + k * tn * 2 + tm * tn * jnp.dtype(out_dtype).itemsize) + tm * tn * 4
    return pl.pallas_call(
        _mm_kernel,
        grid=(n // tn, m // tm),
        in_specs=[pl.BlockSpec((tm, k), lambda j, i: (i, 0)), w_spec],
        out_specs=pl.BlockSpec((tm, tn), lambda j, i: (i, j)),
        out_shape=jax.ShapeDtypeStruct((m, n), out_dtype),
        compiler_params=_cparams(("arbitrary", "arbitrary"), est),
    )(x, w)


def _regroup_kernel(w_ref, o_ref, *, pieces):
    x = w_ref[...]
    parts = [jnp.zeros((x.shape[0], w), F32) if s0 is None else x[:, s0:s0 + w] for s0, w in pieces]
    o_ref[...] = jnp.concatenate(parts, axis=1).astype(o_ref.dtype)


def regroup_columns(w, pieces, rows=256):
    depth, k, n = w.shape
    n_out = sum(wd for _, wd in pieces)
    rows = _pick_rows(k, rows)
    est = 2 * rows * (n * 4 + n_out * 2) + 3 * rows * n_out * 4
    return pl.pallas_call(
        functools.partial(_regroup_kernel, pieces=pieces),
        grid=(depth, k // rows),
        in_specs=[pl.BlockSpec((None, rows, n), lambda l, i: (l, i, 0))],
        out_specs=pl.BlockSpec((None, rows, n_out), lambda l, i: (l, i, 0)),
        out_shape=jax.ShapeDtypeStruct((depth, k, n_out), BF16),
        compiler_params=_cparams(("arbitrary", "arbitrary"), est),
    )(w)


def _cast_kernel(w_ref, o_ref):
    o_ref[...] = w_ref[...].astype(o_ref.dtype)


def cast_bf16(w, rows=1024):
    r, n = w.shape
    rows = _pick_rows(r, rows)
    return pl.pallas_call(
        _cast_kernel,
        grid=(r // rows,),
        in_specs=[pl.BlockSpec((rows, n), lambda i: (i, 0))],
        out_specs=pl.BlockSpec((rows, n), lambda i: (i, 0)),
        out_shape=jax.ShapeDtypeStruct((r, n), BF16),
        compiler_params=_cparams(("arbitrary",), 2 * rows * n * 6 + rows * n * 4),
    )(w)


def _mm_castw_kernel(x_ref, w_ref, o_ref):
    o_ref[...] = jnp.dot(x_ref[...], w_ref[...].astype(BF16), preferred_element_type=F32).astype(o_ref.dtype)


def matmul_few_rows(x, w, layer, tn=1024):
    m, k = x.shape
    n = w.shape[-1]
    tn = _pick_tile(n, tn)
    est = 2 * (m * k * 2 + k * tn * 4 + m * tn * 4) + k * tn * 2
    return pl.pallas_call(
        _mm_castw_kernel,
        grid=(n // tn,),
        in_specs=[pl.BlockSpec((m, k), lambda j: (0, 0)), pl.BlockSpec((None, k, tn), lambda j: (layer, 0, j))],
        out_specs=pl.BlockSpec((m, tn), lambda j: (0, j)),
        out_shape=jax.ShapeDtypeStruct((m, n), F32),
        compiler_params=_cparams(("arbitrary",), est),
    )(x, w)


def _rms_mm_kernel(x_ref, g_ref, w_ref, o_ref):
    x = x_ref[:, :w_ref.shape[0]]
    xn = x * lax.rsqrt(jnp.mean(x * x, axis=-1, keepdims=True) + RMS_EPS) * g_ref[...]
    o_ref[...] = jnp.dot(xn.astype(BF16), w_ref[...], preferred_element_type=F32).astype(o_ref.dtype)


def rms_matmul(h, col0, kb, gain, w, layer, out_dtype, tm=1024, tn=1024):
    m = h.shape[0]
    k, n = w.shape[1], w.shape[2]
    assert col0 % kb == 0 and kb >= k
    tm = _pick_rows(m, tm)
    tn = _pick_tile(n, tn)
    est = 2 * (tm * kb * 4 + k * tn * 2 + tm * tn * 4) + 3 * tm * k * 4 + tm * tn * 4
    return pl.pallas_call(
        _rms_mm_kernel,
        grid=(n // tn, m // tm),
        in_specs=[pl.BlockSpec((tm, kb), lambda j, i: (i, col0 // kb)),
                  pl.BlockSpec((1, k), lambda j, i: (0, 0)),
                  pl.BlockSpec((None, k, tn), lambda j, i: (layer, 0, j))],
        out_specs=pl.BlockSpec((tm, tn), lambda j, i: (i, j)),
        out_shape=jax.ShapeDtypeStruct((m, n), out_dtype),
        compiler_params=_cparams(("arbitrary", "arbitrary"), est),
    )(h, gain, w)


def _out_proj_kernel(a_ref, b_ref, c_ref, d_ref, w_ref, o_ref, wb_scr):
    @pl.when(pl.program_id(1) == 0)
    def _():
        wb_scr[...] = w_ref[...].astype(BF16)

    kq = a_ref.shape[1]
    acc = jnp.dot(a_ref[...], wb_scr[0:kq, :], preferred_element_type=F32)
    for g, r in enumerate((b_ref, c_ref, d_ref), start=1):
        acc = acc + jnp.dot(r[...], wb_scr[g * kq:(g + 1) * kq, :], preferred_element_type=F32)
    o_ref[...] = acc


def out_proj(parts, w, layer, tm=1024, tn=512):
    m, kq = parts[0].shape
    k, n = w.shape[1], w.shape[2]
    assert len(parts) == 4 and 4 * kq == k
    tm = _pick_rows(m, tm)
    tn = _pick_tile(n, tn)
    est = 2 * (4 * tm * kq * 2 + k * tn * 4 + tm * tn * 4) + k * tn * 2 + 2 * tm * tn * 4
    part_spec = pl.BlockSpec((tm, kq), lambda j, i: (i, 0))
    return pl.pallas_call(
        _out_proj_kernel,
        grid=(n // tn, m // tm),
        in_specs=[part_spec] * 4 + [pl.BlockSpec((None, k, tn), lambda j, i: (layer, 0, j))],
        out_specs=pl.BlockSpec((tm, tn), lambda j, i: (i, j)),
        out_shape=jax.ShapeDtypeStruct((m, n), F32),
        scratch_shapes=[pltpu.VMEM((k, tn), BF16)],
        compiler_params=_cparams(("arbitrary", "arbitrary"), est),
    )(*parts, w)


def _swiglu_kernel(x_ref, w1_ref, w3_ref, o_ref, w1b_scr, w3b_scr):
    @pl.when(pl.program_id(1) == 0)
    def _():
        w1b_scr[...] = w1_ref[...].astype(BF16)
        w3b_scr[...] = w3_ref[...].astype(BF16)

    x = x_ref[...]
    a = jnp.dot(x, w1b_scr[...], preferred_element_type=F32)
    b = jnp.dot(x, w3b_scr[...], preferred_element_type=F32)
    o_ref[...] = (a * jax.nn.sigmoid(a) * b).astype(o_ref.dtype)


def swiglu_up(x, w1, w3, layer, tm=512, tn=512):
    m, k = x.shape
    n = w1.shape[-1]
    tm = _pick_rows(m, tm)
    tn = _pick_tile(n, tn)
    w_spec = pl.BlockSpec((None, k, tn), lambda j, i: (layer, 0, j))
    est = 2 * (tm * k * 2 + 2 * k * tn * 4 + tm * tn * 2) + 2 * k * tn * 2 + 3 * tm * tn * 4
    return pl.pallas_call(
        _swiglu_kernel,
        grid=(n // tn, m // tm),
        in_specs=[pl.BlockSpec((tm, k), lambda j, i: (i, 0)), w_spec, w_spec],
        out_specs=pl.BlockSpec((tm, tn), lambda j, i: (i, j)),
        out_shape=jax.ShapeDtypeStruct((m, n), BF16),
        scratch_shapes=[pltpu.VMEM((k, tn), BF16), pltpu.VMEM((k, tn), BF16)],
        compiler_params=_cparams(("arbitrary", "arbitrary"), est),
    )(x, w1, w3)


def _down_kernel(x_ref, w_ref, o_ref):
    p = jnp.dot(x_ref[...], w_ref[...], preferred_element_type=F32)

    @pl.when(pl.program_id(2) == 0)
    def _():
        o_ref[...] = p

    @pl.when(pl.program_id(2) > 0)
    def _():
        o_ref[...] += p


def down_proj(x, w, layer, tm=2048, tn=1024, tk=1792):
    m, k = x.shape
    n = w.shape[-1]
    tm = _pick_rows(m, tm)
    tn = _pick_tile(n, tn)
    tk = _pick_tile(k, tk)
    est = 2 * (tm * tk * 2 + tk * tn * 2 + tm * tn * 4) + tm * tn * 4
    return pl.pallas_call(
        _down_kernel,
        grid=(m // tm, n // tn, k // tk),
        in_specs=[pl.BlockSpec((tm, tk), lambda i, j, s: (i, s)),
                  pl.BlockSpec((None, tk, tn), lambda i, j, s: (layer, s, j))],
        out_specs=pl.BlockSpec((tm, tn), lambda i, j, s: (i, j)),
        out_shape=jax.ShapeDtypeStruct((m, n), F32),
        compiler_params=_cparams(("arbitrary", "arbitrary", "arbitrary"), est),
    )(x, w)


def _moe_up_kernel(rt_ref, nt_ref, ex_ref, ok_ref, new_ref, x_ref, w1_ref, w3_ref, o_ref, w1b_scr, w3b_scr):
    @pl.when(new_ref[pl.program_id(0)] == 1)
    def _():
        w1b_scr[...] = w1_ref[...].astype(BF16)
        w3b_scr[...] = w3_ref[...].astype(BF16)

    @pl.when(ok_ref[pl.program_id(0)] == 1)
    def _():
        x = x_ref[...]
        a = jnp.dot(x, w1b_scr[...], preferred_element_type=F32)
        b = jnp.dot(x, w3b_scr[...], preferred_element_type=F32)
        o_ref[...] = (a * jax.nn.sigmoid(a) * b).astype(o_ref.dtype)


def _moe_down_kernel(rt_ref, nt_ref, ex_ref, ok_ref, x_ref, w_ref, g_ref, o_ref):
    @pl.when(ok_ref[pl.program_id(0)] == 1)
    def _():
        o_ref[...] = jnp.dot(x_ref[...], w_ref[...], preferred_element_type=F32) * g_ref[...]


def _moe_items(tiles_e, tile0_e, nj, n_tiles):
    n_items = n_tiles * nj
    per_e = tiles_e * nj
    end_e = jnp.cumsum(per_e)
    total = end_e[-1]
    pos = jnp.arange(n_items, dtype=jnp.int32)
    idx = jnp.minimum(pos, total - 1)
    ex = jnp.searchsorted(end_e, idx, side="right").astype(jnp.int32)
    local = idx - (end_e - per_e)[ex]
    te = jnp.maximum(tiles_e[ex], 1)
    rt = tile0_e[ex] + local % te
    nt = local // te
    ok = pos < total
    new = ok & (local % te == 0)
    return rt.astype(jnp.int32), nt.astype(jnp.int32), ex, ok.astype(jnp.int32), new.astype(jnp.int32)


def moe_experts(u, logits, w1, w3, w2, group, n_exp):
    n, d = u.shape
    f = w1.shape[-1]
    tm = min(MOE_ROWS, n)
    top_v, top_i = lax.top_k(logits, TOP_K)
    wts = jax.nn.softmax(top_v, axis=-1)
    n_pairs = n * TOP_K
    n_rows = -(-n_pairs // tm) * tm + n_exp * tm
    n_tiles = n_rows // tm

    e_flat = top_i.reshape(-1).astype(jnp.int32)
    order = jnp.argsort(e_flat, stable=True)
    cnt = jnp.sum(jax.nn.one_hot(e_flat, n_exp, dtype=jnp.int32), axis=0)
    tiles_e = (cnt + tm - 1) // tm
    tile0_e = jnp.cumsum(tiles_e) - tiles_e
    start_e = jnp.cumsum(cnt) - cnt
    e_sorted = e_flat[order]
    dest_sorted = tile0_e[e_sorted] * tm + jnp.arange(n_pairs, dtype=jnp.int32) - start_e[e_sorted]
    dest = dest_sorted[jnp.argsort(order)]
    rows = jnp.arange(n_rows, dtype=jnp.int32)
    e_row = jnp.minimum(jnp.searchsorted((tile0_e + tiles_e) * tm, rows, side="right"), n_exp - 1).astype(jnp.int32)
    r_in = rows - tile0_e[e_row] * tm
    live = (r_in >= 0) & (r_in < cnt[e_row])
    pair = order[jnp.clip(start_e[e_row] + r_in, 0, n_pairs - 1)]
    row_token = jnp.where(live, pair // TOP_K, 0).astype(jnp.int32)
    row_gate = jnp.where(live, wts.reshape(-1)[pair], 0.0)

    xg = jnp.take(u, row_token, axis=0, mode="clip")

    tn = _pick_tile(f, 512)
    items = _moe_items(tiles_e, tile0_e, f // tn, n_tiles)
    w_spec = pl.BlockSpec((None, d, tn), lambda i, rt, nt, ex, ok, new: (group * n_exp + ex[i], 0, nt[i]))
    est = 2 * (tm * d * 2 + 2 * d * tn * 4 + tm * tn * 2) + 2 * d * tn * 2 + 3 * tm * tn * 4
    act = pl.pallas_call(
        _moe_up_kernel,
        grid_spec=pltpu.PrefetchScalarGridSpec(
            num_scalar_prefetch=5, grid=(n_tiles * (f // tn),),
            in_specs=[pl.BlockSpec((tm, d), lambda i, rt, nt, ex, ok, new: (rt[i], 0)), w_spec, w_spec],
            out_specs=pl.BlockSpec((tm, tn), lambda i, rt, nt, ex, ok, new: (rt[i], nt[i])),
            scratch_shapes=[pltpu.VMEM((d, tn), BF16), pltpu.VMEM((d, tn), BF16)]),
        out_shape=jax.ShapeDtypeStruct((n_rows, f), BF16),
        compiler_params=_cparams(("arbitrary",), est),
    )(*items, xg, w1, w3)

    tn2 = _pick_tile(d, 512)
    items2 = _moe_items(tiles_e, tile0_e, d // tn2, n_tiles)[:4]
    est = 2 * (tm * f * 2 + f * tn2 * 2 + tm * tn2 * 4 + tm * LANE * 4) + tm * tn2 * 4
    y = pl.pallas_call(
        _moe_down_kernel,
        grid_spec=pltpu.PrefetchScalarGridSpec(
            num_scalar_prefetch=4, grid=(n_tiles * (d // tn2),),
            in_specs=[pl.BlockSpec((tm, f), lambda i, rt, nt, ex, ok: (rt[i], 0)),
                      pl.BlockSpec((None, f, tn2), lambda i, rt, nt, ex, ok: (group * n_exp + ex[i], 0, nt[i])),
                      pl.BlockSpec((tm, 1), lambda i, rt, nt, ex, ok: (rt[i], 0))],
            out_specs=pl.BlockSpec((tm, tn2), lambda i, rt, nt, ex, ok: (rt[i], nt[i]))),
        out_shape=jax.ShapeDtypeStruct((n_rows, d), F32),
        compiler_params=_cparams(("arbitrary",), est),
    )(*items2, act, w2, row_gate[:, None])

    dest = dest.reshape(n, TOP_K)
    return jnp.take(y, dest[:, 0], axis=0, mode="clip"), jnp.take(y, dest[:, 1], axis=0, mode="clip")


def _modulate_kernel(x_ref, scale_ref, shift_ref, u_ref):
    u_ref[...] = (x_ref[...] * (1.0 + scale_ref[0, 0]) + shift_ref[0, 0]).astype(u_ref.dtype)


def _ln_kernel(*refs, alpha, n_y, with_next):
    x_ref = refs[0]
    y = refs[1][...]
    if n_y == 2:
        y = y + refs[2][...]
    gate_ref, g_ref, b_ref = refs[1 + n_y:4 + n_y]
    z = alpha * x_ref[...] + gate_ref[0, 0] * y
    zc = z - jnp.mean(z, axis=-1, keepdims=True)
    yn = zc * lax.rsqrt(jnp.mean(zc * zc, axis=-1, keepdims=True) + LN_EPS)
    xn = yn * g_ref[...] + b_ref[...]
    if with_next:
        scale_ref, shift_ref, xo_ref, uo_ref = refs[4 + n_y:]
        xo_ref[...] = xn
        uo_ref[...] = (xn * (1.0 + scale_ref[0, 0]) + shift_ref[0, 0]).astype(uo_ref.dtype)
    else:
        refs[4 + n_y][...] = xn


class _Tokens:
    def __init__(self, n_ctx, t_lat, n_lat_batches, d):
        self.n_ctx, self.t_lat, self.d = n_ctx, t_lat, d
        self.n = n_ctx + t_lat * n_lat_batches
        tm = 256
        while n_ctx % tm or t_lat % tm:
            tm //= 2
        self.tm = tm
        self.ctx_tiles = n_ctx // tm
        self.tiles_per_lat = t_lat // tm

    def mod_row(self, i):
        return jnp.where(i < self.ctx_tiles, 0, 1 + (i - self.ctx_tiles) // self.tiles_per_lat)

    def mod_spec(self, idx):
        return pl.BlockSpec((1, 1, 1, self.d), lambda i: (idx, self.mod_row(i), 0, 0))

    def row_spec(self):
        return pl.BlockSpec((self.tm, self.d), lambda i: (i, 0))

    def vec_spec(self, idx):
        return pl.BlockSpec((None, 1, self.d), lambda i: (idx, 0, 0))


def modulate(tok, x, mod, idx_scale, idx_shift):
    est = 2 * tok.tm * tok.d * 6 + 4 * tok.tm * tok.d * 4
    return pl.pallas_call(
        _modulate_kernel,
        grid=(tok.n // tok.tm,),
        in_specs=[tok.row_spec(), tok.mod_spec(idx_scale), tok.mod_spec(idx_shift)],
        out_specs=tok.row_spec(),
        out_shape=jax.ShapeDtypeStruct((tok.n, tok.d), BF16),
        compiler_params=_cparams(("arbitrary",), est),
    )(x, mod, mod)


def residual_ln(tok, x, y, mod, idx_gate, ln_gb, idx_g, idx_b, alpha, idx_scale=None, idx_shift=None):
    ys = list(y) if isinstance(y, (tuple, list)) else [y]
    est = 2 * tok.tm * tok.d * (4 + 4 * len(ys) + 4 + 2) + 6 * tok.tm * tok.d * 4
    in_specs = ([tok.row_spec()] * (1 + len(ys))
                + [tok.mod_spec(idx_gate), tok.vec_spec(idx_g), tok.vec_spec(idx_b)])
    args = [x] + ys + [mod, ln_gb, ln_gb]
    if idx_scale is None:
        return pl.pallas_call(
            functools.partial(_ln_kernel, alpha=alpha, n_y=len(ys), with_next=False),
            grid=(tok.n // tok.tm,),
            in_specs=in_specs,
            out_specs=tok.row_spec(),
            out_shape=jax.ShapeDtypeStruct((tok.n, tok.d), F32),
            compiler_params=_cparams(("arbitrary",), est),
        )(*args), None
    in_specs += [tok.mod_spec(idx_scale), tok.mod_spec(idx_shift)]
    args += [mod, mod]
    return pl.pallas_call(
        functools.partial(_ln_kernel, alpha=alpha, n_y=len(ys), with_next=True),
        grid=(tok.n // tok.tm,),
        in_specs=in_specs,
        out_specs=[tok.row_spec(), tok.row_spec()],
        out_shape=[jax.ShapeDtypeStruct((tok.n, tok.d), F32), jax.ShapeDtypeStruct((tok.n, tok.d), BF16)],
        compiler_params=_cparams(("arbitrary",), est),
    )(*args)


ATTN_Q_ROWS = 512


def _softmax_pv(scores, vals):
    m = jnp.max(scores[0], axis=-1, keepdims=True)
    for s in scores[1:]:
        m = jnp.maximum(m, jnp.max(s, axis=-1, keepdims=True))
    acc = None
    den = None
    for s, v in zip(scores, vals):
        p = jnp.exp(s - m)
        ps = jnp.sum(p, axis=-1, keepdims=True)
        pv = jnp.dot(p.astype(BF16), v, preferred_element_type=F32)
        acc = pv if acc is None else acc + pv
        den = ps if den is None else den + ps
    return acc / den


def _rope_lanes(x, cos, sin_signed, n):
    lane = lax.broadcasted_iota(jnp.int32, x.shape, 1)
    first = ((lane // n) % 2) == 0
    partner = jnp.where(first, pltpu.roll(x, LANE - n, 1), pltpu.roll(x, n, 1))
    return x * cos + partner * sin_signed


def _rms_rows(x, g):
    return x * lax.rsqrt(jnp.mean(x * x, axis=-1, keepdims=True) + RMS_EPS) * g


def _attn_refs(refs, n_main, use_rope, has_cache, n_cache, aliased):
    it = iter(refs)
    main = [next(it) for _ in range(n_main)]
    rope = (next(it), next(it)) if use_rope else (None, None)
    cache = [next(it) for _ in range(n_cache)] if has_cache else [None] * n_cache
    if aliased:
        next(it)
    return main, rope, cache, next(it)


def _gqa_fused_kernel(*refs, hp, kv_group, t, scale, n_rot, use_rope, has_cache, aliased):
    (q_ref, k_ref, v_ref, gq_ref, gk_ref), (cos_ref, sin_ref), (ck_ref, cv_ref), o_ref = _attn_refs(
        refs, 5, use_rope, has_cache, 2, aliased)
    tq = min(ATTN_Q_ROWS, t)
    n_kv = max(1, hp // kv_group)
    keys, vals = [], []
    for j in range(n_kv):
        sl = slice(j * LANE, (j + 1) * LANE)
        k = _rms_rows(k_ref[:, sl], gk_ref[...])
        if use_rope:
            k = _rope_lanes(k, cos_ref[...], sin_ref[...], n_rot)
        kj, vj = [k.astype(BF16)], [v_ref[:, sl].astype(BF16)]
        if has_cache:
            kj.insert(0, ck_ref[:, sl].astype(BF16))
            vj.insert(0, cv_ref[:, sl].astype(BF16))
        keys.append(kj)
        vals.append(vj)
    for hh in range(hp):
        j = hh // kv_group if n_kv > 1 else 0
        for r0 in range(0, t, tq):
            q = _rms_rows(q_ref[r0:r0 + tq, hh * LANE:(hh + 1) * LANE], gq_ref[...])
            if use_rope:
                q = _rope_lanes(q, cos_ref[r0:r0 + tq, :], sin_ref[r0:r0 + tq, :], n_rot)
            qb = q.astype(BF16)
            o = _softmax_pv([_dot_nt(qb, k) * scale for k in keys[j]], vals[j])
            o_ref[r0:r0 + tq, hh * LANE:(hh + 1) * LANE] = o.astype(o_ref.dtype)


def _diff_fused_kernel(*refs, hp, t, scale, dd, n_rot, use_rope, has_cache, aliased):
    (q_ref, k_ref, v_ref, lam_ref, g_ref), (cos_ref, sin_ref), (ck_ref, cv_ref), o_ref = _attn_refs(
        refs, 5, use_rope, has_cache, 2, aliased)
    tq = min(ATTN_Q_ROWS, t)
    for hh in range(hp):
        sl = slice(hh * LANE, (hh + 1) * LANE)
        k = k_ref[:, sl]
        if use_rope:
            k = _rope_lanes(k, cos_ref[...], sin_ref[...], n_rot)
        keys, vals = [k.astype(BF16)], [v_ref[:, sl].astype(BF16)]
        if has_cache:
            keys.insert(0, ck_ref[:, sl].astype(BF16))
            vals.insert(0, cv_ref[:, sl].astype(BF16))
        for r0 in range(0, t, tq):
            q = q_ref[r0:r0 + tq, sl]
            if use_rope:
                q = _rope_lanes(q, cos_ref[r0:r0 + tq, :], sin_ref[r0:r0 + tq, :], n_rot)
            lane = lax.broadcasted_iota(jnp.int32, q.shape, 1)
            q1 = jnp.where(lane < dd, q, 0.0).astype(BF16)
            q2 = jnp.where(lane < dd, 0.0, q).astype(BF16)
            a1 = _softmax_pv([_dot_nt(q1, kk) * scale for kk in keys], vals)
            a2 = _softmax_pv([_dot_nt(q2, kk) * scale for kk in keys], vals)
            d = a1 - lam_ref[...] * a2
            o_ref[r0:r0 + tq, sl] = _rms_rows(d, g_ref[...]).astype(o_ref.dtype)


def _mla_fused_kernel(*refs, hp, t, scale, rope_d, n_rot, use_rope, has_cache, aliased):
    (q_ref, kv_ref, kr_ref), (cos_ref, sin_ref), (kvc_ref, krc_ref), o_ref = _attn_refs(
        refs, 3, use_rope, has_cache, 2, aliased)
    tq = min(ATTN_Q_ROWS, t)
    hw = 2 * LANE
    kr = kr_ref[...]
    if use_rope:
        kr = _rope_lanes(kr, cos_ref[...], sin_ref[...], n_rot)
    krs = [kr[:, :rope_d].astype(BF16)]
    if has_cache:
        krs.insert(0, krc_ref[...].astype(BF16))
    for hh in range(hp):
        c0 = hh * hw
        kns, vals = [kv_ref[:, c0:c0 + LANE]], [kv_ref[:, c0 + LANE:c0 + hw]]
        if has_cache:
            kns.insert(0, kvc_ref[:, c0:c0 + LANE])
            vals.insert(0, kvc_ref[:, c0 + LANE:c0 + hw])
        for r0 in range(0, t, tq):
            qn = q_ref[r0:r0 + tq, c0:c0 + LANE].astype(BF16)
            qr = q_ref[r0:r0 + tq, c0 + LANE:c0 + hw]
            if use_rope:
                qr = _rope_lanes(qr, cos_ref[r0:r0 + tq, :], sin_ref[r0:r0 + tq, :], n_rot)
            qr = qr[:, :rope_d].astype(BF16)
            scores = [(_dot_nt(qn, kn) + _dot_nt(qr, kk)) * scale for kn, kk in zip(kns, krs)]
            o_ref[r0:r0 + tq, hh * LANE:(hh + 1) * LANE] = _softmax_pv(scores, vals).astype(o_ref.dtype)


def _attn_launch(body, n_rows, heads, hp, row0, b, t, main, rope, cache, prev, width_est):
    rb0 = row0 // t
    in_specs = [s for _, s in main]
    args = [a for a, _ in main]
    if rope is not None:
        in_specs += [pl.BlockSpec((t, LANE), lambda bi, hb: (0, 0))] * 2
        args += list(rope)
    in_specs += [s for _, s in cache]
    args += [a for a, _ in cache]
    aliases = {}
    if prev is not None:
        aliases = {len(args): 0}
        in_specs.append(pl.BlockSpec(memory_space=pl.ANY))
        args.append(prev)
    n_cached = cache[1][0].shape[2] if cache else 0
    est = 2 * width_est + 8 * min(ATTN_Q_ROWS, t) * (t + n_cached) * 4 + (4 << 20)
    return pl.pallas_call(
        body,
        grid=(b, heads // hp),
        in_specs=in_specs,
        out_specs=pl.BlockSpec((t, hp * LANE), lambda bi, hb: (rb0 + bi, hb)),
        out_shape=jax.ShapeDtypeStruct((n_rows, heads * LANE), BF16),
        input_output_aliases=aliases,
        compiler_params=_cparams(("arbitrary", "arbitrary"), est),
    )(*args)


def _cache_kernel(*refs, n_kv, rope_d, aliased):
    ckv_ref, g_ckv_ref, kr_ref, bk_ref, bv_ref, gk_ref, g_gk_ref, gv_ref = refs[:8]
    o_ckv, o_kr, o_dk, o_dv, o_gk, o_gv = refs[-6:]
    o_ckv[...] = _rms_rows(ckv_ref[...], g_ckv_ref[...])
    o_kr[...] = kr_ref[:, :rope_d]
    o_dk[...] = bk_ref[...]
    o_dv[...] = bv_ref[...]
    for j in range(n_kv):
        sl = slice(j * LANE, (j + 1) * LANE)
        o_gk[:, sl] = _rms_rows(gk_ref[:, sl], g_gk_ref[...])
    o_gv[...] = gv_ref[...]


def write_caches(h, col, l, depth, b, t, kv_rank, rope_d, unit, kvw, g_ckv, g_gk, prev):
    def hcols(c0, width):
        return pl.BlockSpec((t, width), lambda bi: (bi, c0 // width))

    def vec(n):
        return pl.BlockSpec((1, n), lambda bi: (0, 0))

    widths = (kv_rank, rope_d, unit, unit, kvw, kvw)
    in_specs = [hcols(col["ckv"], kv_rank), vec(kv_rank), hcols(col["kr"], LANE), hcols(col["bk"], unit),
                hcols(col["bv"], unit), hcols(col["gk"], kvw), vec(LANE), hcols(col["gv"], kvw)]
    args = [h, g_ckv, h, h, h, h, g_gk, h]
    aliases = {}
    if prev is not None:
        aliases = {len(args) + i: i for i in range(6)}
        in_specs += [pl.BlockSpec(memory_space=pl.ANY)] * 6
        args += list(prev)
    est = 4 * t * (kv_rank + LANE + 2 * unit + 2 * kvw) * 4 * 2 + (4 << 20)
    return pl.pallas_call(
        functools.partial(_cache_kernel, n_kv=kvw // LANE, rope_d=rope_d, aliased=prev is not None),
        grid=(b,),
        in_specs=in_specs,
        out_specs=[pl.BlockSpec((None, None, t, w), lambda bi: (bi, l, 0, 0)) for w in widths],
        out_shape=[jax.ShapeDtypeStruct((b, depth, t, w), F32) for w in widths],
        input_output_aliases=aliases,
        compiler_params=_cparams(("arbitrary",), est),
    )(*args)


def _hgrn_kernel(*refs, t, nb, chunk, sub, q_scale, has_s0, want_state, aliased):
    it = iter(refs)
    dq_ref, di_ref, df_ref, dfb_ref, dg_ref, tab_ref = (next(it) for _ in range(6))
    s0_ref = next(it) if has_s0 else None
    for _ in range(aliased):
        next(it)
    o_ref = next(it)
    st_out_ref = next(it) if want_state else None
    st_scr, o_scr, cum_scr, k_scr, q_scr = (next(it) for _ in range(5))

    n_chunks = t // chunk
    n_sub = chunk // sub
    half = sub // 2
    half_row = lax.broadcasted_iota(jnp.int32, (half, 1), 0)
    row = lax.broadcasted_iota(jnp.int32, (chunk, chunk), 0)
    col = lax.broadcasted_iota(jnp.int32, (chunk, chunk), 1)
    key_row = lax.broadcasted_iota(jnp.int32, (chunk, 1), 0)
    neg_inf = jnp.float32(-jnp.inf)
    tab = tab_ref[...]
    tri_f = jnp.where(row >= col, 1.0, 0.0).astype(F32)
    tri_b = jnp.where(row <= col, 1.0, 0.0).astype(F32)
    same_sub = (row // sub) == (col // sub)

    for bb in range(nb):
        for direction in range(2):
            if has_s0:
                st_scr[2 * bb + direction] = s0_ref[bb, direction, 0].T
            else:
                st_scr[2 * bb + direction] = jnp.zeros(st_scr.shape[1:], F32)

    def prepare(ci, carry):
        r0 = pl.multiple_of(ci * chunk, chunk)
        zq = dq_ref[pl.ds(r0, chunk), :]
        q_scr[pl.ds(r0, chunk), :] = zq * jax.nn.sigmoid(zq) * q_scale
        for direction, z_ref in enumerate((df_ref, dfb_ref)):
            log_lb = tab[3 * direction:3 * direction + 1, :]
            log_1m = tab[3 * direction + 1:3 * direction + 2, :]
            one_m = tab[3 * direction + 2:3 * direction + 3, :]
            z = z_ref[pl.ds(r0, chunk), :]
            e = jnp.exp(-jnp.abs(z))
            d1 = 1.0 + e
            t2 = log_1m + (jnp.minimum(z, 0.0) - jnp.log(d1))
            g = jnp.maximum(log_lb, t2) + jnp.log(1.0 + jnp.exp(-jnp.abs(log_lb - t2)))
            k_scr[direction, pl.ds(r0, chunk), :] = one_m * (jnp.where(z >= 0.0, e, 1.0) / d1)
            cum_scr[direction, pl.ds(r0, chunk), :] = jnp.dot(
                tri_b if direction else tri_f, g, precision=lax.Precision.HIGHEST, preferred_element_type=F32)
        return carry

    lax.fori_loop(0, nb * n_chunks, prepare, 0, unroll=2)

    def chain_step(ci, bb, direction):
        rev = direction == 1
        r0 = pl.multiple_of(bb * t + ((n_chunks - 1 - ci) if rev else ci) * chunk, chunk)
        q = q_scr[pl.ds(r0, chunk), :]
        k = k_scr[direction, pl.ds(r0, chunk), :]
        cum = cum_scr[direction, pl.ds(r0, chunk), :]
        v = di_ref[pl.ds(r0, chunk), :]
        edge = cum[0:1, :] if rev else cum[chunk - 1:chunk, :]
        slot = 2 * bb + direction
        st = st_scr[slot]
        vb = v.astype(BF16)
        inter = _dot_nt((q * jnp.exp(cum)).astype(BF16), st.astype(BF16))

        mid = [cum[i * sub + half:i * sub + half + 1, :] if rev else cum[i * sub + half - 1:i * sub + half, :]
               for i in range(n_sub)]
        base2 = jnp.concatenate([jnp.broadcast_to(m, (sub, LANE)) for m in mid], axis=0)
        late = (key_row % sub) >= half
        q_side, k_side = (~late, late) if rev else (late, ~late)
        q2 = (q * jnp.exp(jnp.where(q_side, cum - base2, neg_inf))).astype(BF16)
        k2 = (k * jnp.exp(jnp.where(k_side, base2 - cum, neg_inf))).astype(BF16)
        att2 = jnp.where(same_sub, _dot_nt(q2, k2), 0.0)
        near = inter + jnp.dot(att2.astype(BF16), vb, preferred_element_type=F32)

        outs = []
        for i in range(n_sub):
            lo = i * sub
            acc = near[lo:lo + sub]
            if (i < n_sub - 1) if rev else (i > 0):
                base = cum[lo + sub:lo + sub + 1, :] if rev else cum[lo - 1:lo, :]
                seen = (key_row >= lo + sub) if rev else (key_row < lo)
                qn = (q[lo:lo + sub] * jnp.exp(cum[lo:lo + sub] - base)).astype(BF16)
                kn = (k * jnp.exp(jnp.where(seen, base - cum, neg_inf))).astype(BF16)
                att = _dot_nt(qn, kn)
                acc = acc + jnp.dot(att.astype(BF16), vb, preferred_element_type=F32)
            for hb in range(sub // half):
                lo8 = lo + hb * half
                q8 = q[lo8:lo8 + half]
                cum8 = cum[lo8:lo8 + half]
                acc8 = acc[hb * half:(hb + 1) * half]
                for s in range(half):
                    r = lo8 + s
                    live = (half_row <= s) if rev else (half_row >= s)
                    w = jnp.exp(jnp.where(live, cum8 - cum_scr[direction, pl.ds(r0 + r, 1), :], neg_inf))
                    colv = jnp.sum(q8 * k_scr[direction, pl.ds(r0 + r, 1), :] * w, axis=-1, keepdims=True)
                    acc8 = acc8 + colv * di_ref[pl.ds(r0 + r, 1), :]
                outs.append(acc8)
        o_scr[direction, pl.ds(r0, chunk), :] = jnp.concatenate(outs, axis=0)
        kl = (k * jnp.exp(edge - cum)).astype(BF16)
        st_scr[2 * bb + direction] = st * jnp.exp(edge) + _dot_tn(vb, kl)

    def body(ci, carry):
        for bb in range(nb):
            for direction in range(2):
                chain_step(ci, bb, direction)
        return carry

    lax.fori_loop(0, n_chunks, body, 0)

    tot = o_scr[0] + o_scr[1]
    zg = dg_ref[...]
    y = tot * lax.rsqrt(jnp.mean(tot * tot, axis=-1, keepdims=True) + RMS_EPS) * tab[6:7, :]
    o_ref[...] = (y * (zg * jax.nn.sigmoid(zg))).astype(o_ref.dtype)
    if want_state:
        for bb in range(nb):
            for direction in range(2):
                st_out_ref[bb, direction, 0] = st_scr[2 * bb + direction].T


def hgrn_mix(h, row0, b, t, heads, dk, dv, cols, tab, s0, state_into, prev):
    assert dk == LANE and dv == LANE
    nb = HGRN_BATCHES if (b % HGRN_BATCHES == 0 and row0 % (HGRN_BATCHES * t) == 0) else 1
    assert row0 % (nb * t) == 0 and all(cc % LANE == 0 for cc in cols)
    chunk = min(HGRN_CHUNK, t)
    sub = min(HGRN_SUB, chunk)
    rb0 = row0 // (nb * t)

    def col_spec(c0):
        return pl.BlockSpec((nb * t, LANE), lambda bi, hh: (rb0 + bi, c0 // LANE + hh))

    st_spec = pl.BlockSpec((nb, 2, 1, dk, dv), lambda bi, hh: (bi, 0, hh, 0, 0))
    in_specs = [col_spec(cc) for cc in cols] + [pl.BlockSpec((8, LANE), lambda bi, hh: (0, hh))]
    args = [h] * len(cols) + [tab]
    if s0 is not None:
        in_specs.append(st_spec)
        args.append(s0)
    want_state = state_into is not None
    aliases = {}
    if prev is not None:
        aliases[len(args)] = 0
        in_specs.append(pl.BlockSpec(memory_space=pl.ANY))
        args.append(prev)
    if want_state and state_into[2] is not None:
        aliases[len(args)] = 1
        in_specs.append(pl.BlockSpec(memory_space=pl.ANY))
        args.append(state_into[2])
    out_specs = [pl.BlockSpec((nb * t, LANE), lambda bi, hh: (rb0 + bi, hh))]
    out_shape = [jax.ShapeDtypeStruct((h.shape[0], heads * dv), BF16)]
    if want_state:
        layer, depth = state_into[0], state_into[1]
        out_specs.append(pl.BlockSpec((nb, None, 2, 1, dk, dv), lambda bi, hh: (bi, layer, 0, hh, 0, 0)))
        out_shape.append(jax.ShapeDtypeStruct((b, depth, 2, heads, dk, dv), F32))
    est = 2 * nb * t * LANE * (5 * 4 + 2) + 7 * nb * t * LANE * 4 + (8 * nb + 6) * dk * dv * 4 + (4 << 20)
    res = pl.pallas_call(
        functools.partial(_hgrn_kernel, t=t, nb=nb, chunk=chunk, sub=sub, q_scale=dk ** -0.5,
                          has_s0=s0 is not None, want_state=want_state, aliased=len(aliases)),
        grid=(b // nb, heads),
        in_specs=in_specs,
        out_specs=out_specs,
        out_shape=out_shape,
        input_output_aliases=aliases,
        scratch_shapes=[pltpu.VMEM((2 * nb, dv, dk), F32), pltpu.VMEM((2, nb * t, LANE), F32),
                        pltpu.VMEM((2, nb * t, LANE), F32), pltpu.VMEM((2, nb * t, LANE), F32),
                        pltpu.VMEM((nb * t, LANE), F32)],
        compiler_params=_cparams(("arbitrary", "arbitrary"), est),
    )(*args)
    return (res[0], res[1]) if want_state else (res[0], None)


def _rope_lane_tables(t, d):
    n = d // 4
    inv = ROPE_THETA ** (-jnp.arange(n, dtype=F32) / n)
    pos = jnp.arange(t)
    rows = (pos // GRID_W).astype(F32)
    cols = (pos % GRID_W).astype(F32)
    ar = rows[:, None] * inv[None, :]
    ac = cols[:, None] * inv[None, :]
    cos = jnp.concatenate([jnp.cos(ar), jnp.cos(ar), jnp.cos(ac), jnp.cos(ac)], axis=-1)
    sin = jnp.concatenate([-jnp.sin(ar), jnp.sin(ar), -jnp.sin(ac), jnp.sin(ac)], axis=-1)
    reps = LANE // d
    return jnp.tile(cos, (1, reps)), jnp.tile(sin, (1, reps))


def kernel(x_prompt, x_sample, cache_mla_ckv, cache_mla_krope, cache_diff_k, cache_diff_v, cache_gqa_k, cache_gqa_v, state_hgrn, c, c_ctx, w_in, w_out, mla_g_cq, mla_g_ckv, mla_w_uq, mla_w_ukv, diff_lambda, diff_subln_g, gqa_g_q, gqa_g_k, hgrn_gamma, hgrn_g_norm, ada_w, ada_b, ln_g, ln_b, ffn_w1, ffn_w3, ffn_w2, moe_router, moe_w1, moe_w3, moe_w2):
    bc, tc, d = x_prompt.shape
    bl, tl, _ = x_sample.shape
    depth = w_in.shape[0]
    q_rank = mla_g_cq.shape[1]
    kv_rank = mla_g_ckv.shape[1]
    rope_d = cache_mla_krope.shape[-1]
    heads = cache_diff_k.shape[3]
    dd = cache_diff_k.shape[4] // 2
    kv_heads = cache_gqa_k.shape[3]
    gd = cache_gqa_k.shape[4]
    hk = state_hgrn.shape[4]
    hv = state_hgrn.shape[5]
    n_exp = moe_w1.shape[1]
    alpha = (2 * depth) ** 0.25
    n_ctx = bc * tc
    n_lat = bl * tl
    tok = _Tokens(n_ctx, tl, bl, d)
    mla_hw = 2 * LANE
    assert MLA_NOPE_DIM + rope_d <= mla_hw and MLA_NOPE_DIM + MLA_V_DIM == mla_hw

    unit = heads * LANE
    kvw = kv_heads * gd
    assert 2 * dd == LANE and gd == LANE and hk == LANE and hv == LANE
    names = ("cq", "ckv", "kr", "bq", "bk", "bv", "gq", "gk", "gv", "dq", "di", "df", "dfb", "dg")
    widths = (q_rank, kv_rank, rope_d, unit, unit, unit, unit, kvw, kvw, unit, unit, unit, unit, unit)
    assert sum(widths) == w_in.shape[2]
    src, o = {}, 0
    for nm, w in zip(names, widths):
        src[nm] = (o, w)
        o += w
    col, pieces, o = {}, [], 0

    def place(nm, pad_to=None):
        nonlocal o
        if nm is not None:
            s0, w = src[nm]
            col[nm] = o
            pieces.append((s0, w))
            o += w
        if pad_to is not None and o % pad_to:
            z = pad_to - o % pad_to
            pieces.append((None, z))
            o += z

    place("cq")
    place("gk", pad_to=unit)
    place("ckv")
    place("gv")
    place("kr", pad_to=LANE)
    place(None, pad_to=unit)
    for nm in ("bq", "bk", "bv", "gq", "dq", "di", "df", "dfb", "dg"):
        place(nm)
    w_in_b = regroup_columns(w_in, tuple(pieces))
    assert q_rank <= unit and col["gk"] % kvw == 0 and col["gv"] % kvw == 0 and col["ckv"] % kv_rank == 0
    w_uq_b = jnp.pad(mla_w_uq.reshape(depth, q_rank, heads, MLA_NOPE_DIM + rope_d),
                     ((0, 0), (0, 0), (0, 0), (0, mla_hw - MLA_NOPE_DIM - rope_d))
                     ).reshape(depth, q_rank, heads * mla_hw).astype(BF16)
    w_ukv_b = mla_w_ukv.astype(BF16)
    ffn_w2_b = ffn_w2.astype(BF16)
    e_ff = moe_w1.shape[-1]
    n_moe = moe_w1.shape[0]
    moe_w1_r = moe_w1.reshape(n_moe * n_exp, d, e_ff)
    moe_w3_r = moe_w3.reshape(n_moe * n_exp, d, e_ff)
    moe_w2_b = cast_bf16(moe_w2.reshape(n_moe * n_exp * e_ff, d)).reshape(n_moe * n_exp, e_ff, d)
    router_b = jnp.pad(moe_router, ((0, 0), (0, 0), (0, LANE - n_exp))).astype(BF16)

    cond = jnp.concatenate([c_ctx[None, :], c], axis=0)
    n_cond = cond.shape[0]
    cond_p = jnp.pad(jax.nn.silu(cond), ((0, -n_cond % 16), (0, 0))).astype(BF16)
    mods = []
    for l in range(depth):
        ml = matmul_few_rows(cond_p, ada_w, l)[:n_cond] + ada_b[l][None, :]
        mods.append(ml.reshape(n_cond, 6, d).transpose(1, 0, 2))
    mod = jnp.concatenate(mods, axis=0).reshape(depth * 6, n_cond, 1, d)
    ln_gb = jnp.concatenate([ln_g.reshape(depth * 2, 1, d), ln_b.reshape(depth * 2, 1, d)], axis=0)

    def lower_bounds(gamma):
        cs = jnp.cumsum(jax.nn.softmax(gamma.astype(F32), axis=0), axis=0)
        return cs - cs[0:1]

    lb_f = lower_bounds(hgrn_gamma[0])
    lb_b = lower_bounds(hgrn_gamma[1])
    n_rot_small, n_rot_big = dd // 4, gd // 4
    assert rope_d == dd
    rope_small = _rope_lane_tables(tl, dd)
    rope_big = _rope_lane_tables(tl, gd)
    n_tok = n_ctx + n_lat
    past = cache_mla_ckv.shape[2]
    cache_dk = cache_diff_k.reshape(bl, depth, past, unit)
    cache_dv = cache_diff_v.reshape(bl, depth, past, unit)
    cache_gk = cache_gqa_k.reshape(bl, depth, past, kvw)
    cache_gv = cache_gqa_v.reshape(bl, depth, past, kvw)
    hgrn_cols = [col[nm] for nm in ("dq", "di", "df", "dfb", "dg")]

    def mixers(h, q_all, kv_new, kv_cache, l, lat, prevs, new_states=None):
        row0, b, t = (n_ctx, bl, tl) if lat else (0, bc, tc)
        hp = 1 if lat else heads
        rb0 = row0 // t
        flags = dict(use_rope=lat, has_cache=lat, aliased=prevs is not None)
        pa, pb, pc, pd = prevs if prevs is not None else (None,) * 4

        def hcols(c0, width):
            return pl.BlockSpec((t, width), lambda bi, hb: (rb0 + bi, c0 // width + hb))

        def vec(n):
            return pl.BlockSpec((1, n), lambda bi, hb: (0, 0))

        def cache_spec(width, per_head):
            if per_head:
                return pl.BlockSpec((None, None, past, width), lambda bi, hb: (bi, l, 0, hb))
            return pl.BlockSpec((None, None, past, width), lambda bi, hb: (bi, l, 0, 0))

        wide = pl.BlockSpec((t, hp * mla_hw), lambda bi, hb: (rb0 + bi, hb))
        main = [(q_all, wide), (kv_new, wide), (h, pl.BlockSpec((t, LANE), lambda bi, hb: (rb0 + bi, col["kr"] // LANE)))]
        cache = []
        if lat:
            cache = [(kv_cache, pl.BlockSpec((past, hp * mla_hw), lambda bi, hb: (bi, hb))),
                     (cache_mla_krope, cache_spec(rope_d, False))]
        o_a = _attn_launch(
            functools.partial(_mla_fused_kernel, hp=hp, t=t, scale=(MLA_NOPE_DIM + rope_d) ** -0.5, rope_d=rope_d,
                              n_rot=n_rot_small, **flags),
            n_tok, heads, hp, row0, b, t, main, rope_small if lat else None, cache, pa,
            t * hp * mla_hw * 6 + past * hp * mla_hw * 2)

        lam_p = diff_lambda[l].astype(F32)
        lam_init = 0.8 - 0.6 * math.exp(-0.3 * l)
        lam = jnp.exp(jnp.sum(lam_p[0] * lam_p[1])) - jnp.exp(jnp.sum(lam_p[2] * lam_p[3])) + lam_init
        lam_v = jnp.full((1, LANE), lam, F32)
        g_v = (diff_subln_g[l] * (1.0 - lam_init)).reshape(1, LANE)
        w = hp * LANE
        main = [(h, hcols(col["bq"], w)), (h, hcols(col["bk"], w)), (h, hcols(col["bv"], w)),
                (lam_v, vec(LANE)), (g_v, vec(LANE))]
        cache = [(cache_dk, cache_spec(w, True)), (cache_dv, cache_spec(w, True))] if lat else []
        o_b = _attn_launch(
            functools.partial(_diff_fused_kernel, hp=hp, t=t, scale=dd ** -0.5, dd=dd, n_rot=n_rot_small, **flags),
            n_tok, heads, hp, row0, b, t, main, rope_small if lat else None, cache, pb,
            (3 * t + 2 * past) * w * 4)

        kv_group = heads // kv_heads
        if lat:
            kspec = pl.BlockSpec((t, LANE), lambda bi, hb: (rb0 + bi, col["gk"] // LANE + hb // kv_group))
            vspec = pl.BlockSpec((t, LANE), lambda bi, hb: (rb0 + bi, col["gv"] // LANE + hb // kv_group))
            cspec = pl.BlockSpec((None, None, past, LANE), lambda bi, hb: (bi, l, 0, hb // kv_group))
            cache = [(cache_gk, cspec), (cache_gv, cspec)]
        else:
            kspec = pl.BlockSpec((t, kvw), lambda bi, hb: (rb0 + bi, col["gk"] // kvw))
            vspec = pl.BlockSpec((t, kvw), lambda bi, hb: (rb0 + bi, col["gv"] // kvw))
            cache = []
        main = [(h, hcols(col["gq"], w)), (h, kspec), (h, vspec),
                (gqa_g_q[l].reshape(1, gd), vec(gd)), (gqa_g_k[l].reshape(1, gd), vec(gd))]
        o_c = _attn_launch(
            functools.partial(_gqa_fused_kernel, hp=hp, kv_group=kv_group, t=t, scale=gd ** -0.5, n_rot=n_rot_big,
                              **flags),
            n_tok, heads, hp, row0, b, t, main, rope_big if lat else None, cache, pc,
            (t + 2 * t + 2 * past) * w * 4)

        tab = jnp.stack([jnp.log(lb_f[l]), jnp.log1p(-lb_f[l]), 1.0 - lb_f[l],
                         jnp.log(lb_b[l]), jnp.log1p(-lb_b[l]), 1.0 - lb_b[l],
                         jnp.tile(hgrn_g_norm[l], heads), jnp.zeros((heads * hk,), F32)], axis=0)
        o_d, states = hgrn_mix(h, row0, b, t, heads, hk, hv, hgrn_cols, tab, state_hgrn[:, l] if lat else None,
                               None if lat else (l, depth, new_states), pd)
        return (o_a, o_b, o_c, o_d), states

    x = jnp.concatenate([x_prompt.reshape(n_ctx, d), x_sample.reshape(n_lat, d)], axis=0)
    u = modulate(tok, x, mod, 1, 0)
    caches, new_states = None, None
    for l in range(depth):
        h = matmul(u, w_in_b, F32, layer=l)
        q_all = rms_matmul(h, col["cq"], unit, mla_g_cq[l].reshape(1, q_rank), w_uq_b, l, F32)
        kv_new = rms_matmul(h, col["ckv"], kv_rank, mla_g_ckv[l].reshape(1, kv_rank), w_ukv_b, l, BF16)
        kv_cache = matmul(cache_mla_ckv[:, l].reshape(bl * past, kv_rank).astype(BF16), w_ukv_b, BF16, layer=l)
        parts, new_states = mixers(h, q_all, kv_new, kv_cache, l, False, None, new_states)
        parts, _ = mixers(h, q_all, kv_new, kv_cache, l, True, parts)
        caches = write_caches(h, col, l, depth, bc, tc, kv_rank, rope_d, unit, kvw, mla_g_ckv[l].reshape(1, kv_rank),
                              gqa_g_k[l].reshape(1, gd), caches)
        mix = out_proj(parts, w_out, l)
        x, u = residual_ln(tok, x, mix, mod, l * 6 + 2, ln_gb, 2 * l, 2 * depth + 2 * l, alpha,
                           l * 6 + 4, l * 6 + 3)
        j = l // 2
        if l % 2 == 0:
            f = down_proj(swiglu_up(u, ffn_w1, ffn_w3, j), ffn_w2_b, j)
        else:
            logits = matmul(u, router_b, F32, layer=j)[:, :n_exp]
            f = moe_experts(u, logits, moe_w1_r, moe_w3_r, moe_w2_b, j, n_exp)
        if l + 1 < depth:
            x, u = residual_ln(tok, x, f, mod, l * 6 + 5, ln_gb, 2 * l + 1, 2 * depth + 2 * l + 1, alpha,
                               (l + 1) * 6 + 1, (l + 1) * 6 + 0)
        else:
            x, _ = residual_ln(tok, x, f, mod, l * 6 + 5, ln_gb, 2 * l + 1, 2 * depth + 2 * l + 1, alpha)

    y_prompt = x[:n_ctx].reshape(bc, tc, d)
    y_sample = x[n_ctx:].reshape(bl, tl, d)
    new_ckv, new_kr, new_dk, new_dv, new_gk, new_gv = caches
    return (y_prompt, y_sample, new_ckv, new_kr,
            new_dk.reshape(bc, depth, tc, heads, 2 * dd), new_dv.reshape(bc, depth, tc, heads, 2 * dd),
            new_gk.reshape(bc, depth, tc, kv_heads, gd), new_gv.reshape(bc, depth, tc, kv_heads, gd), new_states)
```
